```python
import math
import jax
import jax.numpy as jnp
from jax import lax
import numpy as np

D_MODEL = 1024
BATCH = 16
SEQ = 256
DEPTH = 4
DEC_BATCH = 4
DEC_SEQ = 4096
PAST_LEN = 512

GRID_W = 64
HEAD_DIM = 64
DA_HEADS = 4
DA_WIDTH = DA_HEADS * 2 * HEAD_DIM
NA_HEADS = 8
NA_WIDTH = NA_HEADS * HEAD_DIM
NA_ROWS_MAX = 8
NA_COLS = 16
NA_QUERY_COLS = 16
IN_COLS = 3 * DA_WIDTH + 3 * NA_WIDTH + 2 * D_MODEL
ROPE_BASE = 10000.0
ROPE_PAIRS = HEAD_DIM // 4
QUERY_BLOCK = 128
N_EXPERTS = 16
N_GROUPS = 4
EXPERTS_PER_GROUP = N_EXPERTS // N_GROUPS
TOP_K = 2
D_EXPERT = 256
N_MOD = 6
EPS = 1e-6

kernel_name = "hybrid_diffattn_natten_moe_dit_step"


def rms_norm(x, g):
    xf = x.astype(jnp.float32)
    y = xf * lax.rsqrt(jnp.mean(xf * xf, axis=-1, keepdims=True) + EPS)
    return (y * g.astype(jnp.float32)).astype(x.dtype)


def modulation(cond, w_mod, b_mod):
    m = (jax.nn.silu(cond) @ w_mod + b_mod)[..., None, :]
    return jnp.split(m, N_MOD, axis=-1)


def axial_rope_tables(n_tokens):
    t = jnp.arange(n_tokens)
    pos = jnp.stack([t // GRID_W, t % GRID_W], axis=-1).astype(jnp.float32)
    inv_freq = ROPE_BASE ** (-jnp.arange(ROPE_PAIRS, dtype=jnp.float32) / ROPE_PAIRS)
    ang = pos[:, :, None] * inv_freq
    return jnp.cos(ang), jnp.sin(ang)


def apply_axial_rope(x, cos, sin):
    xs = x.reshape(x.shape[:-1] + (2, 2, ROPE_PAIRS))
    x1, x2 = xs[..., 0, :], xs[..., 1, :]
    cos, sin = cos.astype(x.dtype), sin.astype(x.dtype)
    out = jnp.stack([x1 * cos - x2 * sin, x2 * cos + x1 * sin], axis=-2)
    return out.reshape(x.shape)


def split_query_blocks(q, axis):
    n = q.shape[axis]
    q = q.reshape(q.shape[:axis] + (n // QUERY_BLOCK, QUERY_BLOCK) + q.shape[axis + 1:])
    return jnp.moveaxis(q, axis, 0)


def merge_query_blocks(o, axis):
    o = jnp.moveaxis(o, 0, axis)
    return o.reshape(o.shape[:axis] + (o.shape[axis] * o.shape[axis + 1],) + o.shape[axis + 2:])


def diff_attention(q, k, v, lam, scale):
    s = jnp.einsum('bmhqd,bmhkd->bmhqk', q, k).astype(jnp.float32) * scale
    p = jax.nn.softmax(s, axis=-1)
    a = p[:, 0] - lam * p[:, 1]
    return jnp.einsum('bhqk,bhkd->bhqd', a.astype(v.dtype), v)


def dense_attention(q, k, v, scale):
    s = jnp.einsum('bhqd,bhkd->bhqk', q, k).astype(jnp.float32) * scale
    p = jax.nn.softmax(s, axis=-1).astype(v.dtype)
    return jnp.einsum('bhqk,bhkd->bhqd', p, v)


def neighbourhood_attention(q, k, v, k_ctx, v_ctx, rpb, scale):
    B, H, N, DH = q.shape
    rows = N // GRID_W
    kh = min(NA_ROWS_MAX, rows)
    qr = math.gcd(rows, NA_ROWS_MAX)
    qc = NA_QUERY_COLS
    uh = min(kh + qr - 1, rows)
    uw = min(NA_COLS + qc - 1, GRID_W)
    n_rb, n_cb = rows // qr, GRID_W // qc
    q_rows = jnp.arange(n_rb)[:, None] * qr + jnp.arange(qr)
    q_cols = jnp.arange(n_cb)[:, None] * qc + jnp.arange(qc)
    k_rows = jnp.clip(q_rows[:, :1] - kh // 2, 0, rows - uh) + jnp.arange(uh)
    k_cols = jnp.clip(q_cols[:, :1] - NA_COLS // 2, 0, GRID_W - uw) + jnp.arange(uw)
    r0 = jnp.clip(q_rows - kh // 2, 0, rows - kh)
    c0 = jnp.clip(q_cols - NA_COLS // 2, 0, GRID_W - NA_COLS)
    dr = k_rows[:, None, :] - q_rows[:, :, None]
    dc = k_cols[:, None, :] - q_cols[:, :, None]
    row_ok = (k_rows[:, None, :] >= r0[:, :, None]) & (k_rows[:, None, :] < r0[:, :, None] + kh)
    col_ok = (k_cols[:, None, :] >= c0[:, :, None]) & (k_cols[:, None, :] < c0[:, :, None] + NA_COLS)
    ri = jnp.clip(dr + NA_ROWS_MAX - 1, 0, 2 * NA_ROWS_MAX - 2)[:, None, :, None, :, None]
    ci = jnp.clip(dc + NA_COLS - 1, 0, 2 * NA_COLS - 2)[None, :, None, :, None, :]
    ok = row_ok[:, None, :, None, :, None] & col_ok[None, :, None, :, None, :]
    bias = jnp.where(ok, rpb[:, ri, ci].astype(jnp.float32), -jnp.inf)
    bias = jnp.moveaxis(bias.reshape(H, n_rb, n_cb, qr * qc, uh * uw), 1, 0)
    idx = (k_rows[:, None, :, None] * GRID_W + k_cols[None, :, None, :]).reshape(n_rb, n_cb, uh * uw)
    qb = q.reshape(B, H, n_rb, qr, n_cb, qc, DH).transpose(2, 0, 1, 4, 3, 5, 6)
    qb = qb.reshape(n_rb, B, H, n_cb, qr * qc, DH)
    n_loc = uh * uw

    def row_block(args):
        q_blk, idx_blk, bias_blk = args
        kg = jnp.take(k, idx_blk, axis=2)
        vg = jnp.take(v, idx_blk, axis=2)
        s_loc = jnp.einsum('bhcqd,bhckd->bhcqk', q_blk, kg).astype(jnp.float32) * scale + bias_blk
        s_ctx = jnp.einsum('bhcqd,bhkd->bhcqk', q_blk, k_ctx).astype(jnp.float32) * scale
        p = jax.nn.softmax(jnp.concatenate([s_loc, s_ctx], axis=-1), axis=-1).astype(v.dtype)
        return (jnp.einsum('bhcqk,bhckd->bhcqd', p[..., :n_loc], vg)
                + jnp.einsum('bhcqk,bhkd->bhcqd', p[..., n_loc:], v_ctx))

    o = lax.map(row_block, (qb, idx, bias))
    o = o.reshape(n_rb, B, H, n_cb, qr, qc, DH).transpose(1, 2, 0, 4, 3, 5, 6)
    return o.reshape(B, H, N, DH)


def split_projection(p):
    B, N, _ = p.shape
    offs = [DA_WIDTH, 2 * DA_WIDTH, 3 * DA_WIDTH, 3 * DA_WIDTH + NA_WIDTH,
            3 * DA_WIDTH + 2 * NA_WIDTH, 3 * DA_WIDTH + 3 * NA_WIDTH, 3 * DA_WIDTH + 3 * NA_WIDTH + D_MODEL]
    da_q, da_k, da_v, na_q, na_k, na_v, g_a, g_b = jnp.split(p, offs, axis=-1)
    da_q = da_q.reshape(B, N, 2, DA_HEADS, HEAD_DIM).transpose(0, 2, 3, 1, 4)
    da_k = da_k.reshape(B, N, 2, DA_HEADS, HEAD_DIM).transpose(0, 2, 3, 1, 4)
    da_v = da_v.reshape(B, N, DA_HEADS, 2 * HEAD_DIM).transpose(0, 2, 1, 3)
    heads = lambda t: t.reshape(B, N, NA_HEADS, HEAD_DIM).transpose(0, 2, 1, 3)
    return da_q, da_k, da_v, heads(na_q), heads(na_k), heads(na_v), g_a, g_b


def moe(h, w_router, b_router, w_gate, w_up, w_down):
    scores = jax.nn.sigmoid((h @ w_router).astype(jnp.float32))
    sel = scores + b_router.astype(jnp.float32)
    grp = sel.reshape(sel.shape[:-1] + (N_GROUPS, EXPERTS_PER_GROUP))
    grp_score = lax.top_k(grp, 2)[0].sum(-1)
    best = jnp.argmax(grp_score, axis=-1)
    in_group = (jnp.arange(N_EXPERTS) // EXPERTS_PER_GROUP) == best[..., None]
    _, idx = lax.top_k(jnp.where(in_group, sel, -jnp.inf), TOP_K)
    w = jnp.take_along_axis(scores, idx, axis=-1)
    w = w / jnp.sum(w, axis=-1, keepdims=True)
    combine = jnp.sum(jax.nn.one_hot(idx, N_EXPERTS, dtype=jnp.float32) * w[..., None], axis=-2)
    combine = combine.astype(h.dtype)
    out = jnp.zeros_like(h)
    for e in range(N_EXPERTS):
        hid = jax.nn.silu(h @ w_gate[e]) * (h @ w_up[e])
        out = out + combine[..., e:e + 1] * (hid @ w_down[e])
    return out


def trunk_layer(x, cond, lp, lam_init, ctx_kv=None):
    sh1, sc1, gt1, sh2, sc2, gt2 = modulation(cond, lp['w_mod'], lp['b_mod'])
    h = rms_norm(x, lp['g_norm1']) * (1 + sc1) + sh1
    da_q, da_k, da_v, na_q, na_k, na_v, g_a, g_b = split_projection(h @ lp['w_in'])
    da_q, da_k = rms_norm(da_q, lp['g_q_da']), rms_norm(da_k, lp['g_k_da'])
    na_q, na_k = rms_norm(na_q, lp['g_q_na']), rms_norm(na_k, lp['g_k_na'])
    f32 = jnp.float32
    lam = (jnp.exp(jnp.sum(lp['lam_q1'].astype(f32) * lp['lam_k1'].astype(f32)))
           - jnp.exp(jnp.sum(lp['lam_q2'].astype(f32) * lp['lam_k2'].astype(f32))) + lam_init)
    scale = HEAD_DIM ** -0.5
    if ctx_kv is None:
        o_a = merge_query_blocks(lax.map(lambda qq: diff_attention(qq, da_k, da_v, lam, scale),
                                         split_query_blocks(da_q, 3)), 2)
        o_b = merge_query_blocks(lax.map(lambda qq: dense_attention(qq, na_k, na_v, scale),
                                         split_query_blocks(na_q, 2)), 2)
        new_ctx = (da_k, da_v, na_k, na_v)
    else:
        ctx_da_k, ctx_da_v, ctx_na_k, ctx_na_v = ctx_kv
        cos, sin = axial_rope_tables(x.shape[1])
        da_q, da_k = apply_axial_rope(da_q, cos, sin), apply_axial_rope(da_k, cos, sin)
        k_all = jnp.concatenate([da_k, ctx_da_k], axis=3)
        v_all = jnp.concatenate([da_v, ctx_da_v], axis=2)
        o_a = merge_query_blocks(lax.map(lambda qq: diff_attention(qq, k_all, v_all, lam, scale),
                                         split_query_blocks(da_q, 3)), 2)
        o_b = neighbourhood_attention(na_q, na_k, na_v, ctx_na_k, ctx_na_v, lp['rpb'], scale)
        new_ctx = None
    o_a = rms_norm(o_a, lp['g_subln']) * (1.0 - lam_init)
    B, N = x.shape[0], x.shape[1]
    y_a = o_a.transpose(0, 2, 1, 3).reshape(B, N, DA_WIDTH) @ lp['w_br_a']
    y_b = o_b.transpose(0, 2, 1, 3).reshape(B, N, NA_WIDTH) @ lp['w_br_b']
    mixed = (jax.nn.sigmoid(g_a) * y_a + jax.nn.sigmoid(g_b) * y_b) @ lp['w_out']
    x = x + gt1 * mixed
    h2 = rms_norm(x, lp['g_norm2']) * (1 + sc2) + sh2
    x = x + gt2 * moe(h2, lp['w_router'], lp['b_router'], lp['w_gate'], lp['w_up'], lp['w_down'])
    return x, new_ctx


def setup_inputs(seed: int = 0) -> dict:
    key = jax.random.key(seed)
    ks = jax.random.split(key, 32)
    f32 = jnp.float32

    def nrm(k, shape, s=1.0):
        return jax.random.normal(k, shape, f32) * s

    return {
        'x_prompt': nrm(ks[0], (BATCH, SEQ, D_MODEL)),
        'x_sample': nrm(ks[1], (DEC_BATCH, DEC_SEQ, D_MODEL)),
        'cache_da_k': nrm(ks[2], (DEC_BATCH, DEPTH, 2, DA_HEADS, PAST_LEN, HEAD_DIM)),
        'cache_da_v': nrm(ks[3], (DEC_BATCH, DEPTH, DA_HEADS, PAST_LEN, 2 * HEAD_DIM)),
        'cache_na_k': nrm(ks[4], (DEC_BATCH, DEPTH, NA_HEADS, PAST_LEN, HEAD_DIM)),
        'cache_na_v': nrm(ks[5], (DEC_BATCH, DEPTH, NA_HEADS, PAST_LEN, HEAD_DIM)),
        'c': nrm(ks[6], (DEC_BATCH, D_MODEL)),
        'c_ctx': nrm(ks[7], (D_MODEL,)),
        'w_mod': nrm(ks[8], (DEPTH, D_MODEL, N_MOD * D_MODEL), 0.5 * D_MODEL ** -0.5),
        'b_mod': nrm(ks[9], (DEPTH, N_MOD * D_MODEL), 0.01),
        'g_norm1': 1.0 + nrm(ks[10], (DEPTH, D_MODEL), 0.02),
        'g_norm2': 1.0 + nrm(ks[11], (DEPTH, D_MODEL), 0.02),
        'w_in': nrm(ks[12], (DEPTH, D_MODEL, IN_COLS), D_MODEL ** -0.5),
        'g_q_da': 1.0 + nrm(ks[13], (DEPTH, HEAD_DIM), 0.02),
        'g_k_da': 1.0 + nrm(ks[14], (DEPTH, HEAD_DIM), 0.02),
        'g_q_na': 1.0 + nrm(ks[15], (DEPTH, HEAD_DIM), 0.02),
        'g_k_na': 1.0 + nrm(ks[16], (DEPTH, HEAD_DIM), 0.02),
        'lam_q1': nrm(ks[17], (DEPTH, HEAD_DIM), 0.1),
        'lam_k1': nrm(ks[18], (DEPTH, HEAD_DIM), 0.1),
        'lam_q2': nrm(ks[19], (DEPTH, HEAD_DIM), 0.1),
        'lam_k2': nrm(ks[20], (DEPTH, HEAD_DIM), 0.1),
        'g_subln': 1.0 + nrm(ks[21], (DEPTH, 2 * HEAD_DIM), 0.02),
        'rpb': nrm(ks[22], (DEPTH, NA_HEADS, 2 * NA_ROWS_MAX - 1, 2 * NA_COLS - 1), 0.1),
        'w_br_a': nrm(ks[23], (DEPTH, DA_WIDTH, D_MODEL), DA_WIDTH ** -0.5),
        'w_br_b': nrm(ks[24], (DEPTH, NA_WIDTH, D_MODEL), NA_WIDTH ** -0.5),
        'w_out': nrm(ks[25], (DEPTH, D_MODEL, D_MODEL), D_MODEL ** -0.5),
        'w_router': nrm(ks[26], (D_MODEL, N_EXPERTS), D_MODEL ** -0.5),
        'b_router': nrm(ks[27], (N_EXPERTS,), 0.01),
        'w_gate': nrm(ks[28], (DEPTH, N_EXPERTS, D_MODEL, D_EXPERT), D_MODEL ** -0.5),
        'w_up': nrm(ks[29], (DEPTH, N_EXPERTS, D_MODEL, D_EXPERT), D_MODEL ** -0.5),
        'w_down': nrm(ks[30], (DEPTH, N_EXPERTS, D_EXPERT, D_MODEL), D_EXPERT ** -0.5),
    }


def reference(x_prompt, x_sample, cache_da_k, cache_da_v, cache_na_k, cache_na_v, c, c_ctx,
              w_mod, b_mod, g_norm1, g_norm2, w_in, g_q_da, g_k_da, g_q_na, g_k_na,
              lam_q1, lam_k1, lam_q2, lam_k2, g_subln, rpb, w_br_a, w_br_b, w_out,
              w_router, b_router, w_gate, w_up, w_down):
    y_prompt, y_sample = x_prompt, x_sample
    da_ks, da_vs, na_ks, na_vs = [], [], [], []
    for l in range(DEPTH):
        lp = {'w_mod': w_mod[l], 'b_mod': b_mod[l], 'g_norm1': g_norm1[l], 'g_norm2': g_norm2[l],
              'w_in': w_in[l], 'g_q_da': g_q_da[l], 'g_k_da': g_k_da[l], 'g_q_na': g_q_na[l],
              'g_k_na': g_k_na[l], 'lam_q1': lam_q1[l], 'lam_k1': lam_k1[l], 'lam_q2': lam_q2[l],
              'lam_k2': lam_k2[l], 'g_subln': g_subln[l], 'rpb': rpb[l], 'w_br_a': w_br_a[l],
              'w_br_b': w_br_b[l], 'w_out': w_out[l], 'w_router': w_router, 'b_router': b_router,
              'w_gate': w_gate[l], 'w_up': w_up[l], 'w_down': w_down[l]}
        lam_init = 0.8 - 0.6 * math.exp(-0.3 * l)
        y_prompt, (dk, dv, nk, nv) = trunk_layer(y_prompt, c_ctx, lp, lam_init)
        da_ks.append(dk)
        da_vs.append(dv)
        na_ks.append(nk)
        na_vs.append(nv)
        y_sample, _ = trunk_layer(y_sample, c, lp, lam_init,
                                  (cache_da_k[:, l], cache_da_v[:, l], cache_na_k[:, l], cache_na_v[:, l]))
    new_da_k = jnp.stack(da_ks, axis=1)
    new_da_v = jnp.stack(da_vs, axis=1)
    new_na_k = jnp.stack(na_ks, axis=1)
    new_na_v = jnp.stack(na_vs, axis=1)
    return (y_prompt, y_sample, new_da_k, new_da_v, new_na_k, new_na_v)
```

```python
import functools
import math

import numpy as np
import jax
import jax.numpy as jnp
from jax import lax
from jax.experimental import pallas as pl
from jax.experimental.pallas import tpu as pltpu

D_MODEL = 1024
BATCH = 16
SEQ = 256
DEPTH = 4
DEC_BATCH = 4
DEC_SEQ = 4096
PAST_LEN = 512
GRID_W = 64
GRID_H = DEC_SEQ // GRID_W
HEAD_DIM = 64
DA_HEADS = 4
DA_WIDTH = 512
NA_HEADS = 8
NA_WIDTH = 512
NA_ROWS = 8
NA_COLS = 16
IN_COLS = 3 * DA_WIDTH + 3 * NA_WIDTH + 2 * D_MODEL
ROPE_BASE = 10000.0
ROPE_PAIRS = HEAD_DIM // 4
N_EXPERTS = 16
N_GROUPS = 4
EXPERTS_PER_GROUP = 4
D_EXPERT = 256
N_MOD = 6
EPS = 1e-6
ATTN_SCALE = HEAD_DIM ** -0.5

N_SEG = 1 + DEC_BATCH
SEG_ROWS = DEC_SEQ
LANES = 128
CHUNK = 512

VMEM_LIMIT = 56 * 1024 * 1024

F32 = jnp.float32
BF16 = jnp.bfloat16


def _dot(a, b):
    return jnp.dot(a, b, preferred_element_type=F32)


def _dot_nt(a, b):
    return lax.dot_general(a, b, (((1,), (1,)), ((), ())), preferred_element_type=F32)


def _params(*sem):
    return pltpu.CompilerParams(dimension_semantics=sem, vmem_limit_bytes=VMEM_LIMIT)


def _mod_kernel(cond_ref, w_ref, b_ref, o_ref):
    c = cond_ref[...]
    sc = (c * jax.nn.sigmoid(c)).astype(BF16)
    o_ref[0] = _dot(sc, w_ref[0].astype(BF16)) + b_ref[0]


def _modulation(cond8, w_mod, b_mod):
    tn = 1536
    return pl.pallas_call(
        _mod_kernel,
        out_shape=jax.ShapeDtypeStruct((DEPTH, 8, N_MOD * D_MODEL), F32),
        grid=(DEPTH, N_MOD * D_MODEL // tn),
        in_specs=[
            pl.BlockSpec((8, D_MODEL), lambda l, j: (0, 0)),
            pl.BlockSpec((1, D_MODEL, tn), lambda l, j: (l, 0, j)),
            pl.BlockSpec((1, 1, tn), lambda l, j: (l, 0, j)),
        ],
        out_specs=pl.BlockSpec((1, 8, tn), lambda l, j: (l, 0, j)),
        compiler_params=_params("arbitrary", "arbitrary"),
        name="modulation",
    )(cond8, w_mod, b_mod.reshape(DEPTH, 1, N_MOD * D_MODEL))


def _inproj_kernel(x_ref, mod_ref, g1_ref, w_ref, hg_ref, bd_ref, rope_ref, qkv_ref, gate_ref, kvf_ref):
    s = pl.program_id(0)
    x = x_ref[0]
    ms = jnp.mean(x * x, axis=-1, keepdims=True)
    y = x * lax.rsqrt(ms + EPS) * g1_ref[0]
    mod = mod_ref[0]
    h = (y * (1.0 + mod[1:2]) + mod[0:1]).astype(BF16)

    rope = rope_ref[0]
    cos4 = jnp.concatenate([rope[:, 0:LANES]] * 4, axis=1)
    sin_up4 = jnp.concatenate([rope[:, LANES:2 * LANES]] * 4, axis=1)
    sin_dn4 = jnp.concatenate([rope[:, 2 * LANES:3 * LANES]] * 4, axis=1)

    def head_norm(acc, row):
        ssum = _dot((acc * acc).astype(BF16), bd_ref[...])
        return acc * lax.rsqrt(ssum * (1.0 / HEAD_DIM) + EPS) * hg_ref[0, row:row + 1, :]

    def rope_rot(t):
        return (t * cos4 + pltpu.roll(t, CHUNK - ROPE_PAIRS, 1) * sin_up4
                + pltpu.roll(t, ROPE_PAIRS, 1) * sin_dn4)

    def proj(c):
        return _dot(h, w_ref[0, :, c * CHUNK:(c + 1) * CHUNK])

    is_ctx = s == 0

    q = rope_rot(head_norm(proj(0), 0))
    qkv_ref[0, :, 0:CHUNK] = q.astype(BF16)

    k = rope_rot(head_norm(proj(1), 1))
    qkv_ref[0, :, CHUNK:2 * CHUNK] = k.astype(BF16)

    @pl.when(is_ctx)
    def _():
        kvf_ref[0, :, 0:CHUNK] = k

    v = proj(2)
    qkv_ref[0, :, 2 * CHUNK:3 * CHUNK] = v.astype(BF16)

    @pl.when(is_ctx)
    def _():
        kvf_ref[0, :, CHUNK:2 * CHUNK] = v

    q = head_norm(proj(3), 2)
    qkv_ref[0, :, 3 * CHUNK:4 * CHUNK] = q.astype(BF16)

    k = head_norm(proj(4), 3)
    qkv_ref[0, :, 4 * CHUNK:5 * CHUNK] = k.astype(BF16)

    @pl.when(is_ctx)
    def _():
        kvf_ref[0, :, 2 * CHUNK:3 * CHUNK] = k

    v = proj(5)
    qkv_ref[0, :, 5 * CHUNK:6 * CHUNK] = v.astype(BF16)

    @pl.when(is_ctx)
    def _():
        kvf_ref[0, :, 3 * CHUNK:4 * CHUNK] = v

    for c in range(4):
        g = proj(6 + c)
        gate_ref[0, :, c * CHUNK:(c + 1) * CHUNK] = jax.nn.sigmoid(g).astype(BF16)


def _in_projection(l, x_all, mods, g_norm1, w_in_b, head_gains, bd, rope_tab):
    tm = 256
    nt = SEG_ROWS // tm
    return pl.pallas_call(
        _inproj_kernel,
        out_shape=(
            jax.ShapeDtypeStruct((N_SEG, SEG_ROWS, 6 * CHUNK), BF16),
            jax.ShapeDtypeStruct((N_SEG, SEG_ROWS, 2 * D_MODEL), BF16),
            jax.ShapeDtypeStruct((1, SEG_ROWS, 4 * CHUNK), F32),
        ),
        grid=(N_SEG, nt),
        in_specs=[
            pl.BlockSpec((1, tm, D_MODEL), lambda s, i: (s, i, 0)),
            pl.BlockSpec((1, N_MOD, D_MODEL), lambda s, i: (s, 0, 0)),
            pl.BlockSpec((1, 1, D_MODEL), lambda s, i: (l, 0, 0)),
            pl.BlockSpec((1, D_MODEL, IN_COLS), lambda s, i: (l, 0, 0)),
            pl.BlockSpec((1, 4, CHUNK), lambda s, i: (l, 0, 0)),
            pl.BlockSpec((CHUNK, CHUNK), lambda s, i: (0, 0)),
            pl.BlockSpec((1, tm, 3 * LANES), lambda s, i: (jnp.minimum(s, 1), i, 0)),
        ],
        out_specs=(
            pl.BlockSpec((1, tm, 6 * CHUNK), lambda s, i: (s, i, 0)),
            pl.BlockSpec((1, tm, 2 * D_MODEL), lambda s, i: (s, i, 0)),
            pl.BlockSpec((1, tm, 4 * CHUNK), lambda s, i: (0, jnp.where(s == 0, i, nt - 1), 0)),
        ),
        compiler_params=_params("arbitrary", "arbitrary"),
        name="in_projection",
    )(x_all, mods, g_norm1, w_in_b, head_gains, bd, rope_tab)


def _lam_value(lam_ref):
    lp = lam_ref[0]
    s1 = jnp.sum(lp[0:1] * lp[1:2], axis=-1, keepdims=True)
    s2 = jnp.sum(lp[2:3] * lp[3:4], axis=-1, keepdims=True)
    lam_init = lp[4:5, 0:1]
    return jnp.exp(s1) - jnp.exp(s2) + lam_init, 1.0 - lam_init


def _half_masks():
    lane = lax.broadcasted_iota(jnp.int32, (1, LANES), 1)
    return (lane < HEAD_DIM, lane >= HEAD_DIM)


def _pick_head(mask, t):
    return jnp.where(mask, t, jnp.zeros_like(t)) * jnp.asarray(ATTN_SCALE, t.dtype)


def _sub_layer_norm(o, gsub, one_minus):
    ms = jnp.mean(o * o, axis=-1, keepdims=True)
    return o * lax.rsqrt(ms + EPS) * gsub * one_minus


def _attn_ctx_kernel(qda_ref, qna_ref, kv_ref, lam_ref, gsub_ref, o_ref):
    lam, one_minus = _lam_value(lam_ref)
    masks = _half_masks()
    qda = qda_ref[0]
    qna = qna_ref[0]
    gsub = gsub_ref[0]

    def kv(c0):
        return kv_ref[0, :, c0:c0 + LANES].astype(BF16)

    def softmax_parts(sc):
        m = jnp.max(sc, axis=-1, keepdims=True)
        e = jnp.exp(sc - m)
        return e, 1.0 / jnp.sum(e, axis=-1, keepdims=True)

    for h in range(DA_HEADS):
        p, sub = divmod(h, 2)
        q1 = _pick_head(masks[sub], qda[:, p * LANES:(p + 1) * LANES])
        q2 = _pick_head(masks[sub], qda[:, 2 * LANES + p * LANES:2 * LANES + (p + 1) * LANES])
        e1, r1 = softmax_parts(_dot_nt(q1, kv(p * LANES)))
        e2, r2 = softmax_parts(_dot_nt(q2, kv(2 * LANES + p * LANES)))
        a = e1 * r1 - e2 * (lam * r2)
        o = _dot(a.astype(BF16), kv(CHUNK + h * LANES))
        o_ref[:, h * LANES:(h + 1) * LANES] = _sub_layer_norm(o, gsub, one_minus).astype(BF16)

    for p in range(NA_HEADS // 2):
        kp = kv(2 * CHUNK + p * LANES)
        vp = kv(3 * CHUNK + p * LANES)
        both = None
        for sub in range(2):
            qm = _pick_head(masks[sub], qna[:, p * LANES:(p + 1) * LANES])
            e, r = softmax_parts(_dot_nt(qm, kp))
            o = _dot(e.astype(BF16), vp) * r
            both = o if sub == 0 else jnp.where(masks[1], o, both)
        o_ref[:, DA_WIDTH + p * LANES:DA_WIDTH + (p + 1) * LANES] = both.astype(BF16)


def _attention_ctx(l, qkv, kvf, lam_tab, g_subln):
    return pl.pallas_call(
        _attn_ctx_kernel,
        out_shape=jax.ShapeDtypeStruct((SEG_ROWS, D_MODEL), BF16),
        grid=(BATCH,),
        in_specs=[
            pl.BlockSpec((1, SEQ, CHUNK), lambda b: (0, b, 0)),
            pl.BlockSpec((1, SEQ, CHUNK), lambda b: (0, b, 3)),
            pl.BlockSpec((1, SEQ, 4 * CHUNK), lambda b: (0, b, 0)),
            pl.BlockSpec((1, 8, LANES), lambda b: (l, 0, 0)),
            pl.BlockSpec((1, 1, LANES), lambda b: (l, 0, 0)),
        ],
        out_specs=pl.BlockSpec((SEQ, D_MODEL), lambda b: (b, 0)),
        compiler_params=_params("arbitrary"),
        name="attention_ctx",
    )(qkv, qkv, kvf, lam_tab, g_subln)


def _attn_da_kernel(q_ref, k_ref, v_ref, kc_ref, vc_ref, lam_ref, gsub_ref, o_ref):
    lam, one_minus = _lam_value(lam_ref)
    masks = _half_masks()
    q = q_ref[0]
    gsub = gsub_ref[0]

    def softmax_parts(s_lat, s_ctx):
        m = jnp.maximum(jnp.max(s_lat, axis=-1, keepdims=True), jnp.max(s_ctx, axis=-1, keepdims=True))
        e_lat = jnp.exp(s_lat - m)
        e_ctx = jnp.exp(s_ctx - m)
        tot = jnp.sum(e_lat, axis=-1, keepdims=True) + jnp.sum(e_ctx, axis=-1, keepdims=True)
        return e_lat, e_ctx, 1.0 / tot

    for h in range(DA_HEADS):
        p, sub = divmod(h, 2)
        c1 = p * LANES
        c2 = 2 * LANES + p * LANES
        q1 = _pick_head(masks[sub], q[:, c1:c1 + LANES])
        q2 = _pick_head(masks[sub], q[:, c2:c2 + LANES])
        e1l, e1c, r1 = softmax_parts(_dot_nt(q1, k_ref[0, :, c1:c1 + LANES]),
                                     _dot_nt(q1, kc_ref[0, 0, :, c1:c1 + LANES]))
        e2l, e2c, r2 = softmax_parts(_dot_nt(q2, k_ref[0, :, c2:c2 + LANES]),
                                     _dot_nt(q2, kc_ref[0, 0, :, c2:c2 + LANES]))
        w2 = lam * r2
        a_lat = (e1l * r1 - e2l * w2).astype(BF16)
        a_ctx = (e1c * r1 - e2c * w2).astype(BF16)
        hv = h * LANES
        o = _dot(a_lat, v_ref[0, :, hv:hv + LANES]) + _dot(a_ctx, vc_ref[0, 0, :, hv:hv + LANES])
        o_ref[0, :, hv:hv + LANES] = _sub_layer_norm(o, gsub, one_minus).astype(BF16)


def _attention_da(l, qkv, ctx_k, ctx_v, lam_tab, g_subln):
    tq = 256
    return pl.pallas_call(
        _attn_da_kernel,
        out_shape=jax.ShapeDtypeStruct((DEC_BATCH, DEC_SEQ, DA_WIDTH), BF16),
        grid=(DEC_BATCH, DEC_SEQ // tq),
        in_specs=[
            pl.BlockSpec((1, tq, CHUNK), lambda b, i: (b + 1, i, 0)),
            pl.BlockSpec((1, DEC_SEQ, CHUNK), lambda b, i: (b + 1, 0, 1)),
            pl.BlockSpec((1, DEC_SEQ, CHUNK), lambda b, i: (b + 1, 0, 2)),
            pl.BlockSpec((1, 1, PAST_LEN, CHUNK), lambda b, i: (l, b, 0, 0)),
            pl.BlockSpec((1, 1, PAST_LEN, CHUNK), lambda b, i: (l, b, 0, 0)),
            pl.BlockSpec((1, 8, LANES), lambda b, i: (l, 0, 0)),
            pl.BlockSpec((1, 1, LANES), lambda b, i: (l, 0, 0)),
        ],
        out_specs=pl.BlockSpec((1, tq, DA_WIDTH), lambda b, i: (b, i, 0)),
        compiler_params=_params("arbitrary", "arbitrary"),
        name="attention_da",
    )(qkv, qkv, qkv, ctx_k, ctx_v, lam_tab, g_subln)


NA_QROWS = 8
NA_KROWS = 16


def _attn_na_kernel(q_ref, k_ref, v_ref, kc_ref, vc_ref, bias_ref, o_ref):
    i = pl.program_id(1)
    masks = _half_masks()
    q = q_ref[0]
    row0 = jnp.clip(i * NA_QROWS - NA_ROWS // 2, 0, GRID_H - NA_KROWS)
    start = pl.multiple_of(row0 * GRID_W, 256)
    nk = NA_KROWS * GRID_W

    for p in range(NA_HEADS // 2):
        c0 = p * LANES
        kw = k_ref[0, pl.ds(start, nk), c0:c0 + LANES]
        vw = v_ref[0, pl.ds(start, nk), c0:c0 + LANES]
        kc = kc_ref[0, 0, :, c0:c0 + LANES]
        vc = vc_ref[0, 0, :, c0:c0 + LANES]
        both = None
        for sub in range(2):
            qm = _pick_head(masks[sub], q[:, c0:c0 + LANES])
            s_loc = _dot_nt(qm, kw) + bias_ref[0, 0, 2 * p + sub].astype(F32)
            s_ctx = _dot_nt(qm, kc)
            m = jnp.maximum(jnp.max(s_loc, axis=-1, keepdims=True), jnp.max(s_ctx, axis=-1, keepdims=True))
            e_loc = jnp.exp(s_loc - m)
            e_ctx = jnp.exp(s_ctx - m)
            tot = jnp.sum(e_loc, axis=-1, keepdims=True) + jnp.sum(e_ctx, axis=-1, keepdims=True)
            o = (_dot(e_loc.astype(BF16), vw) + _dot(e_ctx.astype(BF16), vc)) * (1.0 / tot)
            both = o if sub == 0 else jnp.where(masks[1], o, both)
        o_ref[0, :, c0:c0 + LANES] = both.astype(BF16)


def _attention_na(l, qkv, ctx_k, ctx_v, bias):
    tq = NA_QROWS * GRID_W
    n_i = DEC_SEQ // tq

    def bias_idx(b, i):
        return (l, jnp.where(i == 0, 0, jnp.where(i == n_i - 1, 2, 1)), 0, 0, 0)

    return pl.pallas_call(
        _attn_na_kernel,
        out_shape=jax.ShapeDtypeStruct((DEC_BATCH, DEC_SEQ, NA_WIDTH), BF16),
        grid=(DEC_BATCH, n_i),
        in_specs=[
            pl.BlockSpec((1, tq, CHUNK), lambda b, i: (b + 1, i, 3)),
            pl.BlockSpec((1, DEC_SEQ, CHUNK), lambda b, i: (b + 1, 0, 4)),
            pl.BlockSpec((1, DEC_SEQ, CHUNK), lambda b, i: (b + 1, 0, 5)),
            pl.BlockSpec((1, 1, PAST_LEN, CHUNK), lambda b, i: (l, b, 0, 0)),
            pl.BlockSpec((1, 1, PAST_LEN, CHUNK), lambda b, i: (l, b, 0, 0)),
            pl.BlockSpec((1, 1, NA_HEADS, tq, NA_KROWS * GRID_W), bias_idx),
        ],
        out_specs=pl.BlockSpec((1, tq, NA_WIDTH), lambda b, i: (b, i, 0)),
        compiler_params=_params("arbitrary", "arbitrary"),
        name="attention_na",
    )(qkv, qkv, qkv, ctx_k, ctx_v, bias)


def _na_bias_tables(rpb):
    qr = np.arange(NA_QROWS)
    qc = np.arange(GRID_W)
    kr = np.arange(NA_KROWS)
    kc = np.arange(GRID_W)
    c0 = np.clip(qc - NA_COLS // 2, 0, GRID_W - NA_COLS)
    col_ok = (kc[None, :] >= c0[:, None]) & (kc[None, :] < c0[:, None] + NA_COLS)
    ci = np.clip(kc[None, :] - qc[:, None] + NA_COLS - 1, 0, 2 * NA_COLS - 2)
    tiles = []
    for i in (0, 1, GRID_H // NA_QROWS - 1):
        rows_q = NA_QROWS * i + qr
        rows_k = min(max(NA_QROWS * i - NA_ROWS // 2, 0), GRID_H - NA_KROWS) + kr
        r0 = np.clip(rows_q - NA_ROWS // 2, 0, GRID_H - NA_ROWS)
        row_ok = (rows_k[None, :] >= r0[:, None]) & (rows_k[None, :] < r0[:, None] + NA_ROWS)
        ri = np.clip(rows_k[None, :] - rows_q[:, None] + NA_ROWS - 1, 0, 2 * NA_ROWS - 2)
        ok = row_ok[:, None, :, None] & col_ok[None, :, None, :]
        vals = rpb[:, :, ri[:, None, :, None], ci[None, :, None, :]]
        tile = jnp.where(ok, vals, -jnp.inf)
        tiles.append(tile.reshape(DEPTH, NA_HEADS, NA_QROWS * GRID_W, NA_KROWS * GRID_W))
    return jnp.stack(tiles, axis=1).astype(BF16)


def _out_kernel(x_ref, mod_ref, oc_ref, oda_ref, ona_ref, gate_ref, wa_ref, wb_ref, wo_ref, g2_ref,
                wr_ref, br_ref, x1_ref, h2_ref, comb_ref):
    is_ctx = pl.program_id(0) == 0
    oc = oc_ref[...]
    oa = jnp.where(is_ctx, oc[:, :DA_WIDTH], oda_ref[0])
    ob = jnp.where(is_ctx, oc[:, DA_WIDTH:], ona_ref[0])
    ya = _dot(oa, wa_ref[0])
    yb = _dot(ob, wb_ref[0])
    g = gate_ref[0]
    mixed = (g[:, :D_MODEL].astype(F32) * ya + g[:, D_MODEL:].astype(F32) * yb).astype(BF16)
    mod = mod_ref[0]
    x1 = x_ref[0] + mod[2:3] * _dot(mixed, wo_ref[0])
    x1_ref[0] = x1

    ms = jnp.mean(x1 * x1, axis=-1, keepdims=True)
    h2 = x1 * lax.rsqrt(ms + EPS) * g2_ref[0] * (1.0 + mod[4:5]) + mod[3:4]
    h2_hi = h2.astype(BF16)
    h2_ref[0] = h2_hi

    h2_lo = (h2 - h2_hi.astype(F32)).astype(BF16)
    wr = wr_ref[...]
    wr_hi = wr.astype(BF16)
    wr_lo = (wr - wr_hi.astype(F32)).astype(BF16)
    logits = _dot(h2_hi, wr_hi) + (_dot(h2_hi, wr_lo) + _dot(h2_lo, wr_hi))
    scores = jax.nn.sigmoid(logits)
    sel = scores + br_ref[...]

    lane = lax.broadcasted_iota(jnp.int32, sel.shape, 1).astype(F32)
    neg = -jnp.inf
    far = float(LANES)
    best = None
    for grp in range(N_GROUPS):
        lo = float(grp * EXPERTS_PER_GROUP)
        in_grp = (lane >= lo) & (lane < lo + EXPERTS_PER_GROUP)
        v = jnp.where(in_grp, sel, neg)
        m1 = jnp.max(v, axis=-1, keepdims=True)
        i1 = jnp.min(jnp.where(v == m1, lane, far), axis=-1, keepdims=True)
        v2 = jnp.where(lane == i1, neg, v)
        m2 = jnp.max(v2, axis=-1, keepdims=True)
        i2 = jnp.min(jnp.where(v2 == m2, lane, far), axis=-1, keepdims=True)
        gs = m1 + m2
        if best is None:
            best = (gs, i1, i2)
        else:
            upd = gs > best[0]
            best = (jnp.where(upd, gs, best[0]), jnp.where(upd, i1, best[1]), jnp.where(upd, i2, best[2]))
    _, i1, i2 = best
    hit1 = lane == i1
    hit2 = lane == i2
    s1 = jnp.sum(jnp.where(hit1, scores, 0.0), axis=-1, keepdims=True)
    s2 = jnp.sum(jnp.where(hit2, scores, 0.0), axis=-1, keepdims=True)
    den = s1 + s2
    comb_ref[0] = jnp.where(hit1, s1 / den, 0.0) + jnp.where(hit2, s2 / den, 0.0)


def _output_stage(l, x_all, mods, o_ctx, o_da, o_na, gates, wa_b, wb_b, wo_b, g_norm2, w_router_p, b_router_p):
    tm = 256
    nt = SEG_ROWS // tm

    def lat_idx(s, i):
        return (jnp.maximum(s - 1, 0), jnp.where(s == 0, 0, i), 0)

    return pl.pallas_call(
        _out_kernel,
        out_shape=(
            jax.ShapeDtypeStruct((N_SEG, SEG_ROWS, D_MODEL), F32),
            jax.ShapeDtypeStruct((N_SEG, SEG_ROWS, D_MODEL), BF16),
            jax.ShapeDtypeStruct((N_SEG, SEG_ROWS, LANES), F32),
        ),
        grid=(N_SEG, nt),
        in_specs=[
            pl.BlockSpec((1, tm, D_MODEL), lambda s, i: (s, i, 0)),
            pl.BlockSpec((1, N_MOD, D_MODEL), lambda s, i: (s, 0, 0)),
            pl.BlockSpec((tm, D_MODEL), lambda s, i: (jnp.where(s == 0, i, nt - 1), 0)),
            pl.BlockSpec((1, tm, DA_WIDTH), lat_idx),
            pl.BlockSpec((1, tm, NA_WIDTH), lat_idx),
            pl.BlockSpec((1, tm, 2 * D_MODEL), lambda s, i: (s, i, 0)),
            pl.BlockSpec((1, DA_WIDTH, D_MODEL), lambda s, i: (l, 0, 0)),
            pl.BlockSpec((1, NA_WIDTH, D_MODEL), lambda s, i: (l, 0, 0)),
            pl.BlockSpec((1, D_MODEL, D_MODEL), lambda s, i: (l, 0, 0)),
            pl.BlockSpec((1, 1, D_MODEL), lambda s, i: (l, 0, 0)),
            pl.BlockSpec((D_MODEL, LANES), lambda s, i: (0, 0)),
            pl.BlockSpec((1, LANES), lambda s, i: (0, 0)),
        ],
        out_specs=(
            pl.BlockSpec((1, tm, D_MODEL), lambda s, i: (s, i, 0)),
            pl.BlockSpec((1, tm, D_MODEL), lambda s, i: (s, i, 0)),
            pl.BlockSpec((1, tm, LANES), lambda s, i: (s, i, 0)),
        ),
        compiler_params=_params("arbitrary", "arbitrary"),
        name="output_router",
    )(x_all, mods, o_ctx, o_da, o_na, gates, wa_b, wb_b, wo_b, g_norm2, w_router_p, b_router_p)


MOE_CHUNK = EXPERTS_PER_GROUP * D_EXPERT


def _moe_kernel(h_ref, comb_ref, x1_ref, mod_ref, wg_ref, wu_ref, wd_ref, o_ref):
    h = h_ref[0]
    comb = comb_ref[0]
    acc = jnp.zeros(x1_ref.shape[1:], F32)
    for c in range(N_EXPERTS * D_EXPERT // MOE_CHUNK):
        cols = slice(c * MOE_CHUNK, (c + 1) * MOE_CHUNK)
        g = _dot(h, wg_ref[0, :, cols])
        u = _dot(h, wu_ref[0, :, cols])
        hid = g * jax.nn.sigmoid(g) * u
        parts = []
        for j in range(EXPERTS_PER_GROUP):
            e = c * EXPERTS_PER_GROUP + j
            parts.append(hid[:, j * D_EXPERT:(j + 1) * D_EXPERT] * comb[:, e:e + 1])
        hs = jnp.concatenate(parts, axis=1).astype(BF16)
        acc = acc + _dot(hs, wd_ref[0, cols, :])
    o_ref[0] = x1_ref[0] + mod_ref[0][5:6] * acc


def _experts(l, h2, comb, x1, mods, wg_all, wu_all, wd_all):
    tm = 256
    hidden = N_EXPERTS * D_EXPERT
    once = pl.Buffered(1)
    return pl.pallas_call(
        _moe_kernel,
        out_shape=jax.ShapeDtypeStruct((N_SEG, SEG_ROWS, D_MODEL), F32),
        grid=(N_SEG, SEG_ROWS // tm),
        in_specs=[
            pl.BlockSpec((1, tm, D_MODEL), lambda s, i: (s, i, 0)),
            pl.BlockSpec((1, tm, LANES), lambda s, i: (s, i, 0)),
            pl.BlockSpec((1, tm, D_MODEL), lambda s, i: (s, i, 0)),
            pl.BlockSpec((1, N_MOD, D_MODEL), lambda s, i: (s, 0, 0)),
            pl.BlockSpec((1, D_MODEL, hidden), lambda s, i: (l, 0, 0), pipeline_mode=once),
            pl.BlockSpec((1, D_MODEL, hidden), lambda s, i: (l, 0, 0), pipeline_mode=once),
            pl.BlockSpec((1, hidden, D_MODEL), lambda s, i: (l, 0, 0), pipeline_mode=once),
        ],
        out_specs=pl.BlockSpec((1, tm, D_MODEL), lambda s, i: (s, i, 0)),
        compiler_params=_params("arbitrary", "arbitrary"),
        name="experts",
    )(h2, comb, x1, mods, wg_all, wu_all, wd_all)


def _rope_tables():
    t = np.arange(DEC_SEQ)
    pos = np.stack([t // GRID_W, t % GRID_W], axis=-1).astype(np.float32)
    inv_freq = (ROPE_BASE ** (-np.arange(ROPE_PAIRS, dtype=np.float32) / ROPE_PAIRS)).astype(np.float32)
    ang = pos[:, :, None] * inv_freq
    lane = np.arange(LANES)
    axis = (lane % HEAD_DIM) // (2 * ROPE_PAIRS)
    pair = lane % ROPE_PAIRS
    second = ((lane // ROPE_PAIRS) % 2).astype(bool)
    a = ang[:, axis, pair]
    cos = np.cos(a)
    sin = np.sin(a)
    lat = np.concatenate([cos, np.where(second, 0.0, -sin), np.where(second, sin, 0.0)], axis=1)
    ident = np.concatenate([np.ones_like(cos), np.zeros_like(cos), np.zeros_like(cos)], axis=1)
    return jnp.asarray(np.stack([ident, lat]).astype(np.float32))


def _head_block_diag():
    r = np.arange(CHUNK) // HEAD_DIM
    return jnp.asarray((r[:, None] == r[None, :]).astype(np.float32), dtype=BF16)


def kernel(x_prompt, x_sample, cache_da_k, cache_da_v, cache_na_k, cache_na_v, c, c_ctx, w_mod, b_mod, g_norm1,
           g_norm2, w_in, g_q_da, g_k_da, g_q_na, g_k_na, lam_q1, lam_k1, lam_q2, lam_k2, g_subln, rpb, w_br_a,
           w_br_b, w_out, w_router, b_router, w_gate, w_up, w_down):
    L = DEPTH
    x_all = jnp.concatenate([x_prompt.reshape(1, SEG_ROWS, D_MODEL), x_sample], axis=0)
    cond8 = jnp.concatenate([c_ctx[None], c, jnp.zeros((8 - N_SEG, D_MODEL), F32)], axis=0)
    w_in_b = w_in.astype(BF16)
    wa_b = w_br_a.astype(BF16)
    wb_b = w_br_b.astype(BF16)
    wo_b = w_out.astype(BF16)
    hidden = N_EXPERTS * D_EXPERT
    wg_all = w_gate.astype(BF16).transpose(0, 2, 1, 3).reshape(L, D_MODEL, hidden)
    wu_all = w_up.astype(BF16).transpose(0, 2, 1, 3).reshape(L, D_MODEL, hidden)
    wd_all = w_down.astype(BF16).reshape(L, hidden, D_MODEL)
    w_router_p = jnp.pad(w_router, ((0, 0), (0, LANES - N_EXPERTS)))
    b_router_p = jnp.pad(b_router, (0, LANES - N_EXPERTS)).reshape(1, LANES)
    head_gains = jnp.stack([jnp.tile(g, (1, CHUNK // HEAD_DIM)) for g in (g_q_da, g_k_da, g_q_na, g_k_na)],
                           axis=1).reshape(L, 4, CHUNK)
    g_norm1 = g_norm1.reshape(L, 1, D_MODEL)
    g_norm2 = g_norm2.reshape(L, 1, D_MODEL)
    g_subln = g_subln.reshape(L, 1, LANES)
    pad64 = lambda t: jnp.pad(t, ((0, 0), (0, LANES - HEAD_DIM)))
    lam_inits = jnp.asarray([0.8 - 0.6 * math.exp(-0.3 * l) for l in range(L)], F32)
    lam_tab = jnp.stack([pad64(lam_q1), pad64(lam_k1), pad64(lam_q2), pad64(lam_k2),
                         jnp.broadcast_to(lam_inits[:, None], (L, LANES)),
                         jnp.zeros((L, LANES), F32), jnp.zeros((L, LANES), F32), jnp.zeros((L, LANES), F32)],
                        axis=1)
    ctx_da_k = cache_da_k.transpose(1, 0, 4, 2, 3, 5).reshape(L, DEC_BATCH, PAST_LEN, DA_WIDTH).astype(BF16)
    ctx_da_v = cache_da_v.transpose(1, 0, 3, 2, 4).reshape(L, DEC_BATCH, PAST_LEN, DA_WIDTH).astype(BF16)
    ctx_na_k = cache_na_k.transpose(1, 0, 3, 2, 4).reshape(L, DEC_BATCH, PAST_LEN, NA_WIDTH).astype(BF16)
    ctx_na_v = cache_na_v.transpose(1, 0, 3, 2, 4).reshape(L, DEC_BATCH, PAST_LEN, NA_WIDTH).astype(BF16)
    na_bias = _na_bias_tables(rpb)
    rope_tab = _rope_tables()
    bd = _head_block_diag()

    mods_all = _modulation(cond8, w_mod, b_mod)[:, :N_SEG].reshape(L, N_SEG, N_MOD, D_MODEL)

    kv_layers = []
    for l in range(L):
        mods = mods_all[l]
        qkv, gates, kvf = _in_projection(l, x_all, mods, g_norm1, w_in_b, head_gains, bd, rope_tab)
        o_ctx = _attention_ctx(l, qkv, kvf, lam_tab, g_subln)
        o_da = _attention_da(l, qkv, ctx_da_k, ctx_da_v, lam_tab, g_subln)
        o_na = _attention_na(l, qkv, ctx_na_k, ctx_na_v, na_bias)
        x1, h2, comb = _output_stage(l, x_all, mods, o_ctx, o_da, o_na, gates, wa_b, wb_b, wo_b, g_norm2,
                                     w_router_p, b_router_p)
        x_all = _experts(l, h2, comb, x1, mods, wg_all, wu_all, wd_all)
        kv_layers.append(kvf[0])

    y_prompt = x_all[0].reshape(BATCH, SEQ, D_MODEL)
    y_sample = x_all[1:]
    kv = jnp.stack(kv_layers, axis=1).reshape(BATCH, SEQ, L, 4 * CHUNK).transpose(0, 2, 1, 3)
    new_da_k = kv[..., 0:CHUNK].reshape(BATCH, L, SEQ, 2, DA_HEADS, HEAD_DIM).transpose(0, 1, 3, 4, 2, 5)
    new_da_v = kv[..., CHUNK:2 * CHUNK].reshape(BATCH, L, SEQ, DA_HEADS, 2 * HEAD_DIM).transpose(0, 1, 3, 2, 4)
    new_na_k = kv[..., 2 * CHUNK:3 * CHUNK].reshape(BATCH, L, SEQ, NA_HEADS, HEAD_DIM).transpose(0, 1, 3, 2, 4)
    new_na_v = kv[..., 3 * CHUNK:4 * CHUNK].reshape(BATCH, L, SEQ, NA_HEADS, HEAD_DIM).transpose(0, 1, 3, 2, 4)
    return (y_prompt, y_sample, new_da_k, new_da_v, new_na_k, new_na_v)
```

```python
import functools
import math

import numpy as np
import jax
import jax.numpy as jnp
from jax import lax
from jax.experimental import pallas as pl
from jax.experimental.pallas import tpu as pltpu

D_MODEL = 1024
BATCH = 16
SEQ = 256
DEPTH = 4
DEC_BATCH = 4
DEC_SEQ = 4096
PAST_LEN = 512
GRID_W = 64
GRID_H = DEC_SEQ // GRID_W
HEAD_DIM = 64
DA_HEADS = 4
DA_WIDTH = 512
NA_HEADS = 8
NA_WIDTH = 512
NA_ROWS = 8
NA_COLS = 16
IN_COLS = 3 * DA_WIDTH + 3 * NA_WIDTH + 2 * D_MODEL
ROPE_BASE = 10000.0
ROPE_PAIRS = HEAD_DIM // 4
N_EXPERTS = 16
N_GROUPS = 4
EXPERTS_PER_GROUP = 4
D_EXPERT = 256
N_MOD = 6
EPS = 1e-6
ATTN_SCALE = HEAD_DIM ** -0.5

N_SEG = 1 + DEC_BATCH
SEG_ROWS = DEC_SEQ
LANES = 128
CHUNK = 512

VMEM_LIMIT = 56 * 1024 * 1024

F32 = jnp.float32
BF16 = jnp.bfloat16


def _dot(a, b):
    return jnp.dot(a, b, preferred_element_type=F32)


def _dot_nt(a, b):
    return lax.dot_general(a, b, (((1,), (1,)), ((), ())), preferred_element_type=F32)


def _params(*sem):
    return pltpu.CompilerParams(dimension_semantics=sem, vmem_limit_bytes=VMEM_LIMIT)


def _mod_kernel(cond_ref, w_ref, b_ref, o_ref):
    c = cond_ref[...]
    sc = (c * jax.nn.sigmoid(c)).astype(BF16)
    o_ref[0] = _dot(sc, w_ref[0].astype(BF16)) + b_ref[0]


def _modulation(cond8, w_mod, b_mod):
    tn = 1536
    return pl.pallas_call(
        _mod_kernel,
        out_shape=jax.ShapeDtypeStruct((DEPTH, 8, N_MOD * D_MODEL), F32),
        grid=(DEPTH, N_MOD * D_MODEL // tn),
        in_specs=[
            pl.BlockSpec((8, D_MODEL), lambda l, j: (0, 0)),
            pl.BlockSpec((1, D_MODEL, tn), lambda l, j: (l, 0, j)),
            pl.BlockSpec((1, 1, tn), lambda l, j: (l, 0, j)),
        ],
        out_specs=pl.BlockSpec((1, 8, tn), lambda l, j: (l, 0, j)),
        compiler_params=_params("arbitrary", "arbitrary"),
        name="modulation",
    )(cond8, w_mod, b_mod.reshape(DEPTH, 1, N_MOD * D_MODEL))


def _inproj_kernel(x_ref, mod_ref, g1_ref, w_ref, hg_ref, bd_ref, rope_ref, qkv_ref, gate_ref, kvf_ref):
    s = pl.program_id(0)
    x = x_ref[0]
    ms = jnp.mean(x * x, axis=-1, keepdims=True)
    y = x * lax.rsqrt(ms + EPS) * g1_ref[0]
    mod = mod_ref[0]
    h = (y * (1.0 + mod[1:2]) + mod[0:1]).astype(BF16)

    rope = rope_ref[0]
    cos4 = jnp.concatenate([rope[:, 0:LANES]] * 4, axis=1)
    sin_up4 = jnp.concatenate([rope[:, LANES:2 * LANES]] * 4, axis=1)
    sin_dn4 = jnp.concatenate([rope[:, 2 * LANES:3 * LANES]] * 4, axis=1)

    def head_norm(acc, row):
        ssum = _dot((acc * acc).astype(BF16), bd_ref[...])
        return acc * lax.rsqrt(ssum * (1.0 / HEAD_DIM) + EPS) * hg_ref[0, row:row + 1, :]

    def rope_rot(t):
        return (t * cos4 + pltpu.roll(t, CHUNK - ROPE_PAIRS, 1) * sin_up4
                + pltpu.roll(t, ROPE_PAIRS, 1) * sin_dn4)

    def proj(c):
        return _dot(h, w_ref[0, :, c * CHUNK:(c + 1) * CHUNK])

    is_ctx = s == 0

    q = rope_rot(head_norm(proj(0), 0))
    qkv_ref[0, :, 0:CHUNK] = q.astype(BF16)

    k = rope_rot(head_norm(proj(1), 1))
    qkv_ref[0, :, CHUNK:2 * CHUNK] = k.astype(BF16)

    @pl.when(is_ctx)
    def _():
        kvf_ref[0, :, 0:CHUNK] = k

    v = proj(2)
    qkv_ref[0, :, 2 * CHUNK:3 * CHUNK] = v.astype(BF16)

    @pl.when(is_ctx)
    def _():
        kvf_ref[0, :, CHUNK:2 * CHUNK] = v

    q = head_norm(proj(3), 2)
    qkv_ref[0, :, 3 * CHUNK:4 * CHUNK] = q.astype(BF16)

    k = head_norm(proj(4), 3)
    qkv_ref[0, :, 4 * CHUNK:5 * CHUNK] = k.astype(BF16)

    @pl.when(is_ctx)
    def _():
        kvf_ref[0, :, 2 * CHUNK:3 * CHUNK] = k

    v = proj(5)
    qkv_ref[0, :, 5 * CHUNK:6 * CHUNK] = v.astype(BF16)

    @pl.when(is_ctx)
    def _():
        kvf_ref[0, :, 3 * CHUNK:4 * CHUNK] = v

    for c in range(4):
        g = proj(6 + c)
        gate_ref[0, :, c * CHUNK:(c + 1) * CHUNK] = jax.nn.sigmoid(g).astype(BF16)


def _in_projection(l, x_all, mods, g_norm1, w_in_b, head_gains, bd, rope_tab):
    tm = 256
    nt = SEG_ROWS // tm
    return pl.pallas_call(
        _inproj_kernel,
        out_shape=(
            jax.ShapeDtypeStruct((N_SEG, SEG_ROWS, 6 * CHUNK), BF16),
            jax.ShapeDtypeStruct((N_SEG, SEG_ROWS, 2 * D_MODEL), BF16),
            jax.ShapeDtypeStruct((1, SEG_ROWS, 4 * CHUNK), F32),
        ),
        grid=(N_SEG, nt),
        in_specs=[
            pl.BlockSpec((1, tm, D_MODEL), lambda s, i: (s, i, 0)),
            pl.BlockSpec((1, N_MOD, D_MODEL), lambda s, i: (s, 0, 0)),
            pl.BlockSpec((1, 1, D_MODEL), lambda s, i: (l, 0, 0)),
            pl.BlockSpec((1, D_MODEL, IN_COLS), lambda s, i: (l, 0, 0)),
            pl.BlockSpec((1, 4, CHUNK), lambda s, i: (l, 0, 0)),
            pl.BlockSpec((CHUNK, CHUNK), lambda s, i: (0, 0)),
            pl.BlockSpec((1, tm, 3 * LANES), lambda s, i: (jnp.minimum(s, 1), i, 0)),
        ],
        out_specs=(
            pl.BlockSpec((1, tm, 6 * CHUNK), lambda s, i: (s, i, 0)),
            pl.BlockSpec((1, tm, 2 * D_MODEL), lambda s, i: (s, i, 0)),
            pl.BlockSpec((1, tm, 4 * CHUNK), lambda s, i: (0, jnp.where(s == 0, i, nt - 1), 0)),
        ),
        compiler_params=_params("arbitrary", "arbitrary"),
        name="in_projection",
    )(x_all, mods, g_norm1, w_in_b, head_gains, bd, rope_tab)


def _lam_value(lam_ref):
    lp = lam_ref[0]
    s1 = jnp.sum(lp[0:1] * lp[1:2], axis=-1, keepdims=True)
    s2 = jnp.sum(lp[2:3] * lp[3:4], axis=-1, keepdims=True)
    lam_init = lp[4:5, 0:1]
    return jnp.exp(s1) - jnp.exp(s2) + lam_init, 1.0 - lam_init


def _half_masks():
    lane = lax.broadcasted_iota(jnp.int32, (1, LANES), 1)
    return (lane < HEAD_DIM, lane >= HEAD_DIM)


def _pick_head(mask, t):
    return jnp.where(mask, t, jnp.zeros_like(t)) * jnp.asarray(ATTN_SCALE, t.dtype)


def _sub_layer_norm(o, gsub, one_minus):
    ms = jnp.mean(o * o, axis=-1, keepdims=True)
    return o * lax.rsqrt(ms + EPS) * gsub * one_minus


def _attn_ctx_kernel(qda_ref, qna_ref, kv_ref, lam_ref, gsub_ref, o_ref):
    lam, one_minus = _lam_value(lam_ref)
    masks = _half_masks()
    qda = qda_ref[0]
    qna = qna_ref[0]
    gsub = gsub_ref[0]

    def kv(c0):
        return kv_ref[0, :, c0:c0 + LANES].astype(BF16)

    def softmax_parts(sc):
        m = jnp.max(sc, axis=-1, keepdims=True)
        e = jnp.exp(sc - m)
        return e, 1.0 / jnp.sum(e, axis=-1, keepdims=True)

    for h in range(DA_HEADS):
        p, sub = divmod(h, 2)
        q1 = _pick_head(masks[sub], qda[:, p * LANES:(p + 1) * LANES])
        q2 = _pick_head(masks[sub], qda[:, 2 * LANES + p * LANES:2 * LANES + (p + 1) * LANES])
        e1, r1 = softmax_parts(_dot_nt(q1, kv(p * LANES)))
        e2, r2 = softmax_parts(_dot_nt(q2, kv(2 * LANES + p * LANES)))
        a = e1 * r1 - e2 * (lam * r2)
        o = _dot(a.astype(BF16), kv(CHUNK + h * LANES))
        o_ref[:, h * LANES:(h + 1) * LANES] = _sub_layer_norm(o, gsub, one_minus).astype(BF16)

    for p in range(NA_HEADS // 2):
        kp = kv(2 * CHUNK + p * LANES)
        vp = kv(3 * CHUNK + p * LANES)
        both = None
        for sub in range(2):
            qm = _pick_head(masks[sub], qna[:, p * LANES:(p + 1) * LANES])
            e, r = softmax_parts(_dot_nt(qm, kp))
            o = _dot(e.astype(BF16), vp) * r
            both = o if sub == 0 else jnp.where(masks[1], o, both)
        o_ref[:, DA_WIDTH + p * LANES:DA_WIDTH + (p + 1) * LANES] = both.astype(BF16)


def _attention_ctx(l, qkv, kvf, lam_tab, g_subln):
    return pl.pallas_call(
        _attn_ctx_kernel,
        out_shape=jax.ShapeDtypeStruct((SEG_ROWS, D_MODEL), BF16),
        grid=(BATCH,),
        in_specs=[
            pl.BlockSpec((1, SEQ, CHUNK), lambda b: (0, b, 0)),
            pl.BlockSpec((1, SEQ, CHUNK), lambda b: (0, b, 3)),
            pl.BlockSpec((1, SEQ, 4 * CHUNK), lambda b: (0, b, 0)),
            pl.BlockSpec((1, 8, LANES), lambda b: (l, 0, 0)),
            pl.BlockSpec((1, 1, LANES), lambda b: (l, 0, 0)),
        ],
        out_specs=pl.BlockSpec((SEQ, D_MODEL), lambda b: (b, 0)),
        compiler_params=_params("arbitrary"),
        name="attention_ctx",
    )(qkv, qkv, kvf, lam_tab, g_subln)


def _attn_da_kernel(q_ref, k_ref, v_ref, kc_ref, vc_ref, lam_ref, gsub_ref, o_ref):
    lam, one_minus = _lam_value(lam_ref)
    masks = _half_masks()
    q = q_ref[0]
    gsub = gsub_ref[0]

    def softmax_parts(s_lat, s_ctx):
        m = jnp.maximum(jnp.max(s_lat, axis=-1, keepdims=True), jnp.max(s_ctx, axis=-1, keepdims=True))
        e_lat = jnp.exp(s_lat - m)
        e_ctx = jnp.exp(s_ctx - m)
        tot = jnp.sum(e_lat, axis=-1, keepdims=True) + jnp.sum(e_ctx, axis=-1, keepdims=True)
        return e_lat, e_ctx, 1.0 / tot

    for h in range(DA_HEADS):
        p, sub = divmod(h, 2)
        c1 = p * LANES
        c2 = 2 * LANES + p * LANES
        q1 = _pick_head(masks[sub], q[:, c1:c1 + LANES])
        q2 = _pick_head(masks[sub], q[:, c2:c2 + LANES])
        e1l, e1c, r1 = softmax_parts(_dot_nt(q1, k_ref[0, :, c1:c1 + LANES]),
                                     _dot_nt(q1, kc_ref[0, 0, :, c1:c1 + LANES]))
        e2l, e2c, r2 = softmax_parts(_dot_nt(q2, k_ref[0, :, c2:c2 + LANES]),
                                     _dot_nt(q2, kc_ref[0, 0, :, c2:c2 + LANES]))
        w2 = lam * r2
        a_lat = (e1l * r1 - e2l * w2).astype(BF16)
        a_ctx = (e1c * r1 - e2c * w2).astype(BF16)
        hv = h * LANES
        o = _dot(a_lat, v_ref[0, :, hv:hv + LANES]) + _dot(a_ctx, vc_ref[0, 0, :, hv:hv + LANES])
        o_ref[0, :, hv:hv + LANES] = _sub_layer_norm(o, gsub, one_minus).astype(BF16)


def _attention_da(l, qkv, ctx_k, ctx_v, lam_tab, g_subln):
    tq = 256
    return pl.pallas_call(
        _attn_da_kernel,
        out_shape=jax.ShapeDtypeStruct((DEC_BATCH, DEC_SEQ, DA_WIDTH), BF16),
        grid=(DEC_BATCH, DEC_SEQ // tq),
        in_specs=[
            pl.BlockSpec((1, tq, CHUNK), lambda b, i: (b + 1, i, 0)),
            pl.BlockSpec((1, DEC_SEQ, CHUNK), lambda b, i: (b + 1, 0, 1)),
            pl.BlockSpec((1, DEC_SEQ, CHUNK), lambda b, i: (b + 1, 0, 2)),
            pl.BlockSpec((1, 1, PAST_LEN, CHUNK), lambda b, i: (l, b, 0, 0)),
            pl.BlockSpec((1, 1, PAST_LEN, CHUNK), lambda b, i: (l, b, 0, 0)),
            pl.BlockSpec((1, 8, LANES), lambda b, i: (l, 0, 0)),
            pl.BlockSpec((1, 1, LANES), lambda b, i: (l, 0, 0)),
        ],
        out_specs=pl.BlockSpec((1, tq, DA_WIDTH), lambda b, i: (b, i, 0)),
        compiler_params=_params("arbitrary", "arbitrary"),
        name="attention_da",
    )(qkv, qkv, qkv, ctx_k, ctx_v, lam_tab, g_subln)


NA_QROWS = 8
NA_KROWS = 16


def _attn_na_kernel(q_ref, k_ref, v_ref, kc_ref, vc_ref, bias_ref, o_ref):
    i = pl.program_id(1)
    masks = _half_masks()
    q = q_ref[0]
    row0 = jnp.clip(i * NA_QROWS - NA_ROWS // 2, 0, GRID_H - NA_KROWS)
    start = pl.multiple_of(row0 * GRID_W, 256)
    nk = NA_KROWS * GRID_W

    for p in range(NA_HEADS // 2):
        c0 = p * LANES
        kw = k_ref[0, pl.ds(start, nk), c0:c0 + LANES]
        vw = v_ref[0, pl.ds(start, nk), c0:c0 + LANES]
        kc = kc_ref[0, 0, :, c0:c0 + LANES]
        vc = vc_ref[0, 0, :, c0:c0 + LANES]
        both = None
        for sub in range(2):
            qm = _pick_head(masks[sub], q[:, c0:c0 + LANES])
            s_loc = _dot_nt(qm, kw) + bias_ref[0, 0, 2 * p + sub].astype(F32)
            s_ctx = _dot_nt(qm, kc)
            m = jnp.maximum(jnp.max(s_loc, axis=-1, keepdims=True), jnp.max(s_ctx, axis=-1, keepdims=True))
            e_loc = jnp.exp(s_loc - m)
            e_ctx = jnp.exp(s_ctx - m)
            tot = jnp.sum(e_loc, axis=-1, keepdims=True) + jnp.sum(e_ctx, axis=-1, keepdims=True)
            o = (_dot(e_loc.astype(BF16), vw) + _dot(e_ctx.astype(BF16), vc)) * (1.0 / tot)
            both = o if sub == 0 else jnp.where(masks[1], o, both)
        o_ref[0, :, c0:c0 + LANES] = both.astype(BF16)


def _attention_na(l, qkv, ctx_k, ctx_v, bias):
    tq = NA_QROWS * GRID_W
    n_i = DEC_SEQ // tq

    def bias_idx(b, i):
        return (l, jnp.where(i == 0, 0, jnp.where(i == n_i - 1, 2, 1)), 0, 0, 0)

    return pl.pallas_call(
        _attn_na_kernel,
        out_shape=jax.ShapeDtypeStruct((DEC_BATCH, DEC_SEQ, NA_WIDTH), BF16),
        grid=(DEC_BATCH, n_i),
        in_specs=[
            pl.BlockSpec((1, tq, CHUNK), lambda b, i: (b + 1, i, 3)),
            pl.BlockSpec((1, DEC_SEQ, CHUNK), lambda b, i: (b + 1, 0, 4)),
            pl.BlockSpec((1, DEC_SEQ, CHUNK), lambda b, i: (b + 1, 0, 5)),
            pl.BlockSpec((1, 1, PAST_LEN, CHUNK), lambda b, i: (l, b, 0, 0)),
            pl.BlockSpec((1, 1, PAST_LEN, CHUNK), lambda b, i: (l, b, 0, 0)),
            pl.BlockSpec((1, 1, NA_HEADS, tq, NA_KROWS * GRID_W), bias_idx),
        ],
        out_specs=pl.BlockSpec((1, tq, NA_WIDTH), lambda b, i: (b, i, 0)),
        compiler_params=_params("arbitrary", "arbitrary"),
        name="attention_na",
    )(qkv, qkv, qkv, ctx_k, ctx_v, bias)


def _na_bias_tables(rpb):
    qr = np.arange(NA_QROWS)
    qc = np.arange(GRID_W)
    kr = np.arange(NA_KROWS)
    kc = np.arange(GRID_W)
    c0 = np.clip(qc - NA_COLS // 2, 0, GRID_W - NA_COLS)
    col_ok = (kc[None, :] >= c0[:, None]) & (kc[None, :] < c0[:, None] + NA_COLS)
    cpad = GRID_W - NA_COLS
    rpb_c = jnp.pad(rpb, ((0, 0), (0, 0), (0, 0), (cpad, cpad)))
    by_col = jnp.stack([rpb_c[..., GRID_W - 1 - c:2 * GRID_W - 1 - c] for c in range(GRID_W)], axis=-2)
    by_col = jnp.where(col_ok, by_col, -jnp.inf)
    rpad = NA_ROWS
    by_col = jnp.pad(by_col, ((0, 0), (0, 0), (rpad, rpad), (0, 0), (0, 0)))
    tiles = []
    for i in (0, 1, GRID_H // NA_QROWS - 1):
        rows_q = NA_QROWS * i + qr
        row0_k = min(max(NA_QROWS * i - NA_ROWS // 2, 0), GRID_H - NA_KROWS)
        rows_k = row0_k + kr
        r0 = np.clip(rows_q - NA_ROWS // 2, 0, GRID_H - NA_ROWS)
        row_ok = (rows_k[None, :] >= r0[:, None]) & (rows_k[None, :] < r0[:, None] + NA_ROWS)
        strips = []
        for r in range(NA_QROWS):
            lo = row0_k - int(rows_q[r]) + NA_ROWS - 1 + rpad
            strips.append(by_col[:, :, lo:lo + NA_KROWS])
        tile = jnp.stack(strips, axis=2).transpose(0, 1, 2, 4, 3, 5)
        tile = jnp.where(row_ok[:, None, :, None], tile, -jnp.inf)
        tiles.append(tile.reshape(DEPTH, NA_HEADS, NA_QROWS * GRID_W, NA_KROWS * GRID_W).astype(BF16))
    return jnp.stack(tiles, axis=1)


def _out_kernel(x_ref, mod_ref, oc_ref, oda_ref, ona_ref, gate_ref, wa_ref, wb_ref, wo_ref, g2_ref,
                wr_ref, br_ref, x1_ref, h2_ref, comb_ref):
    is_ctx = pl.program_id(0) == 0
    oc = oc_ref[...]
    oa = jnp.where(is_ctx, oc[:, :DA_WIDTH], oda_ref[0])
    ob = jnp.where(is_ctx, oc[:, DA_WIDTH:], ona_ref[0])
    ya = _dot(oa, wa_ref[0])
    yb = _dot(ob, wb_ref[0])
    g = gate_ref[0]
    mixed = (g[:, :D_MODEL].astype(F32) * ya + g[:, D_MODEL:].astype(F32) * yb).astype(BF16)
    mod = mod_ref[0]
    x1 = x_ref[0] + mod[2:3] * _dot(mixed, wo_ref[0])
    x1_ref[0] = x1

    ms = jnp.mean(x1 * x1, axis=-1, keepdims=True)
    h2 = x1 * lax.rsqrt(ms + EPS) * g2_ref[0] * (1.0 + mod[4:5]) + mod[3:4]
    h2_hi = h2.astype(BF16)
    h2_ref[0] = h2_hi

    h2_lo = (h2 - h2_hi.astype(F32)).astype(BF16)
    wr = wr_ref[...]
    wr_hi = wr.astype(BF16)
    wr_lo = (wr - wr_hi.astype(F32)).astype(BF16)
    logits = _dot(h2_hi, wr_hi) + (_dot(h2_hi, wr_lo) + _dot(h2_lo, wr_hi))
    scores = jax.nn.sigmoid(logits)
    sel = scores + br_ref[...]

    lane = lax.broadcasted_iota(jnp.int32, sel.shape, 1).astype(F32)
    neg = -jnp.inf
    far = float(LANES)
    best = None
    for grp in range(N_GROUPS):
        lo = float(grp * EXPERTS_PER_GROUP)
        in_grp = (lane >= lo) & (lane < lo + EXPERTS_PER_GROUP)
        v = jnp.where(in_grp, sel, neg)
        m1 = jnp.max(v, axis=-1, keepdims=True)
        i1 = jnp.min(jnp.where(v == m1, lane, far), axis=-1, keepdims=True)
        v2 = jnp.where(lane == i1, neg, v)
        m2 = jnp.max(v2, axis=-1, keepdims=True)
        i2 = jnp.min(jnp.where(v2 == m2, lane, far), axis=-1, keepdims=True)
        gs = m1 + m2
        if best is None:
            best = (gs, i1, i2)
        else:
            upd = gs > best[0]
            best = (jnp.where(upd, gs, best[0]), jnp.where(upd, i1, best[1]), jnp.where(upd, i2, best[2]))
    _, i1, i2 = best
    hit1 = lane == i1
    hit2 = lane == i2
    s1 = jnp.sum(jnp.where(hit1, scores, 0.0), axis=-1, keepdims=True)
    s2 = jnp.sum(jnp.where(hit2, scores, 0.0), axis=-1, keepdims=True)
    den = s1 + s2
    comb_ref[0] = jnp.where(hit1, s1 / den, 0.0) + jnp.where(hit2, s2 / den, 0.0)


def _output_stage(l, x_all, mods, o_ctx, o_da, o_na, gates, wa_b, wb_b, wo_b, g_norm2, w_router_p, b_router_p):
    tm = 256
    nt = SEG_ROWS // tm

    def lat_idx(s, i):
        return (jnp.maximum(s - 1, 0), jnp.where(s == 0, 0, i), 0)

    return pl.pallas_call(
        _out_kernel,
        out_shape=(
            jax.ShapeDtypeStruct((N_SEG, SEG_ROWS, D_MODEL), F32),
            jax.ShapeDtypeStruct((N_SEG, SEG_ROWS, D_MODEL), BF16),
            jax.ShapeDtypeStruct((N_SEG, SEG_ROWS, LANES), F32),
        ),
        grid=(N_SEG, nt),
        in_specs=[
            pl.BlockSpec((1, tm, D_MODEL), lambda s, i: (s, i, 0)),
            pl.BlockSpec((1, N_MOD, D_MODEL), lambda s, i: (s, 0, 0)),
            pl.BlockSpec((tm, D_MODEL), lambda s, i: (jnp.where(s == 0, i, nt - 1), 0)),
            pl.BlockSpec((1, tm, DA_WIDTH), lat_idx),
            pl.BlockSpec((1, tm, NA_WIDTH), lat_idx),
            pl.BlockSpec((1, tm, 2 * D_MODEL), lambda s, i: (s, i, 0)),
            pl.BlockSpec((1, DA_WIDTH, D_MODEL), lambda s, i: (l, 0, 0)),
            pl.BlockSpec((1, NA_WIDTH, D_MODEL), lambda s, i: (l, 0, 0)),
            pl.BlockSpec((1, D_MODEL, D_MODEL), lambda s, i: (l, 0, 0)),
            pl.BlockSpec((1, 1, D_MODEL), lambda s, i: (l, 0, 0)),
            pl.BlockSpec((D_MODEL, LANES), lambda s, i: (0, 0)),
            pl.BlockSpec((1, LANES), lambda s, i: (0, 0)),
        ],
        out_specs=(
            pl.BlockSpec((1, tm, D_MODEL), lambda s, i: (s, i, 0)),
            pl.BlockSpec((1, tm, D_MODEL), lambda s, i: (s, i, 0)),
            pl.BlockSpec((1, tm, LANES), lambda s, i: (s, i, 0)),
        ),
        compiler_params=_params("arbitrary", "arbitrary"),
        name="output_router",
    )(x_all, mods, o_ctx, o_da, o_na, gates, wa_b, wb_b, wo_b, g_norm2, w_router_p, b_router_p)


MOE_CHUNK = EXPERTS_PER_GROUP * D_EXPERT


def _moe_kernel(h_ref, comb_ref, x1_ref, mod_ref, wg_ref, wu_ref, wd_ref, o_ref):
    h = h_ref[0]
    comb = comb_ref[0]
    acc = jnp.zeros(x1_ref.shape[1:], F32)
    for c in range(N_GROUPS):
        parts = []
        for j in range(EXPERTS_PER_GROUP):
            e = c * EXPERTS_PER_GROUP + j
            g = _dot(h, wg_ref[0, e])
            u = _dot(h, wu_ref[0, e])
            parts.append(g * jax.nn.sigmoid(g) * u * comb[:, e:e + 1])
        hs = jnp.concatenate(parts, axis=1).astype(BF16)
        acc = acc + _dot(hs, wd_ref[0, c * MOE_CHUNK:(c + 1) * MOE_CHUNK, :])
    o_ref[0] = x1_ref[0] + mod_ref[0][5:6] * acc


def _experts(l, h2, comb, x1, mods, wg_all, wu_all, wd_all):
    tm = 256
    hidden = N_EXPERTS * D_EXPERT
    once = pl.Buffered(1)
    return pl.pallas_call(
        _moe_kernel,
        out_shape=jax.ShapeDtypeStruct((N_SEG, SEG_ROWS, D_MODEL), F32),
        grid=(N_SEG, SEG_ROWS // tm),
        in_specs=[
            pl.BlockSpec((1, tm, D_MODEL), lambda s, i: (s, i, 0)),
            pl.BlockSpec((1, tm, LANES), lambda s, i: (s, i, 0)),
            pl.BlockSpec((1, tm, D_MODEL), lambda s, i: (s, i, 0)),
            pl.BlockSpec((1, N_MOD, D_MODEL), lambda s, i: (s, 0, 0)),
            pl.BlockSpec((1, N_EXPERTS, D_MODEL, D_EXPERT), lambda s, i: (l, 0, 0, 0), pipeline_mode=once),
            pl.BlockSpec((1, N_EXPERTS, D_MODEL, D_EXPERT), lambda s, i: (l, 0, 0, 0), pipeline_mode=once),
            pl.BlockSpec((1, hidden, D_MODEL), lambda s, i: (l, 0, 0), pipeline_mode=once),
        ],
        out_specs=pl.BlockSpec((1, tm, D_MODEL), lambda s, i: (s, i, 0)),
        compiler_params=_params("arbitrary", "arbitrary"),
        name="experts",
    )(h2, comb, x1, mods, wg_all, wu_all, wd_all)


def _rope_tables():
    t = np.arange(DEC_SEQ)
    pos = np.stack([t // GRID_W, t % GRID_W], axis=-1).astype(np.float32)
    inv_freq = (ROPE_BASE ** (-np.arange(ROPE_PAIRS, dtype=np.float32) / ROPE_PAIRS)).astype(np.float32)
    ang = pos[:, :, None] * inv_freq
    lane = np.arange(LANES)
    axis = (lane % HEAD_DIM) // (2 * ROPE_PAIRS)
    pair = lane % ROPE_PAIRS
    second = ((lane // ROPE_PAIRS) % 2).astype(bool)
    a = ang[:, axis, pair]
    cos = np.cos(a)
    sin = np.sin(a)
    lat = np.concatenate([cos, np.where(second, 0.0, -sin), np.where(second, sin, 0.0)], axis=1)
    ident = np.concatenate([np.ones_like(cos), np.zeros_like(cos), np.zeros_like(cos)], axis=1)
    return jnp.asarray(np.stack([ident, lat]).astype(np.float32))


def _head_block_diag():
    r = np.arange(CHUNK) // HEAD_DIM
    return jnp.asarray((r[:, None] == r[None, :]).astype(np.float32), dtype=BF16)


def kernel(x_prompt, x_sample, cache_da_k, cache_da_v, cache_na_k, cache_na_v, c, c_ctx, w_mod, b_mod, g_norm1,
           g_norm2, w_in, g_q_da, g_k_da, g_q_na, g_k_na, lam_q1, lam_k1, lam_q2, lam_k2, g_subln, rpb, w_br_a,
           w_br_b, w_out, w_router, b_router, w_gate, w_up, w_down):
    L = DEPTH
    x_all = jnp.concatenate([x_prompt.reshape(1, SEG_ROWS, D_MODEL), x_sample], axis=0)
    cond8 = jnp.concatenate([c_ctx[None], c, jnp.zeros((8 - N_SEG, D_MODEL), F32)], axis=0)
    w_in_b = w_in.astype(BF16)
    wa_b = w_br_a.astype(BF16)
    wb_b = w_br_b.astype(BF16)
    wo_b = w_out.astype(BF16)
    hidden = N_EXPERTS * D_EXPERT
    wg_all = w_gate.astype(BF16)
    wu_all = w_up.astype(BF16)
    wd_all = w_down.astype(BF16).reshape(L, hidden, D_MODEL)
    w_router_p = jnp.pad(w_router, ((0, 0), (0, LANES - N_EXPERTS)))
    b_router_p = jnp.pad(b_router, (0, LANES - N_EXPERTS)).reshape(1, LANES)
    head_gains = jnp.stack([jnp.tile(g, (1, CHUNK // HEAD_DIM)) for g in (g_q_da, g_k_da, g_q_na, g_k_na)],
                           axis=1).reshape(L, 4, CHUNK)
    g_norm1 = g_norm1.reshape(L, 1, D_MODEL)
    g_norm2 = g_norm2.reshape(L, 1, D_MODEL)
    g_subln = g_subln.reshape(L, 1, LANES)
    pad64 = lambda t: jnp.pad(t, ((0, 0), (0, LANES - HEAD_DIM)))
    lam_inits = jnp.asarray([0.8 - 0.6 * math.exp(-0.3 * l) for l in range(L)], F32)
    lam_tab = jnp.stack([pad64(lam_q1), pad64(lam_k1), pad64(lam_q2), pad64(lam_k2),
                         jnp.broadcast_to(lam_inits[:, None], (L, LANES)),
                         jnp.zeros((L, LANES), F32), jnp.zeros((L, LANES), F32), jnp.zeros((L, LANES), F32)],
                        axis=1)
    ctx_da_k = cache_da_k.transpose(1, 0, 4, 2, 3, 5).reshape(L, DEC_BATCH, PAST_LEN, DA_WIDTH).astype(BF16)
    ctx_da_v = cache_da_v.transpose(1, 0, 3, 2, 4).reshape(L, DEC_BATCH, PAST_LEN, DA_WIDTH).astype(BF16)
    ctx_na_k = cache_na_k.transpose(1, 0, 3, 2, 4).reshape(L, DEC_BATCH, PAST_LEN, NA_WIDTH).astype(BF16)
    ctx_na_v = cache_na_v.transpose(1, 0, 3, 2, 4).reshape(L, DEC_BATCH, PAST_LEN, NA_WIDTH).astype(BF16)
    na_bias = _na_bias_tables(rpb)
    rope_tab = _rope_tables()
    bd = _head_block_diag()

    mods_all = _modulation(cond8, w_mod, b_mod)[:, :N_SEG].reshape(L, N_SEG, N_MOD, D_MODEL)

    kv_layers = []
    for l in range(L):
        mods = mods_all[l]
        qkv, gates, kvf = _in_projection(l, x_all, mods, g_norm1, w_in_b, head_gains, bd, rope_tab)
        o_ctx = _attention_ctx(l, qkv, kvf, lam_tab, g_subln)
        o_da = _attention_da(l, qkv, ctx_da_k, ctx_da_v, lam_tab, g_subln)
        o_na = _attention_na(l, qkv, ctx_na_k, ctx_na_v, na_bias)
        x1, h2, comb = _output_stage(l, x_all, mods, o_ctx, o_da, o_na, gates, wa_b, wb_b, wo_b, g_norm2,
                                     w_router_p, b_router_p)
        x_all = _experts(l, h2, comb, x1, mods, wg_all, wu_all, wd_all)
        kv_layers.append(kvf[0])

    y_prompt = x_all[0].reshape(BATCH, SEQ, D_MODEL)
    y_sample = x_all[1:]
    kv = jnp.stack(kv_layers, axis=1).reshape(BATCH, SEQ, L, 4 * CHUNK).transpose(0, 2, 1, 3)
    new_da_k = kv[..., 0:CHUNK].reshape(BATCH, L, SEQ, 2, DA_HEADS, HEAD_DIM).transpose(0, 1, 3, 4, 2, 5)
    new_da_v = kv[..., CHUNK:2 * CHUNK].reshape(BATCH, L, SEQ, DA_HEADS, 2 * HEAD_DIM).transpose(0, 1, 3, 2, 4)
    new_na_k = kv[..., 2 * CHUNK:3 * CHUNK].reshape(BATCH, L, SEQ, NA_HEADS, HEAD_DIM).transpose(0, 1, 3, 2, 4)
    new_na_v = kv[..., 3 * CHUNK:4 * CHUNK].reshape(BATCH, L, SEQ, NA_HEADS, HEAD_DIM).transpose(0, 1, 3, 2, 4)
    return (y_prompt, y_sample, new_da_k, new_da_v, new_na_k, new_na_v)
```

```python
import functools
import math

import numpy as np
import jax
import jax.numpy as jnp
from jax import lax
from jax.experimental import pallas as pl
from jax.experimental.pallas import tpu as pltpu

D_MODEL = 1024
BATCH = 16
SEQ = 256
DEPTH = 4
DEC_BATCH = 4
DEC_SEQ = 4096
PAST_LEN = 512
GRID_W = 64
GRID_H = DEC_SEQ // GRID_W
HEAD_DIM = 64
DA_HEADS = 4
DA_WIDTH = 512
NA_HEADS = 8
NA_WIDTH = 512
NA_ROWS = 8
NA_COLS = 16
IN_COLS = 3 * DA_WIDTH + 3 * NA_WIDTH + 2 * D_MODEL
ROPE_BASE = 10000.0
ROPE_PAIRS = HEAD_DIM // 4
N_EXPERTS = 16
N_GROUPS = 4
EXPERTS_PER_GROUP = 4
D_EXPERT = 256
N_MOD = 6
EPS = 1e-6
ATTN_SCALE = HEAD_DIM ** -0.5
Q_PRESCALE = ATTN_SCALE * math.log2(math.e)
LOG2E = math.log2(math.e)

N_SEG = 1 + DEC_BATCH
SEG_ROWS = DEC_SEQ
LANES = 128
CHUNK = 512

VMEM_LIMIT = 56 * 1024 * 1024

F32 = jnp.float32
BF16 = jnp.bfloat16


def _dot(a, b):
    return jnp.dot(a, b, preferred_element_type=F32)


def _dot_nt(a, b):
    return lax.dot_general(a, b, (((1,), (1,)), ((), ())), preferred_element_type=F32)


def _params(*sem):
    return pltpu.CompilerParams(dimension_semantics=sem, vmem_limit_bytes=VMEM_LIMIT)


def _mod_kernel(cond_ref, w_ref, b_ref, o_ref):
    c = cond_ref[...]
    sc = (c * jax.nn.sigmoid(c)).astype(BF16)
    o_ref[0] = _dot(sc, w_ref[0].astype(BF16)) + b_ref[0]


def _modulation(cond8, w_mod, b_mod):
    tn = 1536
    return pl.pallas_call(
        _mod_kernel,
        out_shape=jax.ShapeDtypeStruct((DEPTH, 8, N_MOD * D_MODEL), F32),
        grid=(DEPTH, N_MOD * D_MODEL // tn),
        in_specs=[
            pl.BlockSpec((8, D_MODEL), lambda l, j: (0, 0)),
            pl.BlockSpec((1, D_MODEL, tn), lambda l, j: (l, 0, j)),
            pl.BlockSpec((1, 1, tn), lambda l, j: (l, 0, j)),
        ],
        out_specs=pl.BlockSpec((1, 8, tn), lambda l, j: (l, 0, j)),
        compiler_params=_params("arbitrary", "arbitrary"),
        name="modulation",
    )(cond8, w_mod, b_mod.reshape(DEPTH, 1, N_MOD * D_MODEL))


def _inproj_kernel(x_ref, mod_ref, g1_ref, w_ref, hg_ref, bd_ref, rope_ref, qkv_ref, gate_ref, kvf_ref):
    s = pl.program_id(0)
    x = x_ref[0]
    ms = jnp.mean(x * x, axis=-1, keepdims=True)
    y = x * lax.rsqrt(ms + EPS) * g1_ref[0]
    mod = mod_ref[0]
    h = (y * (1.0 + mod[1:2]) + mod[0:1]).astype(BF16)

    rope = rope_ref[0]
    cos4 = jnp.concatenate([rope[:, 0:LANES]] * 4, axis=1)
    sin_up4 = jnp.concatenate([rope[:, LANES:2 * LANES]] * 4, axis=1)
    sin_dn4 = jnp.concatenate([rope[:, 2 * LANES:3 * LANES]] * 4, axis=1)

    def head_norm(acc, row):
        sq = (acc * acc).astype(BF16)
        half = CHUNK // 2
        ssum = jnp.concatenate([_dot(sq[:, :half], bd_ref[...]), _dot(sq[:, half:], bd_ref[...])], axis=1)
        return acc * lax.rsqrt(ssum * (1.0 / HEAD_DIM) + EPS) * hg_ref[0, row:row + 1, :]

    def rope_rot(t):
        return (t * cos4 + pltpu.roll(t, CHUNK - ROPE_PAIRS, 1) * sin_up4
                + pltpu.roll(t, ROPE_PAIRS, 1) * sin_dn4)

    def proj(c):
        return _dot(h, w_ref[0, :, c * CHUNK:(c + 1) * CHUNK])

    is_ctx = s == 0

    q = rope_rot(head_norm(proj(0), 0)) * Q_PRESCALE
    qkv_ref[0, :, 0:CHUNK] = q.astype(BF16)

    k = rope_rot(head_norm(proj(1), 1))
    qkv_ref[0, :, CHUNK:2 * CHUNK] = k.astype(BF16)

    @pl.when(is_ctx)
    def _():
        kvf_ref[0, :, 0:CHUNK] = k

    v = proj(2)
    qkv_ref[0, :, 2 * CHUNK:3 * CHUNK] = v.astype(BF16)

    @pl.when(is_ctx)
    def _():
        kvf_ref[0, :, CHUNK:2 * CHUNK] = v

    q = head_norm(proj(3), 2) * Q_PRESCALE
    qkv_ref[0, :, 3 * CHUNK:4 * CHUNK] = q.astype(BF16)

    k = head_norm(proj(4), 3)
    qkv_ref[0, :, 4 * CHUNK:5 * CHUNK] = k.astype(BF16)

    @pl.when(is_ctx)
    def _():
        kvf_ref[0, :, 2 * CHUNK:3 * CHUNK] = k

    v = proj(5)
    qkv_ref[0, :, 5 * CHUNK:6 * CHUNK] = v.astype(BF16)

    @pl.when(is_ctx)
    def _():
        kvf_ref[0, :, 3 * CHUNK:4 * CHUNK] = v

    for c in range(4):
        g = proj(6 + c)
        gate_ref[0, :, c * CHUNK:(c + 1) * CHUNK] = jax.nn.sigmoid(g).astype(BF16)


def _in_projection(l, x_all, mods, g_norm1, w_in_b, head_gains, bd, rope_tab):
    tm = 512
    nt = SEG_ROWS // tm
    return pl.pallas_call(
        _inproj_kernel,
        out_shape=(
            jax.ShapeDtypeStruct((N_SEG, SEG_ROWS, 6 * CHUNK), BF16),
            jax.ShapeDtypeStruct((N_SEG, SEG_ROWS, 2 * D_MODEL), BF16),
            jax.ShapeDtypeStruct((1, SEG_ROWS, 4 * CHUNK), F32),
        ),
        grid=(N_SEG, nt),
        in_specs=[
            pl.BlockSpec((1, tm, D_MODEL), lambda s, i: (s, i, 0)),
            pl.BlockSpec((1, N_MOD, D_MODEL), lambda s, i: (s, 0, 0)),
            pl.BlockSpec((1, 1, D_MODEL), lambda s, i: (l, 0, 0)),
            pl.BlockSpec((1, D_MODEL, IN_COLS), lambda s, i: (l, 0, 0), pipeline_mode=pl.Buffered(1)),
            pl.BlockSpec((1, 4, CHUNK), lambda s, i: (l, 0, 0)),
            pl.BlockSpec((CHUNK // 2, CHUNK // 2), lambda s, i: (0, 0)),
            pl.BlockSpec((1, tm, 3 * LANES), lambda s, i: (jnp.minimum(s, 1), i, 0)),
        ],
        out_specs=(
            pl.BlockSpec((1, tm, 6 * CHUNK), lambda s, i: (s, i, 0)),
            pl.BlockSpec((1, tm, 2 * D_MODEL), lambda s, i: (s, i, 0)),
            pl.BlockSpec((1, tm, 4 * CHUNK), lambda s, i: (0, jnp.where(s == 0, i, nt - 1), 0)),
        ),
        compiler_params=_params("arbitrary", "arbitrary"),
        name="in_projection",
    )(x_all, mods, g_norm1, w_in_b, head_gains, bd, rope_tab)


def _lam_value(lam_ref):
    lp = lam_ref[0]
    s1 = jnp.sum(lp[0:1] * lp[1:2], axis=-1, keepdims=True)
    s2 = jnp.sum(lp[2:3] * lp[3:4], axis=-1, keepdims=True)
    lam_init = lp[4:5, 0:1]
    return jnp.exp(s1) - jnp.exp(s2) + lam_init, 1.0 - lam_init


def _half_masks():
    lane = lax.broadcasted_iota(jnp.int32, (1, LANES), 1)
    return (lane < HEAD_DIM, lane >= HEAD_DIM)


def _pick_head(mask, t):
    return jnp.where(mask, t, jnp.zeros_like(t))


def _sub_layer_norm(o, gsub, one_minus):
    ms = jnp.mean(o * o, axis=-1, keepdims=True)
    return o * lax.rsqrt(ms + EPS) * gsub * one_minus


def _attn_ctx_kernel(qda_ref, qna_ref, kv_ref, lam_ref, gsub_ref, o_ref):
    lam, one_minus = _lam_value(lam_ref)
    masks = _half_masks()
    qda = qda_ref[0]
    qna = qna_ref[0]
    gsub = gsub_ref[0]

    def kv(c0):
        return kv_ref[0, :, c0:c0 + LANES].astype(BF16)

    def softmax_parts(sc):
        m = jnp.max(sc, axis=-1, keepdims=True)
        e = jnp.exp2(sc - m)
        return e, 1.0 / jnp.sum(e, axis=-1, keepdims=True)

    for h in range(DA_HEADS):
        p, sub = divmod(h, 2)
        q1 = _pick_head(masks[sub], qda[:, p * LANES:(p + 1) * LANES])
        q2 = _pick_head(masks[sub], qda[:, 2 * LANES + p * LANES:2 * LANES + (p + 1) * LANES])
        e1, r1 = softmax_parts(_dot_nt(q1, kv(p * LANES)))
        e2, r2 = softmax_parts(_dot_nt(q2, kv(2 * LANES + p * LANES)))
        a = e1 * r1 - e2 * (lam * r2)
        o = _dot(a.astype(BF16), kv(CHUNK + h * LANES))
        o_ref[:, h * LANES:(h + 1) * LANES] = _sub_layer_norm(o, gsub, one_minus).astype(BF16)

    for p in range(NA_HEADS // 2):
        kp = kv(2 * CHUNK + p * LANES)
        vp = kv(3 * CHUNK + p * LANES)
        both = None
        for sub in range(2):
            qm = _pick_head(masks[sub], qna[:, p * LANES:(p + 1) * LANES])
            e, r = softmax_parts(_dot_nt(qm, kp))
            o = _dot(e.astype(BF16), vp) * r
            both = o if sub == 0 else jnp.where(masks[1], o, both)
        o_ref[:, DA_WIDTH + p * LANES:DA_WIDTH + (p + 1) * LANES] = both.astype(BF16)


def _attention_ctx(l, qkv, kvf, lam_tab, g_subln):
    return pl.pallas_call(
        _attn_ctx_kernel,
        out_shape=jax.ShapeDtypeStruct((SEG_ROWS, D_MODEL), BF16),
        grid=(BATCH,),
        in_specs=[
            pl.BlockSpec((1, SEQ, CHUNK), lambda b: (0, b, 0)),
            pl.BlockSpec((1, SEQ, CHUNK), lambda b: (0, b, 3)),
            pl.BlockSpec((1, SEQ, 4 * CHUNK), lambda b: (0, b, 0)),
            pl.BlockSpec((1, 8, LANES), lambda b: (l, 0, 0)),
            pl.BlockSpec((1, 1, LANES), lambda b: (l, 0, 0)),
        ],
        out_specs=pl.BlockSpec((SEQ, D_MODEL), lambda b: (b, 0)),
        compiler_params=_params("arbitrary"),
        name="attention_ctx",
    )(qkv, qkv, kvf, lam_tab, g_subln)


DA_KEYS = DEC_SEQ + PAST_LEN
DA_KCHUNK = 512
DA_TQ = 256


def _attn_da_kernel(q_ref, k_ref, v_ref, kc_ref, vc_ref, lam_ref, gsub_ref, o_ref, vt_ref, e1_ref, e2_ref):
    n_lat = DEC_SEQ // DA_KCHUNK
    n_chunks = DA_KEYS // DA_KCHUNK

    @pl.when(pl.program_id(1) == 0)
    def _():
        for c in range(n_lat):
            rows = slice(c * DA_KCHUNK, (c + 1) * DA_KCHUNK)
            vt_ref[:, rows] = v_ref[0, rows, :].astype(F32).T.astype(BF16)
        vt_ref[:, DEC_SEQ:DA_KEYS] = vc_ref[0, 0].astype(F32).T.astype(BF16)

    lam, one_minus = _lam_value(lam_ref)
    masks = _half_masks()
    q = q_ref[0]
    gsub = gsub_ref[0]

    def key_chunk(c, cols):
        if c < n_lat:
            return k_ref[0, c * DA_KCHUNK:(c + 1) * DA_KCHUNK, cols]
        return kc_ref[0, 0, :, cols]

    def score_map(qm, cols, e_ref, weight):
        maxes, sums = [], []
        for c in range(n_chunks):
            s = _dot_nt(key_chunk(c, cols), qm)
            mc = jnp.max(s, axis=0, keepdims=True)
            e = jnp.exp2(s - mc)
            sums.append(jnp.sum(e, axis=0, keepdims=True))
            maxes.append(mc)
            e_ref[c * DA_KCHUNK:(c + 1) * DA_KCHUNK, :] = e.astype(BF16)
        m = functools.reduce(jnp.maximum, maxes)
        facs = [jnp.exp2(mc - m) for mc in maxes]
        tot = functools.reduce(lambda a, b: a + b, [lc * fc for lc, fc in zip(sums, facs)])
        w = weight / tot
        return [(fc * w).astype(BF16) for fc in facs]

    for h in range(DA_HEADS):
        p, sub = divmod(h, 2)
        c1 = slice(p * LANES, (p + 1) * LANES)
        c2 = slice(2 * LANES + p * LANES, 2 * LANES + (p + 1) * LANES)
        f1 = score_map(_pick_head(masks[sub], q[:, c1]), c1, e1_ref, 1.0)
        f2 = score_map(_pick_head(masks[sub], q[:, c2]), c2, e2_ref, lam)
        o_t = jnp.zeros((LANES, DA_TQ), F32)
        for c in range(n_chunks):
            rows = slice(c * DA_KCHUNK, (c + 1) * DA_KCHUNK)
            a = e1_ref[rows, :] * f1[c] - e2_ref[rows, :] * f2[c]
            o_t = o_t + _dot(vt_ref[h * LANES:(h + 1) * LANES, rows], a)
        o_ref[0, :, h * LANES:(h + 1) * LANES] = _sub_layer_norm(o_t.T, gsub, one_minus).astype(BF16)


def _attention_da(l, qkv, ctx_k, ctx_v, lam_tab, g_subln):
    tq = DA_TQ
    return pl.pallas_call(
        _attn_da_kernel,
        out_shape=jax.ShapeDtypeStruct((DEC_BATCH, DEC_SEQ, DA_WIDTH), BF16),
        grid=(DEC_BATCH, DEC_SEQ // tq),
        in_specs=[
            pl.BlockSpec((1, tq, CHUNK), lambda b, i: (b + 1, i, 0)),
            pl.BlockSpec((1, DEC_SEQ, CHUNK), lambda b, i: (b + 1, 0, 1), pipeline_mode=pl.Buffered(1)),
            pl.BlockSpec((1, DEC_SEQ, CHUNK), lambda b, i: (b + 1, 0, 2), pipeline_mode=pl.Buffered(1)),
            pl.BlockSpec((1, 1, PAST_LEN, CHUNK), lambda b, i: (l, b, 0, 0)),
            pl.BlockSpec((1, 1, PAST_LEN, CHUNK), lambda b, i: (l, b, 0, 0)),
            pl.BlockSpec((1, 8, LANES), lambda b, i: (l, 0, 0)),
            pl.BlockSpec((1, 1, LANES), lambda b, i: (l, 0, 0)),
        ],
        out_specs=pl.BlockSpec((1, tq, DA_WIDTH), lambda b, i: (b, i, 0)),
        scratch_shapes=[
            pltpu.VMEM((DA_WIDTH, DA_KEYS), BF16),
            pltpu.VMEM((DA_KEYS, tq), BF16),
            pltpu.VMEM((DA_KEYS, tq), BF16),
        ],
        compiler_params=_params("arbitrary", "arbitrary"),
        name="attention_da",
    )(qkv, qkv, qkv, ctx_k, ctx_v, lam_tab, g_subln)


NA_QROWS = 8
NA_KROWS = 16


def _attn_na_kernel(q_ref, k_ref, v_ref, kc_ref, vc_ref, bias_ref, o_ref):
    i = pl.program_id(1)
    masks = _half_masks()
    q = q_ref[0]
    row0 = jnp.clip(i * NA_QROWS - NA_ROWS // 2, 0, GRID_H - NA_KROWS)
    start = pl.multiple_of(row0 * GRID_W, 256)
    nk = NA_KROWS * GRID_W

    for p in range(NA_HEADS // 2):
        c0 = p * LANES
        kw = k_ref[0, pl.ds(start, nk), c0:c0 + LANES]
        vw = v_ref[0, pl.ds(start, nk), c0:c0 + LANES]
        kc = kc_ref[0, 0, :, c0:c0 + LANES]
        vc = vc_ref[0, 0, :, c0:c0 + LANES]
        both = None
        for sub in range(2):
            qm = _pick_head(masks[sub], q[:, c0:c0 + LANES])
            s_loc = _dot_nt(qm, kw) + bias_ref[0, 0, 2 * p + sub].astype(F32)
            s_ctx = _dot_nt(qm, kc)
            m = jnp.maximum(jnp.max(s_loc, axis=-1, keepdims=True), jnp.max(s_ctx, axis=-1, keepdims=True))
            e_loc = jnp.exp2(s_loc - m)
            e_ctx = jnp.exp2(s_ctx - m)
            tot = jnp.sum(e_loc, axis=-1, keepdims=True) + jnp.sum(e_ctx, axis=-1, keepdims=True)
            o = (_dot(e_loc.astype(BF16), vw) + _dot(e_ctx.astype(BF16), vc)) * (1.0 / tot)
            both = o if sub == 0 else jnp.where(masks[1], o, both)
        o_ref[0, :, c0:c0 + LANES] = both.astype(BF16)


def _attention_na(l, qkv, ctx_k, ctx_v, bias):
    tq = NA_QROWS * GRID_W
    n_i = DEC_SEQ // tq

    def bias_idx(b, i):
        return (l, jnp.where(i == 0, 0, jnp.where(i == n_i - 1, 2, 1)), 0, 0, 0)

    return pl.pallas_call(
        _attn_na_kernel,
        out_shape=jax.ShapeDtypeStruct((DEC_BATCH, DEC_SEQ, NA_WIDTH), BF16),
        grid=(DEC_BATCH, n_i),
        in_specs=[
            pl.BlockSpec((1, tq, CHUNK), lambda b, i: (b + 1, i, 3)),
            pl.BlockSpec((1, DEC_SEQ, CHUNK), lambda b, i: (b + 1, 0, 4), pipeline_mode=pl.Buffered(1)),
            pl.BlockSpec((1, DEC_SEQ, CHUNK), lambda b, i: (b + 1, 0, 5), pipeline_mode=pl.Buffered(1)),
            pl.BlockSpec((1, 1, PAST_LEN, CHUNK), lambda b, i: (l, b, 0, 0)),
            pl.BlockSpec((1, 1, PAST_LEN, CHUNK), lambda b, i: (l, b, 0, 0)),
            pl.BlockSpec((1, 1, NA_HEADS, tq, NA_KROWS * GRID_W), bias_idx, pipeline_mode=pl.Buffered(1)),
        ],
        out_specs=pl.BlockSpec((1, tq, NA_WIDTH), lambda b, i: (b, i, 0)),
        compiler_params=_params("arbitrary", "arbitrary"),
        name="attention_na",
    )(qkv, qkv, qkv, ctx_k, ctx_v, bias)


def _na_bias_tables(rpb):
    qr = np.arange(NA_QROWS)
    qc = np.arange(GRID_W)
    kr = np.arange(NA_KROWS)
    kc = np.arange(GRID_W)
    c0 = np.clip(qc - NA_COLS // 2, 0, GRID_W - NA_COLS)
    col_ok = (kc[None, :] >= c0[:, None]) & (kc[None, :] < c0[:, None] + NA_COLS)
    cpad = GRID_W - NA_COLS
    rpb_c = jnp.pad(rpb, ((0, 0), (0, 0), (0, 0), (cpad, cpad)))
    by_col = jnp.stack([rpb_c[..., GRID_W - 1 - c:2 * GRID_W - 1 - c] for c in range(GRID_W)], axis=-2)
    by_col = jnp.where(col_ok, by_col * LOG2E, -jnp.inf)
    rpad = NA_ROWS
    by_col = jnp.pad(by_col, ((0, 0), (0, 0), (rpad, rpad), (0, 0), (0, 0)))
    tiles = []
    for i in (0, 1, GRID_H // NA_QROWS - 1):
        rows_q = NA_QROWS * i + qr
        row0_k = min(max(NA_QROWS * i - NA_ROWS // 2, 0), GRID_H - NA_KROWS)
        rows_k = row0_k + kr
        r0 = np.clip(rows_q - NA_ROWS // 2, 0, GRID_H - NA_ROWS)
        row_ok = (rows_k[None, :] >= r0[:, None]) & (rows_k[None, :] < r0[:, None] + NA_ROWS)
        strips = []
        for r in range(NA_QROWS):
            lo = row0_k - int(rows_q[r]) + NA_ROWS - 1 + rpad
            strips.append(by_col[:, :, lo:lo + NA_KROWS])
        tile = jnp.stack(strips, axis=2).transpose(0, 1, 2, 4, 3, 5)
        tile = jnp.where(row_ok[:, None, :, None], tile, -jnp.inf)
        tiles.append(tile.reshape(DEPTH, NA_HEADS, NA_QROWS * GRID_W, NA_KROWS * GRID_W).astype(BF16))
    return jnp.stack(tiles, axis=1)


def _out_kernel(x_ref, mod_ref, oc_ref, oda_ref, ona_ref, gate_ref, wa_ref, wb_ref, wo_ref, g2_ref,
                wr_ref, br_ref, x1_ref, h2_ref, comb_ref):
    is_ctx = pl.program_id(0) == 0
    oc = oc_ref[...]
    oa = jnp.where(is_ctx, oc[:, :DA_WIDTH], oda_ref[0])
    ob = jnp.where(is_ctx, oc[:, DA_WIDTH:], ona_ref[0])
    ya = _dot(oa, wa_ref[0])
    yb = _dot(ob, wb_ref[0])
    g = gate_ref[0]
    mixed = (g[:, :D_MODEL].astype(F32) * ya + g[:, D_MODEL:].astype(F32) * yb).astype(BF16)
    mod = mod_ref[0]
    x1 = x_ref[0] + mod[2:3] * _dot(mixed, wo_ref[0])
    x1_ref[0] = x1

    ms = jnp.mean(x1 * x1, axis=-1, keepdims=True)
    h2 = x1 * lax.rsqrt(ms + EPS) * g2_ref[0] * (1.0 + mod[4:5]) + mod[3:4]
    h2_hi = h2.astype(BF16)
    h2_ref[0] = h2_hi

    h2_lo = (h2 - h2_hi.astype(F32)).astype(BF16)
    wr = wr_ref[...]
    wr_hi = wr.astype(BF16)
    wr_lo = (wr - wr_hi.astype(F32)).astype(BF16)
    logits = _dot(h2_hi, wr_hi) + (_dot(h2_hi, wr_lo) + _dot(h2_lo, wr_hi))
    scores = jax.nn.sigmoid(logits)
    sel = scores + br_ref[...]

    lane = lax.broadcasted_iota(jnp.int32, sel.shape, 1).astype(F32)
    neg = -jnp.inf
    far = float(LANES)
    best = None
    for grp in range(N_GROUPS):
        lo = float(grp * EXPERTS_PER_GROUP)
        in_grp = (lane >= lo) & (lane < lo + EXPERTS_PER_GROUP)
        v = jnp.where(in_grp, sel, neg)
        m1 = jnp.max(v, axis=-1, keepdims=True)
        i1 = jnp.min(jnp.where(v == m1, lane, far), axis=-1, keepdims=True)
        v2 = jnp.where(lane == i1, neg, v)
        m2 = jnp.max(v2, axis=-1, keepdims=True)
        i2 = jnp.min(jnp.where(v2 == m2, lane, far), axis=-1, keepdims=True)
        gs = m1 + m2
        if best is None:
            best = (gs, i1, i2)
        else:
            upd = gs > best[0]
            best = (jnp.where(upd, gs, best[0]), jnp.where(upd, i1, best[1]), jnp.where(upd, i2, best[2]))
    _, i1, i2 = best
    hit1 = lane == i1
    hit2 = lane == i2
    s1 = jnp.sum(jnp.where(hit1, scores, 0.0), axis=-1, keepdims=True)
    s2 = jnp.sum(jnp.where(hit2, scores, 0.0), axis=-1, keepdims=True)
    den = s1 + s2
    comb_ref[0] = jnp.where(hit1, s1 / den, 0.0) + jnp.where(hit2, s2 / den, 0.0)


def _output_stage(l, x_all, mods, o_ctx, o_da, o_na, gates, wa_b, wb_b, wo_b, g_norm2, w_router_p, b_router_p):
    tm = 512
    nt = SEG_ROWS // tm

    def lat_idx(s, i):
        return (jnp.maximum(s - 1, 0), jnp.where(s == 0, 0, i), 0)

    return pl.pallas_call(
        _out_kernel,
        out_shape=(
            jax.ShapeDtypeStruct((N_SEG, SEG_ROWS, D_MODEL), F32),
            jax.ShapeDtypeStruct((N_SEG, SEG_ROWS, D_MODEL), BF16),
            jax.ShapeDtypeStruct((N_SEG, SEG_ROWS, LANES), F32),
        ),
        grid=(N_SEG, nt),
        in_specs=[
            pl.BlockSpec((1, tm, D_MODEL), lambda s, i: (s, i, 0)),
            pl.BlockSpec((1, N_MOD, D_MODEL), lambda s, i: (s, 0, 0)),
            pl.BlockSpec((tm, D_MODEL), lambda s, i: (jnp.where(s == 0, i, nt - 1), 0)),
            pl.BlockSpec((1, tm, DA_WIDTH), lat_idx),
            pl.BlockSpec((1, tm, NA_WIDTH), lat_idx),
            pl.BlockSpec((1, tm, 2 * D_MODEL), lambda s, i: (s, i, 0)),
            pl.BlockSpec((1, DA_WIDTH, D_MODEL), lambda s, i: (l, 0, 0)),
            pl.BlockSpec((1, NA_WIDTH, D_MODEL), lambda s, i: (l, 0, 0)),
            pl.BlockSpec((1, D_MODEL, D_MODEL), lambda s, i: (l, 0, 0)),
            pl.BlockSpec((1, 1, D_MODEL), lambda s, i: (l, 0, 0)),
            pl.BlockSpec((D_MODEL, LANES), lambda s, i: (0, 0)),
            pl.BlockSpec((1, LANES), lambda s, i: (0, 0)),
        ],
        out_specs=(
            pl.BlockSpec((1, tm, D_MODEL), lambda s, i: (s, i, 0)),
            pl.BlockSpec((1, tm, D_MODEL), lambda s, i: (s, i, 0)),
            pl.BlockSpec((1, tm, LANES), lambda s, i: (s, i, 0)),
        ),
        compiler_params=_params("arbitrary", "arbitrary"),
        name="output_router",
    )(x_all, mods, o_ctx, o_da, o_na, gates, wa_b, wb_b, wo_b, g_norm2, w_router_p, b_router_p)


MOE_CHUNK = EXPERTS_PER_GROUP * D_EXPERT


def _moe_kernel(h_ref, comb_ref, x1_ref, mod_ref, wg_ref, wu_ref, wd_ref, o_ref):
    h = h_ref[0]
    comb = comb_ref[0]
    acc = jnp.zeros(x1_ref.shape[1:], F32)
    for c in range(N_GROUPS):
        parts = []
        for j in range(EXPERTS_PER_GROUP):
            e = c * EXPERTS_PER_GROUP + j
            g = _dot(h, wg_ref[0, e])
            u = _dot(h, wu_ref[0, e])
            parts.append(g * jax.nn.sigmoid(g) * u * comb[:, e:e + 1])
        hs = jnp.concatenate(parts, axis=1).astype(BF16)
        acc = acc + _dot(hs, wd_ref[0, c * MOE_CHUNK:(c + 1) * MOE_CHUNK, :])
    o_ref[0] = x1_ref[0] + mod_ref[0][5:6] * acc


def _experts(l, h2, comb, x1, mods, wg_all, wu_all, wd_all):
    tm = 256
    hidden = N_EXPERTS * D_EXPERT
    once = pl.Buffered(1)
    return pl.pallas_call(
        _moe_kernel,
        out_shape=jax.ShapeDtypeStruct((N_SEG, SEG_ROWS, D_MODEL), F32),
        grid=(N_SEG, SEG_ROWS // tm),
        in_specs=[
            pl.BlockSpec((1, tm, D_MODEL), lambda s, i: (s, i, 0)),
            pl.BlockSpec((1, tm, LANES), lambda s, i: (s, i, 0)),
            pl.BlockSpec((1, tm, D_MODEL), lambda s, i: (s, i, 0)),
            pl.BlockSpec((1, N_MOD, D_MODEL), lambda s, i: (s, 0, 0)),
            pl.BlockSpec((1, N_EXPERTS, D_MODEL, D_EXPERT), lambda s, i: (l, 0, 0, 0), pipeline_mode=once),
            pl.BlockSpec((1, N_EXPERTS, D_MODEL, D_EXPERT), lambda s, i: (l, 0, 0, 0), pipeline_mode=once),
            pl.BlockSpec((1, hidden, D_MODEL), lambda s, i: (l, 0, 0), pipeline_mode=once),
        ],
        out_specs=pl.BlockSpec((1, tm, D_MODEL), lambda s, i: (s, i, 0)),
        compiler_params=_params("arbitrary", "arbitrary"),
        name="experts",
    )(h2, comb, x1, mods, wg_all, wu_all, wd_all)


def _rope_tables():
    t = np.arange(DEC_SEQ)
    pos = np.stack([t // GRID_W, t % GRID_W], axis=-1).astype(np.float32)
    inv_freq = (ROPE_BASE ** (-np.arange(ROPE_PAIRS, dtype=np.float32) / ROPE_PAIRS)).astype(np.float32)
    ang = pos[:, :, None] * inv_freq
    lane = np.arange(LANES)
    axis = (lane % HEAD_DIM) // (2 * ROPE_PAIRS)
    pair = lane % ROPE_PAIRS
    second = ((lane // ROPE_PAIRS) % 2).astype(bool)
    a = ang[:, axis, pair]
    cos = np.cos(a)
    sin = np.sin(a)
    lat = np.concatenate([cos, np.where(second, 0.0, -sin), np.where(second, sin, 0.0)], axis=1)
    ident = np.concatenate([np.ones_like(cos), np.zeros_like(cos), np.zeros_like(cos)], axis=1)
    return jnp.asarray(np.stack([ident, lat]).astype(np.float32))


def _head_block_diag():
    r = np.arange(CHUNK // 2) // HEAD_DIM
    return jnp.asarray((r[:, None] == r[None, :]).astype(np.float32), dtype=BF16)


def kernel(x_prompt, x_sample, cache_da_k, cache_da_v, cache_na_k, cache_na_v, c, c_ctx, w_mod, b_mod, g_norm1,
           g_norm2, w_in, g_q_da, g_k_da, g_q_na, g_k_na, lam_q1, lam_k1, lam_q2, lam_k2, g_subln, rpb, w_br_a,
           w_br_b, w_out, w_router, b_router, w_gate, w_up, w_down):
    L = DEPTH
    x_all = jnp.concatenate([x_prompt.reshape(1, SEG_ROWS, D_MODEL), x_sample], axis=0)
    cond8 = jnp.concatenate([c_ctx[None], c, jnp.zeros((8 - N_SEG, D_MODEL), F32)], axis=0)
    w_in_b = w_in.astype(BF16)
    wa_b = w_br_a.astype(BF16)
    wb_b = w_br_b.astype(BF16)
    wo_b = w_out.astype(BF16)
    hidden = N_EXPERTS * D_EXPERT
    wg_all = w_gate.astype(BF16)
    wu_all = w_up.astype(BF16)
    wd_all = w_down.astype(BF16).reshape(L, hidden, D_MODEL)
    w_router_p = jnp.pad(w_router, ((0, 0), (0, LANES - N_EXPERTS)))
    b_router_p = jnp.pad(b_router, (0, LANES - N_EXPERTS)).reshape(1, LANES)
    head_gains = jnp.stack([jnp.tile(g, (1, CHUNK // HEAD_DIM)) for g in (g_q_da, g_k_da, g_q_na, g_k_na)],
                           axis=1).reshape(L, 4, CHUNK)
    g_norm1 = g_norm1.reshape(L, 1, D_MODEL)
    g_norm2 = g_norm2.reshape(L, 1, D_MODEL)
    g_subln = g_subln.reshape(L, 1, LANES)
    pad64 = lambda t: jnp.pad(t, ((0, 0), (0, LANES - HEAD_DIM)))
    lam_inits = jnp.asarray([0.8 - 0.6 * math.exp(-0.3 * l) for l in range(L)], F32)
    lam_tab = jnp.stack([pad64(lam_q1), pad64(lam_k1), pad64(lam_q2), pad64(lam_k2),
                         jnp.broadcast_to(lam_inits[:, None], (L, LANES)),
                         jnp.zeros((L, LANES), F32), jnp.zeros((L, LANES), F32), jnp.zeros((L, LANES), F32)],
                        axis=1)
    ctx_da_k = cache_da_k.transpose(1, 0, 4, 2, 3, 5).reshape(L, DEC_BATCH, PAST_LEN, DA_WIDTH).astype(BF16)
    ctx_da_v = cache_da_v.transpose(1, 0, 3, 2, 4).reshape(L, DEC_BATCH, PAST_LEN, DA_WIDTH).astype(BF16)
    ctx_na_k = cache_na_k.transpose(1, 0, 3, 2, 4).reshape(L, DEC_BATCH, PAST_LEN, NA_WIDTH).astype(BF16)
    ctx_na_v = cache_na_v.transpose(1, 0, 3, 2, 4).reshape(L, DEC_BATCH, PAST_LEN, NA_WIDTH).astype(BF16)
    na_bias = _na_bias_tables(rpb)
    rope_tab = _rope_tables()
    bd = _head_block_diag()

    mods_all = _modulation(cond8, w_mod, b_mod)[:, :N_SEG].reshape(L, N_SEG, N_MOD, D_MODEL)

    kv_layers = []
    for l in range(L):
        mods = mods_all[l]
        qkv, gates, kvf = _in_projection(l, x_all, mods, g_norm1, w_in_b, head_gains, bd, rope_tab)
        o_ctx = _attention_ctx(l, qkv, kvf, lam_tab, g_subln)
        o_da = _attention_da(l, qkv, ctx_da_k, ctx_da_v, lam_tab, g_subln)
        o_na = _attention_na(l, qkv, ctx_na_k, ctx_na_v, na_bias)
        x1, h2, comb = _output_stage(l, x_all, mods, o_ctx, o_da, o_na, gates, wa_b, wb_b, wo_b, g_norm2,
                                     w_router_p, b_router_p)
        x_all = _experts(l, h2, comb, x1, mods, wg_all, wu_all, wd_all)
        kv_layers.append(kvf[0])

    y_prompt = x_all[0].reshape(BATCH, SEQ, D_MODEL)
    y_sample = x_all[1:]
    kv = jnp.stack(kv_layers, axis=1).reshape(BATCH, SEQ, L, 4 * CHUNK).transpose(0, 2, 1, 3)
    new_da_k = kv[..., 0:CHUNK].reshape(BATCH, L, SEQ, 2, DA_HEADS, HEAD_DIM).transpose(0, 1, 3, 4, 2, 5)
    new_da_v = kv[..., CHUNK:2 * CHUNK].reshape(BATCH, L, SEQ, DA_HEADS, 2 * HEAD_DIM).transpose(0, 1, 3, 2, 4)
    new_na_k = kv[..., 2 * CHUNK:3 * CHUNK].reshape(BATCH, L, SEQ, NA_HEADS, HEAD_DIM).transpose(0, 1, 3, 2, 4)
    new_na_v = kv[..., 3 * CHUNK:4 * CHUNK].reshape(BATCH, L, SEQ, NA_HEADS, HEAD_DIM).transpose(0, 1, 3, 2, 4)
    return (y_prompt, y_sample, new_da_k, new_da_v, new_na_k, new_na_v)
```

```python
import functools
import math

import numpy as np
import jax
import jax.numpy as jnp
from jax import lax
from jax.experimental import pallas as pl
from jax.experimental.pallas import tpu as pltpu

D_MODEL = 1024
BATCH = 16
SEQ = 256
DEPTH = 4
DEC_BATCH = 4
DEC_SEQ = 4096
PAST_LEN = 512
GRID_W = 64
GRID_H = DEC_SEQ // GRID_W
HEAD_DIM = 64
DA_HEADS = 4
DA_WIDTH = 512
NA_HEADS = 8
NA_WIDTH = 512
NA_ROWS = 8
NA_COLS = 16
IN_COLS = 3 * DA_WIDTH + 3 * NA_WIDTH + 2 * D_MODEL
ROPE_BASE = 10000.0
ROPE_PAIRS = HEAD_DIM // 4
N_EXPERTS = 16
N_GROUPS = 4
EXPERTS_PER_GROUP = 4
D_EXPERT = 256
N_MOD = 6
EPS = 1e-6
ATTN_SCALE = HEAD_DIM ** -0.5
Q_PRESCALE = ATTN_SCALE * math.log2(math.e)
LOG2E = math.log2(math.e)

N_SEG = 1 + DEC_BATCH
SEG_ROWS = DEC_SEQ
LANES = 128
CHUNK = 512

VMEM_LIMIT = 56 * 1024 * 1024

F32 = jnp.float32
BF16 = jnp.bfloat16


def _dot(a, b):
    return jnp.dot(a, b, preferred_element_type=F32)


def _dot_nt(a, b):
    return lax.dot_general(a, b, (((1,), (1,)), ((), ())), preferred_element_type=F32)


def _params(*sem):
    return pltpu.CompilerParams(dimension_semantics=sem, vmem_limit_bytes=VMEM_LIMIT)


def _ctx_rows_spec(tm, width):
    nt = SEG_ROWS // tm
    return pl.BlockSpec((1, tm, width), lambda s, i: (0, jnp.where(s == 0, i, nt - 1), 0))


def _lat_rows_spec(tm, width):
    return pl.BlockSpec((1, tm, width), lambda s, i: (jnp.maximum(s - 1, 0), jnp.where(s == 0, 0, i), 0))


def _mod_kernel(cond_ref, w_ref, b_ref, o_ref):
    c = cond_ref[...]
    sc = (c * jax.nn.sigmoid(c)).astype(BF16)
    o_ref[0] = _dot(sc, w_ref[0].astype(BF16)) + b_ref[0]


def _modulation(cond8, w_mod, b_mod):
    tn = 1536
    return pl.pallas_call(
        _mod_kernel,
        out_shape=jax.ShapeDtypeStruct((DEPTH, 8, N_MOD * D_MODEL), F32),
        grid=(DEPTH, N_MOD * D_MODEL // tn),
        in_specs=[
            pl.BlockSpec((8, D_MODEL), lambda l, j: (0, 0)),
            pl.BlockSpec((1, D_MODEL, tn), lambda l, j: (l, 0, j)),
            pl.BlockSpec((1, 1, tn), lambda l, j: (l, 0, j)),
        ],
        out_specs=pl.BlockSpec((1, 8, tn), lambda l, j: (l, 0, j)),
        compiler_params=_params("arbitrary", "arbitrary"),
        name="modulation",
    )(cond8, w_mod, b_mod.reshape(DEPTH, 1, N_MOD * D_MODEL))


def _inproj_kernel(xc_ref, xl_ref, mod_ref, g1_ref, w_ref, hg_ref, bd_ref, rope_ref, qkv_ref, gate_ref, kvf_ref):
    s = pl.program_id(0)
    x = jnp.where(s == 0, xc_ref[0], xl_ref[0])
    ms = jnp.mean(x * x, axis=-1, keepdims=True)
    y = x * lax.rsqrt(ms + EPS) * g1_ref[0]
    mod = mod_ref[0]
    h = (y * (1.0 + mod[1:2]) + mod[0:1]).astype(BF16)

    rope = rope_ref[0]
    cos4 = jnp.concatenate([rope[:, 0:LANES]] * 4, axis=1)
    sin_up4 = jnp.concatenate([rope[:, LANES:2 * LANES]] * 4, axis=1)
    sin_dn4 = jnp.concatenate([rope[:, 2 * LANES:3 * LANES]] * 4, axis=1)

    def head_norm(acc, row):
        sq = (acc * acc).astype(BF16)
        half = CHUNK // 2
        ssum = jnp.concatenate([_dot(sq[:, :half], bd_ref[...]), _dot(sq[:, half:], bd_ref[...])], axis=1)
        return acc * lax.rsqrt(ssum * (1.0 / HEAD_DIM) + EPS) * hg_ref[0, row:row + 1, :]

    def rope_rot(t):
        return (t * cos4 + pltpu.roll(t, CHUNK - ROPE_PAIRS, 1) * sin_up4
                + pltpu.roll(t, ROPE_PAIRS, 1) * sin_dn4)

    def proj(c):
        return _dot(h, w_ref[0, :, c * CHUNK:(c + 1) * CHUNK])

    is_ctx = s == 0

    q = rope_rot(head_norm(proj(0), 0)) * Q_PRESCALE
    qkv_ref[0, :, 0:CHUNK] = q.astype(BF16)

    k = rope_rot(head_norm(proj(1), 1))
    qkv_ref[0, :, CHUNK:2 * CHUNK] = k.astype(BF16)

    @pl.when(is_ctx)
    def _():
        kvf_ref[0, :, 0:CHUNK] = k

    v = proj(2)
    qkv_ref[0, :, 2 * CHUNK:3 * CHUNK] = v.astype(BF16)

    @pl.when(is_ctx)
    def _():
        kvf_ref[0, :, CHUNK:2 * CHUNK] = v

    q = head_norm(proj(3), 2) * Q_PRESCALE
    qkv_ref[0, :, 3 * CHUNK:4 * CHUNK] = q.astype(BF16)

    k = head_norm(proj(4), 3)
    qkv_ref[0, :, 4 * CHUNK:5 * CHUNK] = k.astype(BF16)

    @pl.when(is_ctx)
    def _():
        kvf_ref[0, :, 2 * CHUNK:3 * CHUNK] = k

    v = proj(5)
    qkv_ref[0, :, 5 * CHUNK:6 * CHUNK] = v.astype(BF16)

    @pl.when(is_ctx)
    def _():
        kvf_ref[0, :, 3 * CHUNK:4 * CHUNK] = v

    for c in range(4):
        g = proj(6 + c)
        gate_ref[0, :, c * CHUNK:(c + 1) * CHUNK] = jax.nn.sigmoid(g).astype(BF16)


def _in_projection(l, x_ctx, x_lat, mods, g_norm1, w_in_b, head_gains, bd, rope_tab):
    tm = 512
    nt = SEG_ROWS // tm
    return pl.pallas_call(
        _inproj_kernel,
        out_shape=(
            jax.ShapeDtypeStruct((N_SEG, SEG_ROWS, 6 * CHUNK), BF16),
            jax.ShapeDtypeStruct((N_SEG, SEG_ROWS, 2 * D_MODEL), BF16),
            jax.ShapeDtypeStruct((1, SEG_ROWS, 4 * CHUNK), F32),
        ),
        grid=(N_SEG, nt),
        in_specs=[
            _ctx_rows_spec(tm, D_MODEL),
            _lat_rows_spec(tm, D_MODEL),
            pl.BlockSpec((1, N_MOD, D_MODEL), lambda s, i: (s, 0, 0)),
            pl.BlockSpec((1, 1, D_MODEL), lambda s, i: (l, 0, 0)),
            pl.BlockSpec((1, D_MODEL, IN_COLS), lambda s, i: (l, 0, 0), pipeline_mode=pl.Buffered(1)),
            pl.BlockSpec((1, 4, CHUNK), lambda s, i: (l, 0, 0)),
            pl.BlockSpec((CHUNK // 2, CHUNK // 2), lambda s, i: (0, 0)),
            pl.BlockSpec((1, tm, 3 * LANES), lambda s, i: (jnp.minimum(s, 1), i, 0)),
        ],
        out_specs=(
            pl.BlockSpec((1, tm, 6 * CHUNK), lambda s, i: (s, i, 0)),
            pl.BlockSpec((1, tm, 2 * D_MODEL), lambda s, i: (s, i, 0)),
            _ctx_rows_spec(tm, 4 * CHUNK),
        ),
        compiler_params=_params("arbitrary", "arbitrary"),
        name="in_projection",
    )(x_ctx, x_lat, mods, g_norm1, w_in_b, head_gains, bd, rope_tab)


def _lam_value(lam_ref):
    lp = lam_ref[0]
    s1 = jnp.sum(lp[0:1] * lp[1:2], axis=-1, keepdims=True)
    s2 = jnp.sum(lp[2:3] * lp[3:4], axis=-1, keepdims=True)
    lam_init = lp[4:5, 0:1]
    return jnp.exp(s1) - jnp.exp(s2) + lam_init, 1.0 - lam_init


def _half_masks():
    lane = lax.broadcasted_iota(jnp.int32, (1, LANES), 1)
    return (lane < HEAD_DIM, lane >= HEAD_DIM)


def _pick_head(mask, t):
    return jnp.where(mask, t, jnp.zeros_like(t))


def _sub_layer_norm(o, gsub, one_minus):
    ms = jnp.mean(o * o, axis=-1, keepdims=True)
    return o * lax.rsqrt(ms + EPS) * gsub * one_minus


def _attn_ctx_kernel(qda_ref, qna_ref, kv_ref, lam_ref, gsub_ref, o_ref):
    lam, one_minus = _lam_value(lam_ref)
    masks = _half_masks()
    qda = qda_ref[0]
    qna = qna_ref[0]
    gsub = gsub_ref[0]

    def kv(c0):
        return kv_ref[0, :, c0:c0 + LANES].astype(BF16)

    def softmax_parts(sc):
        m = jnp.max(sc, axis=-1, keepdims=True)
        e = jnp.exp2(sc - m)
        return e, 1.0 / jnp.sum(e, axis=-1, keepdims=True)

    for h in range(DA_HEADS):
        p, sub = divmod(h, 2)
        q1 = _pick_head(masks[sub], qda[:, p * LANES:(p + 1) * LANES])
        q2 = _pick_head(masks[sub], qda[:, 2 * LANES + p * LANES:2 * LANES + (p + 1) * LANES])
        e1, r1 = softmax_parts(_dot_nt(q1, kv(p * LANES)))
        e2, r2 = softmax_parts(_dot_nt(q2, kv(2 * LANES + p * LANES)))
        a = e1 * r1 - e2 * (lam * r2)
        o = _dot(a.astype(BF16), kv(CHUNK + h * LANES))
        o_ref[:, h * LANES:(h + 1) * LANES] = _sub_layer_norm(o, gsub, one_minus).astype(BF16)

    for p in range(NA_HEADS // 2):
        kp = kv(2 * CHUNK + p * LANES)
        vp = kv(3 * CHUNK + p * LANES)
        both = None
        for sub in range(2):
            qm = _pick_head(masks[sub], qna[:, p * LANES:(p + 1) * LANES])
            e, r = softmax_parts(_dot_nt(qm, kp))
            o = _dot(e.astype(BF16), vp) * r
            both = o if sub == 0 else jnp.where(masks[1], o, both)
        o_ref[:, DA_WIDTH + p * LANES:DA_WIDTH + (p + 1) * LANES] = both.astype(BF16)


def _attention_ctx(l, qkv, kvf, lam_tab, g_subln):
    return pl.pallas_call(
        _attn_ctx_kernel,
        out_shape=jax.ShapeDtypeStruct((SEG_ROWS, D_MODEL), BF16),
        grid=(BATCH,),
        in_specs=[
            pl.BlockSpec((1, SEQ, CHUNK), lambda b: (0, b, 0)),
            pl.BlockSpec((1, SEQ, CHUNK), lambda b: (0, b, 3)),
            pl.BlockSpec((1, SEQ, 4 * CHUNK), lambda b: (0, b, 0)),
            pl.BlockSpec((1, 8, LANES), lambda b: (l, 0, 0)),
            pl.BlockSpec((1, 1, LANES), lambda b: (l, 0, 0)),
        ],
        out_specs=pl.BlockSpec((SEQ, D_MODEL), lambda b: (b, 0)),
        compiler_params=_params("arbitrary"),
        name="attention_ctx",
    )(qkv, qkv, kvf, lam_tab, g_subln)


DA_KEYS = DEC_SEQ + PAST_LEN
DA_KCHUNK = 512
DA_TQ = 256


def _attn_da_kernel(q_ref, k_ref, v_ref, kc_ref, vc_ref, lam_ref, gsub_ref, o_ref, vt_ref, e1_ref, e2_ref):
    n_lat = DEC_SEQ // DA_KCHUNK
    n_chunks = DA_KEYS // DA_KCHUNK

    @pl.when(pl.program_id(1) == 0)
    def _():
        for c in range(n_lat):
            rows = slice(c * DA_KCHUNK, (c + 1) * DA_KCHUNK)
            vt_ref[:, rows] = v_ref[0, rows, :].astype(F32).T.astype(BF16)
        vt_ref[:, DEC_SEQ:DA_KEYS] = vc_ref[0, 0].astype(F32).T.astype(BF16)

    lam, one_minus = _lam_value(lam_ref)
    masks = _half_masks()
    q = q_ref[0]
    gsub = gsub_ref[0]

    def key_chunk(c, cols):
        if c < n_lat:
            return k_ref[0, c * DA_KCHUNK:(c + 1) * DA_KCHUNK, cols]
        return kc_ref[0, 0, :, cols]

    def score_map(qm, cols, e_ref, weight):
        maxes, sums = [], []
        for c in range(n_chunks):
            s = _dot_nt(key_chunk(c, cols), qm)
            mc = jnp.max(s, axis=0, keepdims=True)
            e = jnp.exp2(s - mc)
            sums.append(jnp.sum(e, axis=0, keepdims=True))
            maxes.append(mc)
            e_ref[c * DA_KCHUNK:(c + 1) * DA_KCHUNK, :] = e.astype(BF16)
        m = functools.reduce(jnp.maximum, maxes)
        facs = [jnp.exp2(mc - m) for mc in maxes]
        tot = functools.reduce(lambda a, b: a + b, [lc * fc for lc, fc in zip(sums, facs)])
        w = weight / tot
        return [(fc * w).astype(BF16) for fc in facs]

    for h in range(DA_HEADS):
        p, sub = divmod(h, 2)
        c1 = slice(p * LANES, (p + 1) * LANES)
        c2 = slice(2 * LANES + p * LANES, 2 * LANES + (p + 1) * LANES)
        f1 = score_map(_pick_head(masks[sub], q[:, c1]), c1, e1_ref, 1.0)
        f2 = score_map(_pick_head(masks[sub], q[:, c2]), c2, e2_ref, lam)
        o_t = jnp.zeros((LANES, DA_TQ), F32)
        for c in range(n_chunks):
            rows = slice(c * DA_KCHUNK, (c + 1) * DA_KCHUNK)
            a = e1_ref[rows, :] * f1[c] - e2_ref[rows, :] * f2[c]
            o_t = o_t + _dot(vt_ref[h * LANES:(h + 1) * LANES, rows], a)
        o_ref[0, :, h * LANES:(h + 1) * LANES] = _sub_layer_norm(o_t.T, gsub, one_minus).astype(BF16)


def _attention_da(l, qkv, ctx_k, ctx_v, lam_tab, g_subln):
    tq = DA_TQ
    return pl.pallas_call(
        _attn_da_kernel,
        out_shape=jax.ShapeDtypeStruct((DEC_BATCH, DEC_SEQ, DA_WIDTH), BF16),
        grid=(DEC_BATCH, DEC_SEQ // tq),
        in_specs=[
            pl.BlockSpec((1, tq, CHUNK), lambda b, i: (b + 1, i, 0)),
            pl.BlockSpec((1, DEC_SEQ, CHUNK), lambda b, i: (b + 1, 0, 1), pipeline_mode=pl.Buffered(1)),
            pl.BlockSpec((1, DEC_SEQ, CHUNK), lambda b, i: (b + 1, 0, 2), pipeline_mode=pl.Buffered(1)),
            pl.BlockSpec((1, 1, PAST_LEN, CHUNK), lambda b, i: (l, b, 0, 0)),
            pl.BlockSpec((1, 1, PAST_LEN, CHUNK), lambda b, i: (l, b, 0, 0)),
            pl.BlockSpec((1, 8, LANES), lambda b, i: (l, 0, 0)),
            pl.BlockSpec((1, 1, LANES), lambda b, i: (l, 0, 0)),
        ],
        out_specs=pl.BlockSpec((1, tq, DA_WIDTH), lambda b, i: (b, i, 0)),
        scratch_shapes=[
            pltpu.VMEM((DA_WIDTH, DA_KEYS), BF16),
            pltpu.VMEM((DA_KEYS, tq), BF16),
            pltpu.VMEM((DA_KEYS, tq), BF16),
        ],
        compiler_params=_params("arbitrary", "arbitrary"),
        name="attention_da",
    )(qkv, qkv, qkv, ctx_k, ctx_v, lam_tab, g_subln)


NA_QROWS = 8
NA_KROWS = 16


def _attn_na_kernel(q_ref, k_ref, v_ref, kc_ref, vc_ref, bias_ref, o_ref):
    i = pl.program_id(1)
    masks = _half_masks()
    q = q_ref[0]
    row0 = jnp.clip(i * NA_QROWS - NA_ROWS // 2, 0, GRID_H - NA_KROWS)
    start = pl.multiple_of(row0 * GRID_W, 256)
    nk = NA_KROWS * GRID_W

    for p in range(NA_HEADS // 2):
        c0 = p * LANES
        kw = k_ref[0, pl.ds(start, nk), c0:c0 + LANES]
        vw = v_ref[0, pl.ds(start, nk), c0:c0 + LANES]
        kc = kc_ref[0, 0, :, c0:c0 + LANES]
        vc = vc_ref[0, 0, :, c0:c0 + LANES]
        both = None
        for sub in range(2):
            qm = _pick_head(masks[sub], q[:, c0:c0 + LANES])
            s_loc = _dot_nt(qm, kw) + bias_ref[0, 0, 2 * p + sub].astype(F32)
            s_ctx = _dot_nt(qm, kc)
            m = jnp.maximum(jnp.max(s_loc, axis=-1, keepdims=True), jnp.max(s_ctx, axis=-1, keepdims=True))
            e_loc = jnp.exp2(s_loc - m)
            e_ctx = jnp.exp2(s_ctx - m)
            tot = jnp.sum(e_loc, axis=-1, keepdims=True) + jnp.sum(e_ctx, axis=-1, keepdims=True)
            o = (_dot(e_loc.astype(BF16), vw) + _dot(e_ctx.astype(BF16), vc)) * (1.0 / tot)
            both = o if sub == 0 else jnp.where(masks[1], o, both)
        o_ref[0, :, c0:c0 + LANES] = both.astype(BF16)


def _attention_na(l, qkv, ctx_k, ctx_v, bias):
    tq = NA_QROWS * GRID_W
    n_i = DEC_SEQ // tq

    def bias_idx(b, i):
        return (l, jnp.where(i == 0, 0, jnp.where(i == n_i - 1, 2, 1)), 0, 0, 0)

    return pl.pallas_call(
        _attn_na_kernel,
        out_shape=jax.ShapeDtypeStruct((DEC_BATCH, DEC_SEQ, NA_WIDTH), BF16),
        grid=(DEC_BATCH, n_i),
        in_specs=[
            pl.BlockSpec((1, tq, CHUNK), lambda b, i: (b + 1, i, 3)),
            pl.BlockSpec((1, DEC_SEQ, CHUNK), lambda b, i: (b + 1, 0, 4), pipeline_mode=pl.Buffered(1)),
            pl.BlockSpec((1, DEC_SEQ, CHUNK), lambda b, i: (b + 1, 0, 5), pipeline_mode=pl.Buffered(1)),
            pl.BlockSpec((1, 1, PAST_LEN, CHUNK), lambda b, i: (l, b, 0, 0)),
            pl.BlockSpec((1, 1, PAST_LEN, CHUNK), lambda b, i: (l, b, 0, 0)),
            pl.BlockSpec((1, 1, NA_HEADS, tq, NA_KROWS * GRID_W), bias_idx),
        ],
        out_specs=pl.BlockSpec((1, tq, NA_WIDTH), lambda b, i: (b, i, 0)),
        compiler_params=_params("arbitrary", "arbitrary"),
        name="attention_na",
    )(qkv, qkv, qkv, ctx_k, ctx_v, bias)


def _na_bias_tables(rpb):
    qr = np.arange(NA_QROWS)
    qc = np.arange(GRID_W)
    kr = np.arange(NA_KROWS)
    kc = np.arange(GRID_W)
    c0 = np.clip(qc - NA_COLS // 2, 0, GRID_W - NA_COLS)
    col_ok = (kc[None, :] >= c0[:, None]) & (kc[None, :] < c0[:, None] + NA_COLS)
    cpad = GRID_W - NA_COLS
    rpb_c = jnp.pad(rpb, ((0, 0), (0, 0), (0, 0), (cpad, cpad)))
    by_col = jnp.stack([rpb_c[..., GRID_W - 1 - c:2 * GRID_W - 1 - c] for c in range(GRID_W)], axis=-2)
    by_col = jnp.where(col_ok, by_col * LOG2E, -jnp.inf)
    rpad = NA_ROWS
    by_col = jnp.pad(by_col, ((0, 0), (0, 0), (rpad, rpad), (0, 0), (0, 0)))
    tiles = []
    for i in (0, 1, GRID_H // NA_QROWS - 1):
        rows_q = NA_QROWS * i + qr
        row0_k = min(max(NA_QROWS * i - NA_ROWS // 2, 0), GRID_H - NA_KROWS)
        rows_k = row0_k + kr
        r0 = np.clip(rows_q - NA_ROWS // 2, 0, GRID_H - NA_ROWS)
        row_ok = (rows_k[None, :] >= r0[:, None]) & (rows_k[None, :] < r0[:, None] + NA_ROWS)
        strips = []
        for r in range(NA_QROWS):
            lo = row0_k - int(rows_q[r]) + NA_ROWS - 1 + rpad
            strips.append(by_col[:, :, lo:lo + NA_KROWS])
        tile = jnp.stack(strips, axis=2)
        tile = jnp.where(row_ok[:, :, None, None], tile, -jnp.inf)
        tile = tile.transpose(0, 1, 2, 4, 3, 5)
        tiles.append(tile.reshape(DEPTH, NA_HEADS, NA_QROWS * GRID_W, NA_KROWS * GRID_W).astype(BF16))
    return jnp.stack(tiles, axis=1)


def _first_max_of_four(vals):
    a, b, c, d = vals
    m = jnp.maximum(jnp.maximum(a, b), jnp.maximum(c, d))
    idx = jnp.where(a == m, 0.0, jnp.where(b == m, 1.0, jnp.where(c == m, 2.0, 3.0)))
    return m, idx


def _out_kernel(xc_ref, xl_ref, mod_ref, oc_ref, oda_ref, ona_ref, gate_ref, wa_ref, wb_ref, wo_ref, g2_ref,
                wrt_ref, br_ref, x1_ref, h2_ref, comb_ref):
    is_ctx = pl.program_id(0) == 0
    oc = oc_ref[0]
    oa = jnp.where(is_ctx, oc[:, :DA_WIDTH], oda_ref[0])
    ob = jnp.where(is_ctx, oc[:, DA_WIDTH:], ona_ref[0])
    ya = _dot(oa, wa_ref[0])
    yb = _dot(ob, wb_ref[0])
    g = gate_ref[0]
    mixed = (g[:, :D_MODEL].astype(F32) * ya + g[:, D_MODEL:].astype(F32) * yb).astype(BF16)
    mod = mod_ref[0]
    x1 = jnp.where(is_ctx, xc_ref[0], xl_ref[0]) + mod[2:3] * _dot(mixed, wo_ref[0])
    x1_ref[0] = x1

    ms = jnp.mean(x1 * x1, axis=-1, keepdims=True)
    h2 = x1 * lax.rsqrt(ms + EPS) * g2_ref[0] * (1.0 + mod[4:5]) + mod[3:4]
    h2_hi = h2.astype(BF16)
    h2_ref[0] = h2_hi

    h2_lo = (h2 - h2_hi.astype(F32)).astype(BF16)
    wr = wrt_ref[...]
    wr_hi = wr.astype(BF16)
    wr_lo = (wr - wr_hi.astype(F32)).astype(BF16)
    by_hi = _dot_nt(jnp.concatenate([wr_hi, wr_lo], axis=0), h2_hi)
    logits = by_hi[:N_EXPERTS] + (by_hi[N_EXPERTS:] + _dot_nt(wr_hi, h2_lo))
    scores = jax.nn.sigmoid(logits)
    sel = scores + br_ref[...]
    score_rows = [scores[e:e + 1] for e in range(N_EXPERTS)]
    sel_rows = [sel[e:e + 1] for e in range(N_EXPERTS)]

    neg = -jnp.inf
    best = None
    for grp in range(N_GROUPS):
        vals = sel_rows[grp * EXPERTS_PER_GROUP:(grp + 1) * EXPERTS_PER_GROUP]
        m1, i1 = _first_max_of_four(vals)
        rest = [jnp.where(i1 == float(j), neg, v) for j, v in enumerate(vals)]
        m2, i2 = _first_max_of_four(rest)
        cand = (m1 + m2, i1 + float(grp * EXPERTS_PER_GROUP), i2 + float(grp * EXPERTS_PER_GROUP))
        if best is None:
            best = cand
        else:
            upd = cand[0] > best[0]
            best = tuple(jnp.where(upd, n, o) for n, o in zip(cand, best))
    _, e1, e2 = best
    hits1 = [e1 == float(e) for e in range(N_EXPERTS)]
    hits2 = [e2 == float(e) for e in range(N_EXPERTS)]
    s1 = functools.reduce(lambda a, b: a + b, [jnp.where(h, r, 0.0) for h, r in zip(hits1, score_rows)])
    s2 = functools.reduce(lambda a, b: a + b, [jnp.where(h, r, 0.0) for h, r in zip(hits2, score_rows)])
    den = s1 + s2
    w1 = s1 / den
    w2 = s2 / den
    rows = [jnp.where(h1, w1, 0.0) + jnp.where(h2_, w2, 0.0) for h1, h2_ in zip(hits1, hits2)]
    rows.append(jnp.zeros((LANES - N_EXPERTS, rows[0].shape[1]), F32))
    comb_ref[0] = jnp.concatenate(rows, axis=0).T


def _output_stage(l, x_ctx, x_lat, mods, o_ctx, o_da, o_na, gates, wa_b, wb_b, wo_b, g_norm2, w_router_t,
                  b_router_c):
    tm = 256
    nt = SEG_ROWS // tm
    return pl.pallas_call(
        _out_kernel,
        out_shape=(
            jax.ShapeDtypeStruct((N_SEG, SEG_ROWS, D_MODEL), F32),
            jax.ShapeDtypeStruct((N_SEG, SEG_ROWS, D_MODEL), BF16),
            jax.ShapeDtypeStruct((N_SEG, SEG_ROWS, LANES), F32),
        ),
        grid=(N_SEG, nt),
        in_specs=[
            _ctx_rows_spec(tm, D_MODEL),
            _lat_rows_spec(tm, D_MODEL),
            pl.BlockSpec((1, N_MOD, D_MODEL), lambda s, i: (s, 0, 0)),
            _ctx_rows_spec(tm, D_MODEL),
            _lat_rows_spec(tm, DA_WIDTH),
            _lat_rows_spec(tm, NA_WIDTH),
            pl.BlockSpec((1, tm, 2 * D_MODEL), lambda s, i: (s, i, 0)),
            pl.BlockSpec((1, DA_WIDTH, D_MODEL), lambda s, i: (l, 0, 0)),
            pl.BlockSpec((1, NA_WIDTH, D_MODEL), lambda s, i: (l, 0, 0)),
            pl.BlockSpec((1, D_MODEL, D_MODEL), lambda s, i: (l, 0, 0)),
            pl.BlockSpec((1, 1, D_MODEL), lambda s, i: (l, 0, 0)),
            pl.BlockSpec((N_EXPERTS, D_MODEL), lambda s, i: (0, 0)),
            pl.BlockSpec((N_EXPERTS, 1), lambda s, i: (0, 0)),
        ],
        out_specs=(
            pl.BlockSpec((1, tm, D_MODEL), lambda s, i: (s, i, 0)),
            pl.BlockSpec((1, tm, D_MODEL), lambda s, i: (s, i, 0)),
            pl.BlockSpec((1, tm, LANES), lambda s, i: (s, i, 0)),
        ),
        compiler_params=_params("arbitrary", "arbitrary"),
        name="output_router",
    )(x_ctx, x_lat, mods, o_ctx, o_da, o_na, gates, wa_b, wb_b, wo_b, g_norm2, w_router_t, b_router_c)


MOE_CHUNK = EXPERTS_PER_GROUP * D_EXPERT


def _moe_kernel(h_ref, comb_ref, x1_ref, mod_ref, wg_ref, wu_ref, wd_ref, oc_ref, ol_ref):
    is_ctx = pl.program_id(0) == 0
    h = h_ref[0]
    comb = comb_ref[0]
    acc = jnp.zeros(x1_ref.shape[1:], F32)
    for c in range(N_GROUPS):
        parts = []
        for j in range(EXPERTS_PER_GROUP):
            e = c * EXPERTS_PER_GROUP + j
            g = _dot(h, wg_ref[0, e])
            u = _dot(h, wu_ref[0, e])
            parts.append(g * jax.nn.sigmoid(g) * u * comb[:, e:e + 1])
        hs = jnp.concatenate(parts, axis=1).astype(BF16)
        acc = acc + _dot(hs, wd_ref[0, c * MOE_CHUNK:(c + 1) * MOE_CHUNK, :])
    out = x1_ref[0] + mod_ref[0][5:6] * acc

    @pl.when(is_ctx)
    def _():
        oc_ref[0] = out

    @pl.when(jnp.logical_not(is_ctx))
    def _():
        ol_ref[0] = out


def _experts(l, h2, comb, x1, mods, wg_all, wu_all, wd_all):
    tm = 256
    hidden = N_EXPERTS * D_EXPERT
    once = pl.Buffered(1)
    return pl.pallas_call(
        _moe_kernel,
        out_shape=(
            jax.ShapeDtypeStruct((1, SEG_ROWS, D_MODEL), F32),
            jax.ShapeDtypeStruct((DEC_BATCH, DEC_SEQ, D_MODEL), F32),
        ),
        grid=(N_SEG, SEG_ROWS // tm),
        in_specs=[
            pl.BlockSpec((1, tm, D_MODEL), lambda s, i: (s, i, 0)),
            pl.BlockSpec((1, tm, LANES), lambda s, i: (s, i, 0)),
            pl.BlockSpec((1, tm, D_MODEL), lambda s, i: (s, i, 0)),
            pl.BlockSpec((1, N_MOD, D_MODEL), lambda s, i: (s, 0, 0)),
            pl.BlockSpec((1, N_EXPERTS, D_MODEL, D_EXPERT), lambda s, i: (l, 0, 0, 0), pipeline_mode=once),
            pl.BlockSpec((1, N_EXPERTS, D_MODEL, D_EXPERT), lambda s, i: (l, 0, 0, 0), pipeline_mode=once),
            pl.BlockSpec((1, hidden, D_MODEL), lambda s, i: (l, 0, 0), pipeline_mode=once),
        ],
        out_specs=(_ctx_rows_spec(tm, D_MODEL), _lat_rows_spec(tm, D_MODEL)),
        compiler_params=_params("arbitrary", "arbitrary"),
        name="experts",
    )(h2, comb, x1, mods, wg_all, wu_all, wd_all)


def _rope_tables():
    t = np.arange(DEC_SEQ)
    pos = np.stack([t // GRID_W, t % GRID_W], axis=-1).astype(np.float32)
    inv_freq = (ROPE_BASE ** (-np.arange(ROPE_PAIRS, dtype=np.float32) / ROPE_PAIRS)).astype(np.float32)
    ang = pos[:, :, None] * inv_freq
    lane = np.arange(LANES)
    axis = (lane % HEAD_DIM) // (2 * ROPE_PAIRS)
    pair = lane % ROPE_PAIRS
    second = ((lane // ROPE_PAIRS) % 2).astype(bool)
    a = ang[:, axis, pair]
    cos = np.cos(a)
    sin = np.sin(a)
    lat = np.concatenate([cos, np.where(second, 0.0, -sin), np.where(second, sin, 0.0)], axis=1)
    ident = np.concatenate([np.ones_like(cos), np.zeros_like(cos), np.zeros_like(cos)], axis=1)
    return jnp.asarray(np.stack([ident, lat]).astype(np.float32))


def _head_block_diag():
    r = np.arange(CHUNK // 2) // HEAD_DIM
    return jnp.asarray((r[:, None] == r[None, :]).astype(np.float32), dtype=BF16)


def kernel(x_prompt, x_sample, cache_da_k, cache_da_v, cache_na_k, cache_na_v, c, c_ctx, w_mod, b_mod, g_norm1,
           g_norm2, w_in, g_q_da, g_k_da, g_q_na, g_k_na, lam_q1, lam_k1, lam_q2, lam_k2, g_subln, rpb, w_br_a,
           w_br_b, w_out, w_router, b_router, w_gate, w_up, w_down):
    L = DEPTH
    x_ctx = x_prompt.reshape(1, SEG_ROWS, D_MODEL)
    x_lat = x_sample
    cond8 = jnp.concatenate([c_ctx[None], c, jnp.zeros((8 - N_SEG, D_MODEL), F32)], axis=0)
    w_in_b = w_in.astype(BF16)
    wa_b = w_br_a.astype(BF16)
    wb_b = w_br_b.astype(BF16)
    wo_b = w_out.astype(BF16)
    hidden = N_EXPERTS * D_EXPERT
    wg_all = w_gate.astype(BF16)
    wu_all = w_up.astype(BF16)
    wd_all = w_down.astype(BF16).reshape(L, hidden, D_MODEL)
    w_router_t = w_router.T
    b_router_c = b_router.reshape(N_EXPERTS, 1)
    head_gains = jnp.stack([jnp.tile(g, (1, CHUNK // HEAD_DIM)) for g in (g_q_da, g_k_da, g_q_na, g_k_na)],
                           axis=1).reshape(L, 4, CHUNK)
    g_norm1 = g_norm1.reshape(L, 1, D_MODEL)
    g_norm2 = g_norm2.reshape(L, 1, D_MODEL)
    g_subln = g_subln.reshape(L, 1, LANES)
    pad64 = lambda t: jnp.pad(t, ((0, 0), (0, LANES - HEAD_DIM)))
    lam_inits = jnp.asarray([0.8 - 0.6 * math.exp(-0.3 * l) for l in range(L)], F32)
    lam_tab = jnp.stack([pad64(lam_q1), pad64(lam_k1), pad64(lam_q2), pad64(lam_k2),
                         jnp.broadcast_to(lam_inits[:, None], (L, LANES)),
                         jnp.zeros((L, LANES), F32), jnp.zeros((L, LANES), F32), jnp.zeros((L, LANES), F32)],
                        axis=1)
    ctx_da_k = cache_da_k.transpose(1, 0, 4, 2, 3, 5).reshape(L, DEC_BATCH, PAST_LEN, DA_WIDTH).astype(BF16)
    ctx_da_v = cache_da_v.transpose(1, 0, 3, 2, 4).reshape(L, DEC_BATCH, PAST_LEN, DA_WIDTH).astype(BF16)
    ctx_na_k = cache_na_k.transpose(1, 0, 3, 2, 4).reshape(L, DEC_BATCH, PAST_LEN, NA_WIDTH).astype(BF16)
    ctx_na_v = cache_na_v.transpose(1, 0, 3, 2, 4).reshape(L, DEC_BATCH, PAST_LEN, NA_WIDTH).astype(BF16)
    na_bias = _na_bias_tables(rpb)
    rope_tab = _rope_tables()
    bd = _head_block_diag()

    mods_all = _modulation(cond8, w_mod, b_mod)[:, :N_SEG].reshape(L, N_SEG, N_MOD, D_MODEL)

    kv_layers = []
    for l in range(L):
        mods = mods_all[l]
        qkv, gates, kvf = _in_projection(l, x_ctx, x_lat, mods, g_norm1, w_in_b, head_gains, bd, rope_tab)
        o_ctx = _attention_ctx(l, qkv, kvf, lam_tab, g_subln).reshape(1, SEG_ROWS, D_MODEL)
        o_da = _attention_da(l, qkv, ctx_da_k, ctx_da_v, lam_tab, g_subln)
        o_na = _attention_na(l, qkv, ctx_na_k, ctx_na_v, na_bias)
        x1, h2, comb = _output_stage(l, x_ctx, x_lat, mods, o_ctx, o_da, o_na, gates, wa_b, wb_b, wo_b, g_norm2,
                                     w_router_t, b_router_c)
        x_ctx, x_lat = _experts(l, h2, comb, x1, mods, wg_all, wu_all, wd_all)
        kv_layers.append(kvf[0])

    y_prompt = x_ctx.reshape(BATCH, SEQ, D_MODEL)
    y_sample = x_lat
    kv = jnp.stack(kv_layers, axis=1).reshape(BATCH, SEQ, L, 4 * CHUNK).transpose(0, 2, 1, 3)
    new_da_k = kv[..., 0:CHUNK].reshape(BATCH, L, SEQ, 2, DA_HEADS, HEAD_DIM).transpose(0, 1, 3, 4, 2, 5)
    new_da_v = kv[..., CHUNK:2 * CHUNK].reshape(BATCH, L, SEQ, DA_HEADS, 2 * HEAD_DIM).transpose(0, 1, 3, 2, 4)
    new_na_k = kv[..., 2 * CHUNK:3 * CHUNK].reshape(BATCH, L, SEQ, NA_HEADS, HEAD_DIM).transpose(0, 1, 3, 2, 4)
    new_na_v = kv[..., 3 * CHUNK:4 * CHUNK].reshape(BATCH, L, SEQ, NA_HEADS, HEAD_DIM).transpose(0, 1, 3, 2, 4)
    return (y_prompt, y_sample, new_da_k, new_da_v, new_na_k, new_na_v)
```

```python
import functools
import math

import numpy as np
import jax
import jax.numpy as jnp
from jax import lax
from jax.experimental import pallas as pl
from jax.experimental.pallas import tpu as pltpu

D_MODEL = 1024
BATCH = 16
SEQ = 256
DEPTH = 4
DEC_BATCH = 4
DEC_SEQ = 4096
PAST_LEN = 512
GRID_W = 64
GRID_H = DEC_SEQ // GRID_W
HEAD_DIM = 64
DA_HEADS = 4
DA_WIDTH = 512
NA_HEADS = 8
NA_WIDTH = 512
NA_ROWS = 8
NA_COLS = 16
IN_COLS = 3 * DA_WIDTH + 3 * NA_WIDTH + 2 * D_MODEL
ROPE_BASE = 10000.0
ROPE_PAIRS = HEAD_DIM // 4
N_EXPERTS = 16
N_GROUPS = 4
EXPERTS_PER_GROUP = 4
D_EXPERT = 256
N_MOD = 6
EPS = 1e-6
ATTN_SCALE = HEAD_DIM ** -0.5
Q_PRESCALE = ATTN_SCALE * math.log2(math.e)
LOG2E = math.log2(math.e)

N_SEG = 1 + DEC_BATCH
SEG_ROWS = DEC_SEQ
LANES = 128
CHUNK = 512

VMEM_LIMIT = 56 * 1024 * 1024

F32 = jnp.float32
BF16 = jnp.bfloat16


def _dot(a, b):
    return jnp.dot(a, b, preferred_element_type=F32)


def _dot_nt(a, b):
    return lax.dot_general(a, b, (((1,), (1,)), ((), ())), preferred_element_type=F32)


def _params(*sem):
    return pltpu.CompilerParams(dimension_semantics=sem, vmem_limit_bytes=VMEM_LIMIT)


def _ctx_rows_spec(tm, width):
    nt = SEG_ROWS // tm
    return pl.BlockSpec((1, tm, width), lambda s, i: (0, jnp.where(s == 0, i, nt - 1), 0))


def _lat_rows_spec(tm, width):
    return pl.BlockSpec((1, tm, width), lambda s, i: (jnp.maximum(s - 1, 0), jnp.where(s == 0, 0, i), 0))


def _mod_kernel(cond_ref, w_ref, b_ref, o_ref):
    c = cond_ref[...]
    sc = (c * jax.nn.sigmoid(c)).astype(BF16)
    o_ref[0] = _dot(sc, w_ref[0].astype(BF16)) + b_ref[0]


def _modulation(cond8, w_mod, b_mod):
    tn = 1536
    return pl.pallas_call(
        _mod_kernel,
        out_shape=jax.ShapeDtypeStruct((DEPTH, 8, N_MOD * D_MODEL), F32),
        grid=(DEPTH, N_MOD * D_MODEL // tn),
        in_specs=[
            pl.BlockSpec((8, D_MODEL), lambda l, j: (0, 0)),
            pl.BlockSpec((1, D_MODEL, tn), lambda l, j: (l, 0, j)),
            pl.BlockSpec((1, 1, tn), lambda l, j: (l, 0, j)),
        ],
        out_specs=pl.BlockSpec((1, 8, tn), lambda l, j: (l, 0, j)),
        compiler_params=_params("arbitrary", "arbitrary"),
        name="modulation",
    )(cond8, w_mod, b_mod.reshape(DEPTH, 1, N_MOD * D_MODEL))


def _store_heads(dst_ref, b, lead, t, rows, width):
    for h in range(CHUNK // width):
        dst_ref[(b, 0) + lead + (h,)] = t[rows, h * width:(h + 1) * width]


def _inproj_kernel(xc_ref, xl_ref, mod_ref, g1_ref, w_ref, hg_ref, bd_ref, rope_ref, *rest):
    qkv_ref, gate_ref, cdk_ref, cdv_ref, cnk_ref, cnv_ref = rest[-6:]
    s = pl.program_id(0)
    x = jnp.where(s == 0, xc_ref[0], xl_ref[0])
    ms = jnp.mean(x * x, axis=-1, keepdims=True)
    y = x * lax.rsqrt(ms + EPS) * g1_ref[0]
    mod = mod_ref[0]
    h = (y * (1.0 + mod[1:2]) + mod[0:1]).astype(BF16)

    rope = rope_ref[0]
    cos4 = jnp.concatenate([rope[:, 0:LANES]] * 4, axis=1)
    sin_up4 = jnp.concatenate([rope[:, LANES:2 * LANES]] * 4, axis=1)
    sin_dn4 = jnp.concatenate([rope[:, 2 * LANES:3 * LANES]] * 4, axis=1)

    def head_norm(acc, row):
        sq = (acc * acc).astype(BF16)
        half = CHUNK // 2
        ssum = jnp.concatenate([_dot(sq[:, :half], bd_ref[...]), _dot(sq[:, half:], bd_ref[...])], axis=1)
        return acc * lax.rsqrt(ssum * (1.0 / HEAD_DIM) + EPS) * hg_ref[0, row:row + 1, :]

    def rope_rot(t):
        return (t * cos4 + pltpu.roll(t, CHUNK - ROPE_PAIRS, 1) * sin_up4
                + pltpu.roll(t, ROPE_PAIRS, 1) * sin_dn4)

    def proj(c):
        return _dot(h, w_ref[0, :, c * CHUNK:(c + 1) * CHUNK])

    is_ctx = s == 0
    batches = [(b, slice(b * SEQ, (b + 1) * SEQ)) for b in range(x.shape[0] // SEQ)]

    q = rope_rot(head_norm(proj(0), 0)) * Q_PRESCALE
    qkv_ref[0, :, 0:CHUNK] = q.astype(BF16)

    k = rope_rot(head_norm(proj(1), 1))
    qkv_ref[0, :, CHUNK:2 * CHUNK] = k.astype(BF16)

    @pl.when(is_ctx)
    def _():
        for b, rows in batches:
            for m in range(2):
                half = k[:, m * (CHUNK // 2):(m + 1) * (CHUNK // 2)]
                for h in range(DA_HEADS):
                    cdk_ref[b, 0, m, h] = half[rows, h * HEAD_DIM:(h + 1) * HEAD_DIM]

    v = proj(2)
    qkv_ref[0, :, 2 * CHUNK:3 * CHUNK] = v.astype(BF16)

    @pl.when(is_ctx)
    def _():
        for b, rows in batches:
            _store_heads(cdv_ref, b, (), v, rows, 2 * HEAD_DIM)

    q = head_norm(proj(3), 2) * Q_PRESCALE
    qkv_ref[0, :, 3 * CHUNK:4 * CHUNK] = q.astype(BF16)

    k = head_norm(proj(4), 3)
    qkv_ref[0, :, 4 * CHUNK:5 * CHUNK] = k.astype(BF16)

    @pl.when(is_ctx)
    def _():
        for b, rows in batches:
            _store_heads(cnk_ref, b, (), k, rows, HEAD_DIM)

    v = proj(5)
    qkv_ref[0, :, 5 * CHUNK:6 * CHUNK] = v.astype(BF16)

    @pl.when(is_ctx)
    def _():
        for b, rows in batches:
            _store_heads(cnv_ref, b, (), v, rows, HEAD_DIM)

    for c in range(4):
        g = proj(6 + c)
        gate_ref[0, :, c * CHUNK:(c + 1) * CHUNK] = jax.nn.sigmoid(g).astype(BF16)


CACHE_SHAPES = (
    (BATCH, DEPTH, 2, DA_HEADS, SEQ, HEAD_DIM),
    (BATCH, DEPTH, DA_HEADS, SEQ, 2 * HEAD_DIM),
    (BATCH, DEPTH, NA_HEADS, SEQ, HEAD_DIM),
    (BATCH, DEPTH, NA_HEADS, SEQ, HEAD_DIM),
)


def _in_projection(l, x_ctx, x_lat, mods, g_norm1, w_in_b, head_gains, bd, rope_tab, caches):
    tm = 512
    nt = SEG_ROWS // tm
    nb = tm // SEQ

    def cache_spec(shape):
        block = (nb, 1) + shape[2:]
        zeros = (0,) * (len(shape) - 2)
        return pl.BlockSpec(block, lambda s, i: (jnp.where(s == 0, i, nt - 1), l) + zeros)

    chained = caches is not None
    cache_in = list(caches) if chained else []
    n_in = 8
    return pl.pallas_call(
        _inproj_kernel,
        out_shape=(
            jax.ShapeDtypeStruct((N_SEG, SEG_ROWS, 6 * CHUNK), BF16),
            jax.ShapeDtypeStruct((N_SEG, SEG_ROWS, 2 * D_MODEL), BF16),
        ) + tuple(jax.ShapeDtypeStruct(shape, F32) for shape in CACHE_SHAPES),
        grid=(N_SEG, nt),
        in_specs=[
            _ctx_rows_spec(tm, D_MODEL),
            _lat_rows_spec(tm, D_MODEL),
            pl.BlockSpec((1, N_MOD, D_MODEL), lambda s, i: (s, 0, 0)),
            pl.BlockSpec((1, 1, D_MODEL), lambda s, i: (l, 0, 0)),
            pl.BlockSpec((1, D_MODEL, IN_COLS), lambda s, i: (l, 0, 0), pipeline_mode=pl.Buffered(1)),
            pl.BlockSpec((1, 4, CHUNK), lambda s, i: (l, 0, 0)),
            pl.BlockSpec((CHUNK // 2, CHUNK // 2), lambda s, i: (0, 0)),
            pl.BlockSpec((1, tm, 3 * LANES), lambda s, i: (jnp.minimum(s, 1), i, 0)),
        ] + [pl.BlockSpec(memory_space=pl.ANY) for _ in cache_in],
        out_specs=(
            pl.BlockSpec((1, tm, 6 * CHUNK), lambda s, i: (s, i, 0)),
            pl.BlockSpec((1, tm, 2 * D_MODEL), lambda s, i: (s, i, 0)),
        ) + tuple(cache_spec(shape) for shape in CACHE_SHAPES),
        input_output_aliases={n_in + j: 2 + j for j in range(len(cache_in))},
        compiler_params=_params("arbitrary", "arbitrary"),
        name="in_projection",
    )(x_ctx, x_lat, mods, g_norm1, w_in_b, head_gains, bd, rope_tab, *cache_in)


def _lam_value(lam_ref):
    lp = lam_ref[0]
    s1 = jnp.sum(lp[0:1] * lp[1:2], axis=-1, keepdims=True)
    s2 = jnp.sum(lp[2:3] * lp[3:4], axis=-1, keepdims=True)
    lam_init = lp[4:5, 0:1]
    return jnp.exp(s1) - jnp.exp(s2) + lam_init, 1.0 - lam_init


def _half_masks():
    lane = lax.broadcasted_iota(jnp.int32, (1, LANES), 1)
    return (lane < HEAD_DIM, lane >= HEAD_DIM)


def _pick_head(mask, t):
    return jnp.where(mask, t, jnp.zeros_like(t))


def _sub_layer_norm(o, gsub, one_minus):
    ms = jnp.mean(o * o, axis=-1, keepdims=True)
    return o * lax.rsqrt(ms + EPS) * gsub * one_minus


def _attn_ctx_kernel(qkv_ref, lam_ref, gsub_ref, o_ref):
    lam, one_minus = _lam_value(lam_ref)
    masks = _half_masks()
    qda = qkv_ref[0, :, 0:CHUNK]
    qna = qkv_ref[0, :, 3 * CHUNK:4 * CHUNK]
    gsub = gsub_ref[0]

    def kv(c0):
        off = c0 + CHUNK if c0 < 2 * CHUNK else c0 + 2 * CHUNK
        return qkv_ref[0, :, off:off + LANES]

    def softmax_parts(sc):
        m = jnp.max(sc, axis=-1, keepdims=True)
        e = jnp.exp2(sc - m)
        return e, 1.0 / jnp.sum(e, axis=-1, keepdims=True)

    for h in range(DA_HEADS):
        p, sub = divmod(h, 2)
        q1 = _pick_head(masks[sub], qda[:, p * LANES:(p + 1) * LANES])
        q2 = _pick_head(masks[sub], qda[:, 2 * LANES + p * LANES:2 * LANES + (p + 1) * LANES])
        e1, r1 = softmax_parts(_dot_nt(q1, kv(p * LANES)))
        e2, r2 = softmax_parts(_dot_nt(q2, kv(2 * LANES + p * LANES)))
        a = e1 * r1 - e2 * (lam * r2)
        o = _dot(a.astype(BF16), kv(CHUNK + h * LANES))
        o_ref[:, h * LANES:(h + 1) * LANES] = _sub_layer_norm(o, gsub, one_minus).astype(BF16)

    for p in range(NA_HEADS // 2):
        kp = kv(2 * CHUNK + p * LANES)
        vp = kv(3 * CHUNK + p * LANES)
        both = None
        for sub in range(2):
            qm = _pick_head(masks[sub], qna[:, p * LANES:(p + 1) * LANES])
            e, r = softmax_parts(_dot_nt(qm, kp))
            o = _dot(e.astype(BF16), vp) * r
            both = o if sub == 0 else jnp.where(masks[1], o, both)
        o_ref[:, DA_WIDTH + p * LANES:DA_WIDTH + (p + 1) * LANES] = both.astype(BF16)


def _attention_ctx(l, qkv, lam_tab, g_subln):
    return pl.pallas_call(
        _attn_ctx_kernel,
        out_shape=jax.ShapeDtypeStruct((SEG_ROWS, D_MODEL), BF16),
        grid=(BATCH,),
        in_specs=[
            pl.BlockSpec((1, SEQ, 6 * CHUNK), lambda b: (0, b, 0)),
            pl.BlockSpec((1, 8, LANES), lambda b: (l, 0, 0)),
            pl.BlockSpec((1, 1, LANES), lambda b: (l, 0, 0)),
        ],
        out_specs=pl.BlockSpec((SEQ, D_MODEL), lambda b: (b, 0)),
        compiler_params=_params("arbitrary"),
        name="attention_ctx",
    )(qkv, lam_tab, g_subln)


DA_KEYS = DEC_SEQ + PAST_LEN
DA_KCHUNK = 512
DA_TQ = 256


def _attn_da_kernel(q_ref, k_ref, v_ref, kc_ref, vc_ref, lam_ref, gsub_ref, o_ref, vt_ref, e1_ref, e2_ref):
    n_lat = DEC_SEQ // DA_KCHUNK
    n_chunks = DA_KEYS // DA_KCHUNK

    @pl.when(pl.program_id(1) == 0)
    def _():
        for c in range(n_lat):
            rows = slice(c * DA_KCHUNK, (c + 1) * DA_KCHUNK)
            vt_ref[:, rows] = v_ref[0, rows, :].astype(F32).T.astype(BF16)
        vt_ref[:, DEC_SEQ:DA_KEYS] = vc_ref[0, 0].astype(F32).T.astype(BF16)

    lam, one_minus = _lam_value(lam_ref)
    masks = _half_masks()
    q = q_ref[0]
    gsub = gsub_ref[0]

    def key_chunk(c, cols):
        if c < n_lat:
            return k_ref[0, c * DA_KCHUNK:(c + 1) * DA_KCHUNK, cols]
        return kc_ref[0, 0, :, cols]

    def score_map(qm, cols, e_ref, weight):
        maxes, sums = [], []
        for c in range(n_chunks):
            s = _dot_nt(key_chunk(c, cols), qm)
            mc = jnp.max(s, axis=0, keepdims=True)
            e = jnp.exp2(s - mc)
            sums.append(jnp.sum(e, axis=0, keepdims=True))
            maxes.append(mc)
            e_ref[c * DA_KCHUNK:(c + 1) * DA_KCHUNK, :] = e.astype(BF16)
        m = functools.reduce(jnp.maximum, maxes)
        facs = [jnp.exp2(mc - m) for mc in maxes]
        tot = functools.reduce(lambda a, b: a + b, [lc * fc for lc, fc in zip(sums, facs)])
        w = weight / tot
        return [(fc * w).astype(BF16) for fc in facs]

    for h in range(DA_HEADS):
        p, sub = divmod(h, 2)
        c1 = slice(p * LANES, (p + 1) * LANES)
        c2 = slice(2 * LANES + p * LANES, 2 * LANES + (p + 1) * LANES)
        f1 = score_map(_pick_head(masks[sub], q[:, c1]), c1, e1_ref, 1.0)
        f2 = score_map(_pick_head(masks[sub], q[:, c2]), c2, e2_ref, lam)
        o_t = jnp.zeros((LANES, DA_TQ), F32)
        for c in range(n_chunks):
            rows = slice(c * DA_KCHUNK, (c + 1) * DA_KCHUNK)
            a = e1_ref[rows, :] * f1[c] - e2_ref[rows, :] * f2[c]
            o_t = o_t + _dot(vt_ref[h * LANES:(h + 1) * LANES, rows], a)
        o_ref[0, :, h * LANES:(h + 1) * LANES] = _sub_layer_norm(o_t.T, gsub, one_minus).astype(BF16)


def _attention_da(l, qkv, ctx_k, ctx_v, lam_tab, g_subln):
    tq = DA_TQ
    return pl.pallas_call(
        _attn_da_kernel,
        out_shape=jax.ShapeDtypeStruct((DEC_BATCH, DEC_SEQ, DA_WIDTH), BF16),
        grid=(DEC_BATCH, DEC_SEQ // tq),
        in_specs=[
            pl.BlockSpec((1, tq, CHUNK), lambda b, i: (b + 1, i, 0)),
            pl.BlockSpec((1, DEC_SEQ, CHUNK), lambda b, i: (b + 1, 0, 1), pipeline_mode=pl.Buffered(1)),
            pl.BlockSpec((1, DEC_SEQ, CHUNK), lambda b, i: (b + 1, 0, 2), pipeline_mode=pl.Buffered(1)),
            pl.BlockSpec((1, 1, PAST_LEN, CHUNK), lambda b, i: (l, b, 0, 0)),
            pl.BlockSpec((1, 1, PAST_LEN, CHUNK), lambda b, i: (l, b, 0, 0)),
            pl.BlockSpec((1, 8, LANES), lambda b, i: (l, 0, 0)),
            pl.BlockSpec((1, 1, LANES), lambda b, i: (l, 0, 0)),
        ],
        out_specs=pl.BlockSpec((1, tq, DA_WIDTH), lambda b, i: (b, i, 0)),
        scratch_shapes=[
            pltpu.VMEM((DA_WIDTH, DA_KEYS), BF16),
            pltpu.VMEM((DA_KEYS, tq), BF16),
            pltpu.VMEM((DA_KEYS, tq), BF16),
        ],
        compiler_params=_params("arbitrary", "arbitrary"),
        name="attention_da",
    )(qkv, qkv, qkv, ctx_k, ctx_v, lam_tab, g_subln)


NA_QROWS = 8
NA_KROWS = 16


def _attn_na_kernel(q_ref, k_ref, v_ref, kc_ref, vc_ref, bias_ref, o_ref):
    i = pl.program_id(1)
    masks = _half_masks()
    q = q_ref[0]
    row0 = jnp.clip(i * NA_QROWS - NA_ROWS // 2, 0, GRID_H - NA_KROWS)
    start = pl.multiple_of(row0 * GRID_W, 256)
    nk = NA_KROWS * GRID_W

    for p in range(NA_HEADS // 2):
        c0 = p * LANES
        kw = k_ref[0, pl.ds(start, nk), c0:c0 + LANES]
        vw = v_ref[0, pl.ds(start, nk), c0:c0 + LANES]
        kc = kc_ref[0, 0, :, c0:c0 + LANES]
        vc = vc_ref[0, 0, :, c0:c0 + LANES]
        both = None
        for sub in range(2):
            qm = _pick_head(masks[sub], q[:, c0:c0 + LANES])
            s_loc = _dot_nt(qm, kw) + bias_ref[0, 0, 2 * p + sub].astype(F32)
            s_ctx = _dot_nt(qm, kc)
            m = jnp.maximum(jnp.max(s_loc, axis=-1, keepdims=True), jnp.max(s_ctx, axis=-1, keepdims=True))
            e_loc = jnp.exp2(s_loc - m)
            e_ctx = jnp.exp2(s_ctx - m)
            tot = jnp.sum(e_loc, axis=-1, keepdims=True) + jnp.sum(e_ctx, axis=-1, keepdims=True)
            o = (_dot(e_loc.astype(BF16), vw) + _dot(e_ctx.astype(BF16), vc)) * (1.0 / tot)
            both = o if sub == 0 else jnp.where(masks[1], o, both)
        o_ref[0, :, c0:c0 + LANES] = both.astype(BF16)


def _attention_na(l, qkv, ctx_k, ctx_v, bias):
    tq = NA_QROWS * GRID_W
    n_i = DEC_SEQ // tq

    def bias_idx(b, i):
        return (l, jnp.where(i == 0, 0, jnp.where(i == n_i - 1, 2, 1)), 0, 0, 0)

    return pl.pallas_call(
        _attn_na_kernel,
        out_shape=jax.ShapeDtypeStruct((DEC_BATCH, DEC_SEQ, NA_WIDTH), BF16),
        grid=(DEC_BATCH, n_i),
        in_specs=[
            pl.BlockSpec((1, tq, CHUNK), lambda b, i: (b + 1, i, 3)),
            pl.BlockSpec((1, DEC_SEQ, CHUNK), lambda b, i: (b + 1, 0, 4), pipeline_mode=pl.Buffered(1)),
            pl.BlockSpec((1, DEC_SEQ, CHUNK), lambda b, i: (b + 1, 0, 5), pipeline_mode=pl.Buffered(1)),
            pl.BlockSpec((1, 1, PAST_LEN, CHUNK), lambda b, i: (l, b, 0, 0)),
            pl.BlockSpec((1, 1, PAST_LEN, CHUNK), lambda b, i: (l, b, 0, 0)),
            pl.BlockSpec((1, 1, NA_HEADS, tq, NA_KROWS * GRID_W), bias_idx),
        ],
        out_specs=pl.BlockSpec((1, tq, NA_WIDTH), lambda b, i: (b, i, 0)),
        compiler_params=_params("arbitrary", "arbitrary"),
        name="attention_na",
    )(qkv, qkv, qkv, ctx_k, ctx_v, bias)


def _na_bias_tables(rpb):
    qr = np.arange(NA_QROWS)
    qc = np.arange(GRID_W)
    kr = np.arange(NA_KROWS)
    kc = np.arange(GRID_W)
    c0 = np.clip(qc - NA_COLS // 2, 0, GRID_W - NA_COLS)
    col_ok = (kc[None, :] >= c0[:, None]) & (kc[None, :] < c0[:, None] + NA_COLS)
    cpad = GRID_W - NA_COLS
    rpb_c = jnp.pad(rpb, ((0, 0), (0, 0), (0, 0), (cpad, cpad)))
    by_col = jnp.stack([rpb_c[..., GRID_W - 1 - c:2 * GRID_W - 1 - c] for c in range(GRID_W)], axis=-2)
    by_col = jnp.where(col_ok, by_col * LOG2E, -jnp.inf)
    by_col = by_col.transpose(0, 1, 3, 2, 4).astype(BF16)
    rpad = NA_ROWS
    by_col = jnp.pad(by_col, ((0, 0), (0, 0), (0, 0), (rpad, rpad), (0, 0)))
    tiles = []
    for i in (0, 1, GRID_H // NA_QROWS - 1):
        rows_q = NA_QROWS * i + qr
        row0_k = min(max(NA_QROWS * i - NA_ROWS // 2, 0), GRID_H - NA_KROWS)
        rows_k = row0_k + kr
        r0 = np.clip(rows_q - NA_ROWS // 2, 0, GRID_H - NA_ROWS)
        row_ok = (rows_k[None, :] >= r0[:, None]) & (rows_k[None, :] < r0[:, None] + NA_ROWS)
        strips = []
        for r in range(NA_QROWS):
            lo = row0_k - int(rows_q[r]) + NA_ROWS - 1 + rpad
            strips.append(by_col[:, :, :, lo:lo + NA_KROWS])
        tile = jnp.stack(strips, axis=2)
        tile = jnp.where(row_ok[:, None, :, None], tile, -jnp.inf)
        tiles.append(tile.reshape(DEPTH, NA_HEADS, NA_QROWS * GRID_W, NA_KROWS * GRID_W))
    return jnp.stack(tiles, axis=1)


def _first_max_of_four(vals):
    a, b, c, d = vals
    m = jnp.maximum(jnp.maximum(a, b), jnp.maximum(c, d))
    idx = jnp.where(a == m, 0.0, jnp.where(b == m, 1.0, jnp.where(c == m, 2.0, 3.0)))
    return m, idx


def _out_kernel(xc_ref, xl_ref, mod_ref, oc_ref, oda_ref, ona_ref, gate_ref, wa_ref, wb_ref, wo_ref, g2_ref,
                wrt_ref, br_ref, x1_ref, h2_ref, comb_ref):
    is_ctx = pl.program_id(0) == 0
    oc = oc_ref[0]
    oa = jnp.where(is_ctx, oc[:, :DA_WIDTH], oda_ref[0])
    ob = jnp.where(is_ctx, oc[:, DA_WIDTH:], ona_ref[0])
    ya = _dot(oa, wa_ref[0])
    yb = _dot(ob, wb_ref[0])
    g = gate_ref[0]
    mixed = (g[:, :D_MODEL].astype(F32) * ya + g[:, D_MODEL:].astype(F32) * yb).astype(BF16)
    mod = mod_ref[0]
    x1 = jnp.where(is_ctx, xc_ref[0], xl_ref[0]) + mod[2:3] * _dot(mixed, wo_ref[0])
    x1_ref[0] = x1

    ms = jnp.mean(x1 * x1, axis=-1, keepdims=True)
    h2 = x1 * lax.rsqrt(ms + EPS) * g2_ref[0] * (1.0 + mod[4:5]) + mod[3:4]
    h2_hi = h2.astype(BF16)
    h2_ref[0] = h2_hi

    h2_lo = (h2 - h2_hi.astype(F32)).astype(BF16)
    wr = wrt_ref[...]
    wr_hi = wr.astype(BF16)
    wr_lo = (wr - wr_hi.astype(F32)).astype(BF16)
    by_hi = _dot_nt(jnp.concatenate([wr_hi, wr_lo], axis=0), h2_hi)
    logits = by_hi[:N_EXPERTS] + (by_hi[N_EXPERTS:] + _dot_nt(wr_hi, h2_lo))
    scores = jax.nn.sigmoid(logits)
    sel = scores + br_ref[...]
    score_rows = [scores[e:e + 1] for e in range(N_EXPERTS)]
    sel_rows = [sel[e:e + 1] for e in range(N_EXPERTS)]

    neg = -jnp.inf
    best = None
    for grp in range(N_GROUPS):
        vals = sel_rows[grp * EXPERTS_PER_GROUP:(grp + 1) * EXPERTS_PER_GROUP]
        m1, i1 = _first_max_of_four(vals)
        rest = [jnp.where(i1 == float(j), neg, v) for j, v in enumerate(vals)]
        m2, i2 = _first_max_of_four(rest)
        cand = (m1 + m2, i1 + float(grp * EXPERTS_PER_GROUP), i2 + float(grp * EXPERTS_PER_GROUP))
        if best is None:
            best = cand
        else:
            upd = cand[0] > best[0]
            best = tuple(jnp.where(upd, n, o) for n, o in zip(cand, best))
    _, e1, e2 = best
    hits1 = [e1 == float(e) for e in range(N_EXPERTS)]
    hits2 = [e2 == float(e) for e in range(N_EXPERTS)]
    s1 = functools.reduce(lambda a, b: a + b, [jnp.where(h, r, 0.0) for h, r in zip(hits1, score_rows)])
    s2 = functools.reduce(lambda a, b: a + b, [jnp.where(h, r, 0.0) for h, r in zip(hits2, score_rows)])
    den = s1 + s2
    w1 = s1 / den
    w2 = s2 / den
    rows = [jnp.where(h1, w1, 0.0) + jnp.where(h2_, w2, 0.0) for h1, h2_ in zip(hits1, hits2)]
    rows.append(jnp.zeros((LANES - N_EXPERTS, rows[0].shape[1]), F32))
    comb_ref[0] = jnp.concatenate(rows, axis=0).T


def _output_stage(l, x_ctx, x_lat, mods, o_ctx, o_da, o_na, gates, wa_b, wb_b, wo_b, g_norm2, w_router_t,
                  b_router_c):
    tm = 256
    nt = SEG_ROWS // tm
    return pl.pallas_call(
        _out_kernel,
        out_shape=(
            jax.ShapeDtypeStruct((N_SEG, SEG_ROWS, D_MODEL), F32),
            jax.ShapeDtypeStruct((N_SEG, SEG_ROWS, D_MODEL), BF16),
            jax.ShapeDtypeStruct((N_SEG, SEG_ROWS, LANES), F32),
        ),
        grid=(N_SEG, nt),
        in_specs=[
            _ctx_rows_spec(tm, D_MODEL),
            _lat_rows_spec(tm, D_MODEL),
            pl.BlockSpec((1, N_MOD, D_MODEL), lambda s, i: (s, 0, 0)),
            _ctx_rows_spec(tm, D_MODEL),
            _lat_rows_spec(tm, DA_WIDTH),
            _lat_rows_spec(tm, NA_WIDTH),
            pl.BlockSpec((1, tm, 2 * D_MODEL), lambda s, i: (s, i, 0)),
            pl.BlockSpec((1, DA_WIDTH, D_MODEL), lambda s, i: (l, 0, 0)),
            pl.BlockSpec((1, NA_WIDTH, D_MODEL), lambda s, i: (l, 0, 0)),
            pl.BlockSpec((1, D_MODEL, D_MODEL), lambda s, i: (l, 0, 0)),
            pl.BlockSpec((1, 1, D_MODEL), lambda s, i: (l, 0, 0)),
            pl.BlockSpec((N_EXPERTS, D_MODEL), lambda s, i: (0, 0)),
            pl.BlockSpec((N_EXPERTS, 1), lambda s, i: (0, 0)),
        ],
        out_specs=(
            pl.BlockSpec((1, tm, D_MODEL), lambda s, i: (s, i, 0)),
            pl.BlockSpec((1, tm, D_MODEL), lambda s, i: (s, i, 0)),
            pl.BlockSpec((1, tm, LANES), lambda s, i: (s, i, 0)),
        ),
        compiler_params=_params("arbitrary", "arbitrary"),
        name="output_router",
    )(x_ctx, x_lat, mods, o_ctx, o_da, o_na, gates, wa_b, wb_b, wo_b, g_norm2, w_router_t, b_router_c)


MOE_CHUNK = EXPERTS_PER_GROUP * D_EXPERT


def _moe_kernel(h_ref, comb_ref, x1_ref, mod_ref, wg_ref, wu_ref, wd_ref, oc_ref, ol_ref):
    is_ctx = pl.program_id(0) == 0
    h = h_ref[0]
    comb = comb_ref[0]
    acc = jnp.zeros(x1_ref.shape[1:], F32)
    for c in range(N_GROUPS):
        parts = []
        for j in range(EXPERTS_PER_GROUP):
            e = c * EXPERTS_PER_GROUP + j
            g = _dot(h, wg_ref[0, e])
            u = _dot(h, wu_ref[0, e])
            parts.append(g * jax.nn.sigmoid(g) * u * comb[:, e:e + 1])
        hs = jnp.concatenate(parts, axis=1).astype(BF16)
        acc = acc + _dot(hs, wd_ref[0, c * MOE_CHUNK:(c + 1) * MOE_CHUNK, :])
    out = x1_ref[0] + mod_ref[0][5:6] * acc

    @pl.when(is_ctx)
    def _():
        oc_ref[0] = out

    @pl.when(jnp.logical_not(is_ctx))
    def _():
        ol_ref[0] = out


def _experts(l, h2, comb, x1, mods, wg_all, wu_all, wd_all):
    tm = 256
    hidden = N_EXPERTS * D_EXPERT
    once = pl.Buffered(1)
    return pl.pallas_call(
        _moe_kernel,
        out_shape=(
            jax.ShapeDtypeStruct((1, SEG_ROWS, D_MODEL), F32),
            jax.ShapeDtypeStruct((DEC_BATCH, DEC_SEQ, D_MODEL), F32),
        ),
        grid=(N_SEG, SEG_ROWS // tm),
        in_specs=[
            pl.BlockSpec((1, tm, D_MODEL), lambda s, i: (s, i, 0)),
            pl.BlockSpec((1, tm, LANES), lambda s, i: (s, i, 0)),
            pl.BlockSpec((1, tm, D_MODEL), lambda s, i: (s, i, 0)),
            pl.BlockSpec((1, N_MOD, D_MODEL), lambda s, i: (s, 0, 0)),
            pl.BlockSpec((1, N_EXPERTS, D_MODEL, D_EXPERT), lambda s, i: (l, 0, 0, 0), pipeline_mode=once),
            pl.BlockSpec((1, N_EXPERTS, D_MODEL, D_EXPERT), lambda s, i: (l, 0, 0, 0), pipeline_mode=once),
            pl.BlockSpec((1, hidden, D_MODEL), lambda s, i: (l, 0, 0), pipeline_mode=once),
        ],
        out_specs=(_ctx_rows_spec(tm, D_MODEL), _lat_rows_spec(tm, D_MODEL)),
        compiler_params=_params("arbitrary", "arbitrary"),
        name="experts",
    )(h2, comb, x1, mods, wg_all, wu_all, wd_all)


def _rope_tables():
    t = np.arange(DEC_SEQ)
    pos = np.stack([t // GRID_W, t % GRID_W], axis=-1).astype(np.float32)
    inv_freq = (ROPE_BASE ** (-np.arange(ROPE_PAIRS, dtype=np.float32) / ROPE_PAIRS)).astype(np.float32)
    ang = pos[:, :, None] * inv_freq
    lane = np.arange(LANES)
    axis = (lane % HEAD_DIM) // (2 * ROPE_PAIRS)
    pair = lane % ROPE_PAIRS
    second = ((lane // ROPE_PAIRS) % 2).astype(bool)
    a = ang[:, axis, pair]
    cos = np.cos(a)
    sin = np.sin(a)
    lat = np.concatenate([cos, np.where(second, 0.0, -sin), np.where(second, sin, 0.0)], axis=1)
    ident = np.concatenate([np.ones_like(cos), np.zeros_like(cos), np.zeros_like(cos)], axis=1)
    return jnp.asarray(np.stack([ident, lat]).astype(np.float32))


def _head_block_diag():
    r = np.arange(CHUNK // 2) // HEAD_DIM
    return jnp.asarray((r[:, None] == r[None, :]).astype(np.float32), dtype=BF16)


def kernel(x_prompt, x_sample, cache_da_k, cache_da_v, cache_na_k, cache_na_v, c, c_ctx, w_mod, b_mod, g_norm1,
           g_norm2, w_in, g_q_da, g_k_da, g_q_na, g_k_na, lam_q1, lam_k1, lam_q2, lam_k2, g_subln, rpb, w_br_a,
           w_br_b, w_out, w_router, b_router, w_gate, w_up, w_down):
    L = DEPTH
    x_ctx = x_prompt.reshape(1, SEG_ROWS, D_MODEL)
    x_lat = x_sample
    cond8 = jnp.concatenate([c_ctx[None], c, jnp.zeros((8 - N_SEG, D_MODEL), F32)], axis=0)
    w_in_b = w_in.astype(BF16)
    wa_b = w_br_a.astype(BF16)
    wb_b = w_br_b.astype(BF16)
    wo_b = w_out.astype(BF16)
    hidden = N_EXPERTS * D_EXPERT
    wg_all = w_gate.astype(BF16)
    wu_all = w_up.astype(BF16)
    wd_all = w_down.astype(BF16).reshape(L, hidden, D_MODEL)
    w_router_t = w_router.T
    b_router_c = b_router.reshape(N_EXPERTS, 1)
    head_gains = jnp.stack([jnp.tile(g, (1, CHUNK // HEAD_DIM)) for g in (g_q_da, g_k_da, g_q_na, g_k_na)],
                           axis=1).reshape(L, 4, CHUNK)
    g_norm1 = g_norm1.reshape(L, 1, D_MODEL)
    g_norm2 = g_norm2.reshape(L, 1, D_MODEL)
    g_subln = g_subln.reshape(L, 1, LANES)
    pad64 = lambda t: jnp.pad(t, ((0, 0), (0, LANES - HEAD_DIM)))
    lam_inits = jnp.asarray([0.8 - 0.6 * math.exp(-0.3 * l) for l in range(L)], F32)
    lam_tab = jnp.stack([pad64(lam_q1), pad64(lam_k1), pad64(lam_q2), pad64(lam_k2),
                         jnp.broadcast_to(lam_inits[:, None], (L, LANES)),
                         jnp.zeros((L, LANES), F32), jnp.zeros((L, LANES), F32), jnp.zeros((L, LANES), F32)],
                        axis=1)
    ctx_da_k = cache_da_k.transpose(1, 0, 4, 2, 3, 5).reshape(L, DEC_BATCH, PAST_LEN, DA_WIDTH).astype(BF16)
    ctx_da_v = cache_da_v.transpose(1, 0, 3, 2, 4).reshape(L, DEC_BATCH, PAST_LEN, DA_WIDTH).astype(BF16)
    ctx_na_k = cache_na_k.transpose(1, 0, 3, 2, 4).reshape(L, DEC_BATCH, PAST_LEN, NA_WIDTH).astype(BF16)
    ctx_na_v = cache_na_v.transpose(1, 0, 3, 2, 4).reshape(L, DEC_BATCH, PAST_LEN, NA_WIDTH).astype(BF16)
    na_bias = _na_bias_tables(rpb)
    rope_tab = _rope_tables()
    bd = _head_block_diag()

    mods_all = _modulation(cond8, w_mod, b_mod)[:, :N_SEG].reshape(L, N_SEG, N_MOD, D_MODEL)

    caches = None
    for l in range(L):
        mods = mods_all[l]
        qkv, gates, *caches = _in_projection(l, x_ctx, x_lat, mods, g_norm1, w_in_b, head_gains, bd, rope_tab,
                                             caches)
        o_ctx = _attention_ctx(l, qkv, lam_tab, g_subln).reshape(1, SEG_ROWS, D_MODEL)
        o_da = _attention_da(l, qkv, ctx_da_k, ctx_da_v, lam_tab, g_subln)
        o_na = _attention_na(l, qkv, ctx_na_k, ctx_na_v, na_bias)
        x1, h2, comb = _output_stage(l, x_ctx, x_lat, mods, o_ctx, o_da, o_na, gates, wa_b, wb_b, wo_b, g_norm2,
                                     w_router_t, b_router_c)
        x_ctx, x_lat = _experts(l, h2, comb, x1, mods, wg_all, wu_all, wd_all)

    y_prompt = x_ctx.reshape(BATCH, SEQ, D_MODEL)
    y_sample = x_lat
    new_da_k, new_da_v, new_na_k, new_na_v = caches
    return (y_prompt, y_sample, new_da_k, new_da_v, new_na_k, new_na_v)
```

```python
import functools
import math

import numpy as np
import jax
import jax.numpy as jnp
from jax import lax
from jax.experimental import pallas as pl
from jax.experimental.pallas import tpu as pltpu

D_MODEL = 1024
BATCH = 16
SEQ = 256
DEPTH = 4
DEC_BATCH = 4
DEC_SEQ = 4096
PAST_LEN = 512
GRID_W = 64
GRID_H = DEC_SEQ // GRID_W
HEAD_DIM = 64
DA_HEADS = 4
DA_WIDTH = 512
NA_HEADS = 8
NA_WIDTH = 512
NA_ROWS = 8
NA_COLS = 16
IN_COLS = 3 * DA_WIDTH + 3 * NA_WIDTH + 2 * D_MODEL
ROPE_BASE = 10000.0
ROPE_PAIRS = HEAD_DIM // 4
N_EXPERTS = 16
N_GROUPS = 4
EXPERTS_PER_GROUP = 4
D_EXPERT = 256
N_MOD = 6
EPS = 1e-6
ATTN_SCALE = HEAD_DIM ** -0.5
Q_PRESCALE = ATTN_SCALE * math.log2(math.e)
LOG2E = math.log2(math.e)

N_SEG = 1 + DEC_BATCH
SEG_ROWS = DEC_SEQ
LANES = 128
CHUNK = 512

VMEM_LIMIT = 56 * 1024 * 1024

F32 = jnp.float32
BF16 = jnp.bfloat16


def _dot(a, b):
    return jnp.dot(a, b, preferred_element_type=F32)


def _dot_nt(a, b):
    return lax.dot_general(a, b, (((1,), (1,)), ((), ())), preferred_element_type=F32)


def _params(*sem):
    return pltpu.CompilerParams(dimension_semantics=sem, vmem_limit_bytes=VMEM_LIMIT)


def _ctx_rows_spec(tm, width):
    nt = SEG_ROWS // tm
    return pl.BlockSpec((1, tm, width), lambda s, i: (0, jnp.where(s == 0, i, nt - 1), 0))


def _lat_rows_spec(tm, width):
    return pl.BlockSpec((1, tm, width), lambda s, i: (jnp.maximum(s - 1, 0), jnp.where(s == 0, 0, i), 0))


def _mod_kernel(cond_ref, w_ref, b_ref, o_ref):
    c = cond_ref[...]
    sc = (c * jax.nn.sigmoid(c)).astype(BF16)
    o_ref[0] = _dot(sc, w_ref[0].astype(BF16)) + b_ref[0]


def _modulation(cond8, w_mod, b_mod):
    tn = 1536
    return pl.pallas_call(
        _mod_kernel,
        out_shape=jax.ShapeDtypeStruct((DEPTH, 8, N_MOD * D_MODEL), F32),
        grid=(DEPTH, N_MOD * D_MODEL // tn),
        in_specs=[
            pl.BlockSpec((8, D_MODEL), lambda l, j: (0, 0)),
            pl.BlockSpec((1, D_MODEL, tn), lambda l, j: (l, 0, j)),
            pl.BlockSpec((1, 1, tn), lambda l, j: (l, 0, j)),
        ],
        out_specs=pl.BlockSpec((1, 8, tn), lambda l, j: (l, 0, j)),
        compiler_params=_params("arbitrary", "arbitrary"),
        name="modulation",
    )(cond8, w_mod, b_mod.reshape(DEPTH, 1, N_MOD * D_MODEL))


def _store_heads(dst_ref, b, lead, t, rows, width):
    for h in range(CHUNK // width):
        dst_ref[(b, 0) + lead + (h,)] = t[rows, h * width:(h + 1) * width]


def _inproj_kernel(xc_ref, xl_ref, mod_ref, g1_ref, w_ref, hg_ref, bd_ref, rope_ref, *rest):
    qkv_ref, gate_ref, cdk_ref, cdv_ref, cnk_ref, cnv_ref = rest[-6:]
    s = pl.program_id(0)
    x = jnp.where(s == 0, xc_ref[0], xl_ref[0])
    ms = jnp.mean(x * x, axis=-1, keepdims=True)
    y = x * lax.rsqrt(ms + EPS) * g1_ref[0]
    mod = mod_ref[0]
    h = (y * (1.0 + mod[1:2]) + mod[0:1]).astype(BF16)

    rope = rope_ref[0]
    cos4 = jnp.concatenate([rope[:, 0:LANES]] * 4, axis=1)
    sin_up4 = jnp.concatenate([rope[:, LANES:2 * LANES]] * 4, axis=1)
    sin_dn4 = jnp.concatenate([rope[:, 2 * LANES:3 * LANES]] * 4, axis=1)

    def head_norm(acc, row):
        sq = (acc * acc).astype(BF16)
        half = CHUNK // 2
        ssum = jnp.concatenate([_dot(sq[:, :half], bd_ref[...]), _dot(sq[:, half:], bd_ref[...])], axis=1)
        return acc * lax.rsqrt(ssum * (1.0 / HEAD_DIM) + EPS) * hg_ref[0, row:row + 1, :]

    def rope_rot(t):
        return (t * cos4 + pltpu.roll(t, CHUNK - ROPE_PAIRS, 1) * sin_up4
                + pltpu.roll(t, ROPE_PAIRS, 1) * sin_dn4)

    def proj(c):
        return _dot(h, w_ref[0, :, c * CHUNK:(c + 1) * CHUNK])

    is_ctx = s == 0
    batches = [(b, slice(b * SEQ, (b + 1) * SEQ)) for b in range(x.shape[0] // SEQ)]

    q = rope_rot(head_norm(proj(0), 0)) * Q_PRESCALE
    qkv_ref[0, :, 0:CHUNK] = q.astype(BF16)

    k = rope_rot(head_norm(proj(1), 1))
    qkv_ref[0, :, CHUNK:2 * CHUNK] = k.astype(BF16)

    @pl.when(is_ctx)
    def _():
        for b, rows in batches:
            for m in range(2):
                half = k[:, m * (CHUNK // 2):(m + 1) * (CHUNK // 2)]
                for h in range(DA_HEADS):
                    cdk_ref[b, 0, m, h] = half[rows, h * HEAD_DIM:(h + 1) * HEAD_DIM]

    v = proj(2)
    qkv_ref[0, :, 2 * CHUNK:3 * CHUNK] = v.astype(BF16)

    @pl.when(is_ctx)
    def _():
        for b, rows in batches:
            _store_heads(cdv_ref, b, (), v, rows, 2 * HEAD_DIM)

    q = head_norm(proj(3), 2) * Q_PRESCALE
    qkv_ref[0, :, 3 * CHUNK:4 * CHUNK] = q.astype(BF16)

    k = head_norm(proj(4), 3)
    qkv_ref[0, :, 4 * CHUNK:5 * CHUNK] = k.astype(BF16)

    @pl.when(is_ctx)
    def _():
        for b, rows in batches:
            _store_heads(cnk_ref, b, (), k, rows, HEAD_DIM)

    v = proj(5)
    qkv_ref[0, :, 5 * CHUNK:6 * CHUNK] = v.astype(BF16)

    @pl.when(is_ctx)
    def _():
        for b, rows in batches:
            _store_heads(cnv_ref, b, (), v, rows, HEAD_DIM)

    for c in range(4):
        g = proj(6 + c)
        gate_ref[0, :, c * CHUNK:(c + 1) * CHUNK] = jax.nn.sigmoid(g).astype(BF16)


CACHE_SHAPES = (
    (BATCH, DEPTH, 2, DA_HEADS, SEQ, HEAD_DIM),
    (BATCH, DEPTH, DA_HEADS, SEQ, 2 * HEAD_DIM),
    (BATCH, DEPTH, NA_HEADS, SEQ, HEAD_DIM),
    (BATCH, DEPTH, NA_HEADS, SEQ, HEAD_DIM),
)


def _in_projection(l, x_ctx, x_lat, mods, g_norm1, w_in_b, head_gains, bd, rope_tab, caches):
    tm = 512
    nt = SEG_ROWS // tm
    nb = tm // SEQ

    def cache_spec(shape):
        block = (nb, 1) + shape[2:]
        zeros = (0,) * (len(shape) - 2)
        return pl.BlockSpec(block, lambda s, i: (jnp.where(s == 0, i, nt - 1), l) + zeros)

    chained = caches is not None
    cache_in = list(caches) if chained else []
    n_in = 8
    return pl.pallas_call(
        _inproj_kernel,
        out_shape=(
            jax.ShapeDtypeStruct((N_SEG, SEG_ROWS, 6 * CHUNK), BF16),
            jax.ShapeDtypeStruct((N_SEG, SEG_ROWS, 2 * D_MODEL), BF16),
        ) + tuple(jax.ShapeDtypeStruct(shape, F32) for shape in CACHE_SHAPES),
        grid=(N_SEG, nt),
        in_specs=[
            _ctx_rows_spec(tm, D_MODEL),
            _lat_rows_spec(tm, D_MODEL),
            pl.BlockSpec((1, N_MOD, D_MODEL), lambda s, i: (s, 0, 0)),
            pl.BlockSpec((1, 1, D_MODEL), lambda s, i: (l, 0, 0)),
            pl.BlockSpec((1, D_MODEL, IN_COLS), lambda s, i: (l, 0, 0), pipeline_mode=pl.Buffered(1)),
            pl.BlockSpec((1, 4, CHUNK), lambda s, i: (l, 0, 0)),
            pl.BlockSpec((CHUNK // 2, CHUNK // 2), lambda s, i: (0, 0)),
            pl.BlockSpec((1, tm, 3 * LANES), lambda s, i: (jnp.minimum(s, 1), i, 0)),
        ] + [pl.BlockSpec(memory_space=pl.ANY) for _ in cache_in],
        out_specs=(
            pl.BlockSpec((1, tm, 6 * CHUNK), lambda s, i: (s, i, 0)),
            pl.BlockSpec((1, tm, 2 * D_MODEL), lambda s, i: (s, i, 0)),
        ) + tuple(cache_spec(shape) for shape in CACHE_SHAPES),
        input_output_aliases={n_in + j: 2 + j for j in range(len(cache_in))},
        compiler_params=_params("arbitrary", "arbitrary"),
        name="in_projection",
    )(x_ctx, x_lat, mods, g_norm1, w_in_b, head_gains, bd, rope_tab, *cache_in)


def _lam_value(lam_ref):
    lp = lam_ref[0]
    s1 = jnp.sum(lp[0:1] * lp[1:2], axis=-1, keepdims=True)
    s2 = jnp.sum(lp[2:3] * lp[3:4], axis=-1, keepdims=True)
    lam_init = lp[4:5, 0:1]
    return jnp.exp(s1) - jnp.exp(s2) + lam_init, 1.0 - lam_init


def _half_masks():
    lane = lax.broadcasted_iota(jnp.int32, (1, LANES), 1)
    return (lane < HEAD_DIM, lane >= HEAD_DIM)


def _pick_head(mask, t):
    return jnp.where(mask, t, jnp.zeros_like(t))


def _sub_layer_norm(o, gsub, one_minus):
    ms = jnp.mean(o * o, axis=-1, keepdims=True)
    return o * lax.rsqrt(ms + EPS) * gsub * one_minus


def _attn_ctx_kernel(qkv_ref, lam_ref, gsub_ref, o_ref):
    lam, one_minus = _lam_value(lam_ref)
    masks = _half_masks()
    qda = qkv_ref[0, :, 0:CHUNK]
    qna = qkv_ref[0, :, 3 * CHUNK:4 * CHUNK]
    gsub = gsub_ref[0]

    def kv(c0):
        off = c0 + CHUNK if c0 < 2 * CHUNK else c0 + 2 * CHUNK
        return qkv_ref[0, :, off:off + LANES]

    def softmax_parts(sc):
        m = jnp.max(sc, axis=-1, keepdims=True)
        e = jnp.exp2(sc - m)
        return e, 1.0 / jnp.sum(e, axis=-1, keepdims=True)

    for h in range(DA_HEADS):
        p, sub = divmod(h, 2)
        q1 = _pick_head(masks[sub], qda[:, p * LANES:(p + 1) * LANES])
        q2 = _pick_head(masks[sub], qda[:, 2 * LANES + p * LANES:2 * LANES + (p + 1) * LANES])
        e1, r1 = softmax_parts(_dot_nt(q1, kv(p * LANES)))
        e2, r2 = softmax_parts(_dot_nt(q2, kv(2 * LANES + p * LANES)))
        a = e1 * r1 - e2 * (lam * r2)
        o = _dot(a.astype(BF16), kv(CHUNK + h * LANES))
        o_ref[:, h * LANES:(h + 1) * LANES] = _sub_layer_norm(o, gsub, one_minus).astype(BF16)

    for p in range(NA_HEADS // 2):
        kp = kv(2 * CHUNK + p * LANES)
        vp = kv(3 * CHUNK + p * LANES)
        both = None
        for sub in range(2):
            qm = _pick_head(masks[sub], qna[:, p * LANES:(p + 1) * LANES])
            e, r = softmax_parts(_dot_nt(qm, kp))
            o = _dot(e.astype(BF16), vp) * r
            both = o if sub == 0 else jnp.where(masks[1], o, both)
        o_ref[:, DA_WIDTH + p * LANES:DA_WIDTH + (p + 1) * LANES] = both.astype(BF16)


def _attention_ctx(l, qkv, lam_tab, g_subln):
    return pl.pallas_call(
        _attn_ctx_kernel,
        out_shape=jax.ShapeDtypeStruct((SEG_ROWS, D_MODEL), BF16),
        grid=(BATCH,),
        in_specs=[
            pl.BlockSpec((1, SEQ, 6 * CHUNK), lambda b: (0, b, 0)),
            pl.BlockSpec((1, 8, LANES), lambda b: (l, 0, 0)),
            pl.BlockSpec((1, 1, LANES), lambda b: (l, 0, 0)),
        ],
        out_specs=pl.BlockSpec((SEQ, D_MODEL), lambda b: (b, 0)),
        compiler_params=_params("arbitrary"),
        name="attention_ctx",
    )(qkv, lam_tab, g_subln)


DA_KEYS = DEC_SEQ + PAST_LEN
DA_KCHUNK = 1024
DA_TQ = 256


def _pair_heads(head_ref, lead, first):
    a = head_ref[lead + (first,)]
    b = head_ref[lead + (first + 1,)]
    return jnp.concatenate([a, b], axis=1).astype(BF16)


def _attn_da_kernel(q_ref, k_ref, v_ref, kc_ref, vc_ref, lam_ref, gsub_ref, o_ref, vt_ref, kcs_ref, e1_ref, e2_ref):
    chunks = [slice(r, r + DA_KCHUNK) for r in range(0, DEC_SEQ, DA_KCHUNK)] + [slice(DEC_SEQ, DA_KEYS)]
    n_chunks = len(chunks)

    @pl.when(pl.program_id(1) == 0)
    def _():
        for r in range(0, DEC_SEQ, PAST_LEN):
            rows = slice(r, r + PAST_LEN)
            vt_ref[:, rows] = v_ref[0, rows, :].astype(F32).T.astype(BF16)
        for h in range(DA_HEADS):
            vt_ref[h * LANES:(h + 1) * LANES, DEC_SEQ:DA_KEYS] = vc_ref[0, 0, h].T.astype(BF16)
        for m in range(2):
            for p in range(DA_HEADS // 2):
                c0 = m * 2 * LANES + p * LANES
                kcs_ref[:, c0:c0 + LANES] = _pair_heads(kc_ref, (0, 0, m), 2 * p)

    lam, one_minus = _lam_value(lam_ref)
    masks = _half_masks()
    q = q_ref[0]
    gsub = gsub_ref[0]

    def key_chunk(c, cols):
        if chunks[c].start < DEC_SEQ:
            return k_ref[0, chunks[c], cols]
        return kcs_ref[:, cols]

    def score_map(qm, cols, e_ref, weight):
        maxes, sums = [], []
        for c in range(n_chunks):
            s = _dot_nt(key_chunk(c, cols), qm)
            mc = jnp.max(s, axis=0, keepdims=True)
            e = jnp.exp2(s - mc)
            sums.append(jnp.sum(e, axis=0, keepdims=True))
            maxes.append(mc)
            e_ref[chunks[c], :] = e.astype(BF16)
        m = functools.reduce(jnp.maximum, maxes)
        facs = [jnp.exp2(mc - m) for mc in maxes]
        tot = functools.reduce(lambda a, b: a + b, [lc * fc for lc, fc in zip(sums, facs)])
        w = weight / tot
        return [(fc * w).astype(BF16) for fc in facs]

    for h in range(DA_HEADS):
        p, sub = divmod(h, 2)
        c1 = slice(p * LANES, (p + 1) * LANES)
        c2 = slice(2 * LANES + p * LANES, 2 * LANES + (p + 1) * LANES)
        f1 = score_map(_pick_head(masks[sub], q[:, c1]), c1, e1_ref, 1.0)
        f2 = score_map(_pick_head(masks[sub], q[:, c2]), c2, e2_ref, lam)
        o_t = jnp.zeros((LANES, DA_TQ), F32)
        for c, rows in enumerate(chunks):
            a = e1_ref[rows, :] * f1[c] - e2_ref[rows, :] * f2[c]
            o_t = o_t + _dot(vt_ref[h * LANES:(h + 1) * LANES, rows], a)
        o_ref[0, :, h * LANES:(h + 1) * LANES] = _sub_layer_norm(o_t.T, gsub, one_minus).astype(BF16)


def _attention_da(l, qkv, ctx_k, ctx_v, lam_tab, g_subln):
    tq = DA_TQ
    return pl.pallas_call(
        _attn_da_kernel,
        out_shape=jax.ShapeDtypeStruct((DEC_BATCH, DEC_SEQ, DA_WIDTH), BF16),
        grid=(DEC_BATCH, DEC_SEQ // tq),
        in_specs=[
            pl.BlockSpec((1, tq, CHUNK), lambda b, i: (b + 1, i, 0)),
            pl.BlockSpec((1, DEC_SEQ, CHUNK), lambda b, i: (b + 1, 0, 1), pipeline_mode=pl.Buffered(1)),
            pl.BlockSpec((1, DEC_SEQ, CHUNK), lambda b, i: (b + 1, 0, 2), pipeline_mode=pl.Buffered(1)),
            pl.BlockSpec((1, 1, 2, DA_HEADS, PAST_LEN, HEAD_DIM), lambda b, i: (b, l, 0, 0, 0, 0)),
            pl.BlockSpec((1, 1, DA_HEADS, PAST_LEN, 2 * HEAD_DIM), lambda b, i: (b, l, 0, 0, 0)),
            pl.BlockSpec((1, 8, LANES), lambda b, i: (l, 0, 0)),
            pl.BlockSpec((1, 1, LANES), lambda b, i: (l, 0, 0)),
        ],
        out_specs=pl.BlockSpec((1, tq, DA_WIDTH), lambda b, i: (b, i, 0)),
        scratch_shapes=[
            pltpu.VMEM((DA_WIDTH, DA_KEYS), BF16),
            pltpu.VMEM((PAST_LEN, DA_WIDTH), BF16),
            pltpu.VMEM((DA_KEYS, tq), BF16),
            pltpu.VMEM((DA_KEYS, tq), BF16),
        ],
        compiler_params=_params("arbitrary", "arbitrary"),
        name="attention_da",
    )(qkv, qkv, qkv, ctx_k, ctx_v, lam_tab, g_subln)


NA_QROWS = 8
NA_KROWS = 16


def _attn_na_kernel(q_ref, k_ref, v_ref, kc_ref, vc_ref, bias_ref, o_ref, kcs_ref, vcs_ref):
    i = pl.program_id(1)

    @pl.when(i == 0)
    def _():
        for p in range(NA_HEADS // 2):
            kcs_ref[:, p * LANES:(p + 1) * LANES] = _pair_heads(kc_ref, (0, 0), 2 * p)
            vcs_ref[:, p * LANES:(p + 1) * LANES] = _pair_heads(vc_ref, (0, 0), 2 * p)

    masks = _half_masks()
    q = q_ref[0]
    row0 = jnp.clip(i * NA_QROWS - NA_ROWS // 2, 0, GRID_H - NA_KROWS)
    start = pl.multiple_of(row0 * GRID_W, 256)
    nk = NA_KROWS * GRID_W

    for p in range(NA_HEADS // 2):
        c0 = p * LANES
        kw = k_ref[0, pl.ds(start, nk), c0:c0 + LANES]
        vw = v_ref[0, pl.ds(start, nk), c0:c0 + LANES]
        kc = kcs_ref[:, c0:c0 + LANES]
        vc = vcs_ref[:, c0:c0 + LANES]
        both = None
        for sub in range(2):
            qm = _pick_head(masks[sub], q[:, c0:c0 + LANES])
            s_loc = _dot_nt(qm, kw) + bias_ref[0, 0, 2 * p + sub].astype(F32)
            s_ctx = _dot_nt(qm, kc)
            m = jnp.maximum(jnp.max(s_loc, axis=-1, keepdims=True), jnp.max(s_ctx, axis=-1, keepdims=True))
            e_loc = jnp.exp2(s_loc - m)
            e_ctx = jnp.exp2(s_ctx - m)
            tot = jnp.sum(e_loc, axis=-1, keepdims=True) + jnp.sum(e_ctx, axis=-1, keepdims=True)
            o = (_dot(e_loc.astype(BF16), vw) + _dot(e_ctx.astype(BF16), vc)) * (1.0 / tot)
            both = o if sub == 0 else jnp.where(masks[1], o, both)
        o_ref[0, :, c0:c0 + LANES] = both.astype(BF16)


def _attention_na(l, qkv, ctx_k, ctx_v, bias):
    tq = NA_QROWS * GRID_W
    n_i = DEC_SEQ // tq

    def bias_idx(b, i):
        return (l, jnp.where(i == 0, 0, jnp.where(i == n_i - 1, 2, 1)), 0, 0, 0)

    return pl.pallas_call(
        _attn_na_kernel,
        out_shape=jax.ShapeDtypeStruct((DEC_BATCH, DEC_SEQ, NA_WIDTH), BF16),
        grid=(DEC_BATCH, n_i),
        in_specs=[
            pl.BlockSpec((1, tq, CHUNK), lambda b, i: (b + 1, i, 3)),
            pl.BlockSpec((1, DEC_SEQ, CHUNK), lambda b, i: (b + 1, 0, 4), pipeline_mode=pl.Buffered(1)),
            pl.BlockSpec((1, DEC_SEQ, CHUNK), lambda b, i: (b + 1, 0, 5), pipeline_mode=pl.Buffered(1)),
            pl.BlockSpec((1, 1, NA_HEADS, PAST_LEN, HEAD_DIM), lambda b, i: (b, l, 0, 0, 0)),
            pl.BlockSpec((1, 1, NA_HEADS, PAST_LEN, HEAD_DIM), lambda b, i: (b, l, 0, 0, 0)),
            pl.BlockSpec((1, 1, NA_HEADS, tq, NA_KROWS * GRID_W), bias_idx),
        ],
        out_specs=pl.BlockSpec((1, tq, NA_WIDTH), lambda b, i: (b, i, 0)),
        scratch_shapes=[
            pltpu.VMEM((PAST_LEN, NA_WIDTH), BF16),
            pltpu.VMEM((PAST_LEN, NA_WIDTH), BF16),
        ],
        compiler_params=_params("arbitrary", "arbitrary"),
        name="attention_na",
    )(qkv, qkv, qkv, ctx_k, ctx_v, bias)


def _na_bias_tables(rpb):
    qr = np.arange(NA_QROWS)
    qc = np.arange(GRID_W)
    kr = np.arange(NA_KROWS)
    kc = np.arange(GRID_W)
    c0 = np.clip(qc - NA_COLS // 2, 0, GRID_W - NA_COLS)
    col_ok = (kc[None, :] >= c0[:, None]) & (kc[None, :] < c0[:, None] + NA_COLS)
    cpad = GRID_W - NA_COLS
    rpb_c = jnp.pad(rpb, ((0, 0), (0, 0), (0, 0), (cpad, cpad)))
    by_col = jnp.stack([rpb_c[..., GRID_W - 1 - c:2 * GRID_W - 1 - c] for c in range(GRID_W)], axis=-2)
    by_col = jnp.where(col_ok, by_col * LOG2E, -jnp.inf)
    by_col = by_col.transpose(0, 1, 3, 2, 4).astype(BF16)
    rpad = NA_ROWS
    by_col = jnp.pad(by_col, ((0, 0), (0, 0), (0, 0), (rpad, rpad), (0, 0)))
    tiles = []
    for i in (0, 1, GRID_H // NA_QROWS - 1):
        rows_q = NA_QROWS * i + qr
        row0_k = min(max(NA_QROWS * i - NA_ROWS // 2, 0), GRID_H - NA_KROWS)
        rows_k = row0_k + kr
        r0 = np.clip(rows_q - NA_ROWS // 2, 0, GRID_H - NA_ROWS)
        row_ok = (rows_k[None, :] >= r0[:, None]) & (rows_k[None, :] < r0[:, None] + NA_ROWS)
        strips = []
        for r in range(NA_QROWS):
            lo = row0_k - int(rows_q[r]) + NA_ROWS - 1 + rpad
            strips.append(by_col[:, :, :, lo:lo + NA_KROWS])
        tile = jnp.stack(strips, axis=2)
        tile = jnp.where(row_ok[:, None, :, None], tile, -jnp.inf)
        tiles.append(tile.reshape(DEPTH, NA_HEADS, NA_QROWS * GRID_W, NA_KROWS * GRID_W))
    return jnp.stack(tiles, axis=1)


def _first_max_of_four(vals):
    a, b, c, d = vals
    m = jnp.maximum(jnp.maximum(a, b), jnp.maximum(c, d))
    idx = jnp.where(a == m, 0.0, jnp.where(b == m, 1.0, jnp.where(c == m, 2.0, 3.0)))
    return m, idx


def _out_kernel(xc_ref, xl_ref, mod_ref, oc_ref, oda_ref, ona_ref, gate_ref, wa_ref, wb_ref, wo_ref, g2_ref,
                wrt_ref, br_ref, x1_ref, h2_ref, comb_ref):
    is_ctx = pl.program_id(0) == 0
    oc = oc_ref[0]
    oa = jnp.where(is_ctx, oc[:, :DA_WIDTH], oda_ref[0])
    ob = jnp.where(is_ctx, oc[:, DA_WIDTH:], ona_ref[0])
    ya = _dot(oa, wa_ref[0])
    yb = _dot(ob, wb_ref[0])
    g = gate_ref[0]
    mixed = (g[:, :D_MODEL].astype(F32) * ya + g[:, D_MODEL:].astype(F32) * yb).astype(BF16)
    mod = mod_ref[0]
    x1 = jnp.where(is_ctx, xc_ref[0], xl_ref[0]) + mod[2:3] * _dot(mixed, wo_ref[0])
    x1_ref[0] = x1

    ms = jnp.mean(x1 * x1, axis=-1, keepdims=True)
    h2 = x1 * lax.rsqrt(ms + EPS) * g2_ref[0] * (1.0 + mod[4:5]) + mod[3:4]
    h2_hi = h2.astype(BF16)
    h2_ref[0] = h2_hi

    h2_lo = (h2 - h2_hi.astype(F32)).astype(BF16)
    wr = wrt_ref[...]
    wr_hi = wr.astype(BF16)
    wr_lo = (wr - wr_hi.astype(F32)).astype(BF16)
    by_hi = _dot_nt(jnp.concatenate([wr_hi, wr_lo], axis=0), h2_hi)
    logits = by_hi[:N_EXPERTS] + (by_hi[N_EXPERTS:] + _dot_nt(wr_hi, h2_lo))
    scores = jax.nn.sigmoid(logits)
    sel = scores + br_ref[...]
    score_rows = [scores[e:e + 1] for e in range(N_EXPERTS)]
    sel_rows = [sel[e:e + 1] for e in range(N_EXPERTS)]

    neg = -jnp.inf
    best = None
    for grp in range(N_GROUPS):
        vals = sel_rows[grp * EXPERTS_PER_GROUP:(grp + 1) * EXPERTS_PER_GROUP]
        m1, i1 = _first_max_of_four(vals)
        rest = [jnp.where(i1 == float(j), neg, v) for j, v in enumerate(vals)]
        m2, i2 = _first_max_of_four(rest)
        cand = (m1 + m2, i1 + float(grp * EXPERTS_PER_GROUP), i2 + float(grp * EXPERTS_PER_GROUP))
        if best is None:
            best = cand
        else:
            upd = cand[0] > best[0]
            best = tuple(jnp.where(upd, n, o) for n, o in zip(cand, best))
    _, e1, e2 = best
    hits1 = [e1 == float(e) for e in range(N_EXPERTS)]
    hits2 = [e2 == float(e) for e in range(N_EXPERTS)]
    s1 = functools.reduce(lambda a, b: a + b, [jnp.where(h, r, 0.0) for h, r in zip(hits1, score_rows)])
    s2 = functools.reduce(lambda a, b: a + b, [jnp.where(h, r, 0.0) for h, r in zip(hits2, score_rows)])
    den = s1 + s2
    w1 = s1 / den
    w2 = s2 / den
    rows = [jnp.where(h1, w1, 0.0) + jnp.where(h2_, w2, 0.0) for h1, h2_ in zip(hits1, hits2)]
    rows.append(jnp.zeros((LANES - N_EXPERTS, rows[0].shape[1]), F32))
    comb_ref[0] = jnp.concatenate(rows, axis=0).T


def _output_stage(l, x_ctx, x_lat, mods, o_ctx, o_da, o_na, gates, wa_b, wb_b, wo_b, g_norm2, w_router_t,
                  b_router_c):
    tm = 256
    nt = SEG_ROWS // tm
    return pl.pallas_call(
        _out_kernel,
        out_shape=(
            jax.ShapeDtypeStruct((N_SEG, SEG_ROWS, D_MODEL), F32),
            jax.ShapeDtypeStruct((N_SEG, SEG_ROWS, D_MODEL), BF16),
            jax.ShapeDtypeStruct((N_SEG, SEG_ROWS, LANES), F32),
        ),
        grid=(N_SEG, nt),
        in_specs=[
            _ctx_rows_spec(tm, D_MODEL),
            _lat_rows_spec(tm, D_MODEL),
            pl.BlockSpec((1, N_MOD, D_MODEL), lambda s, i: (s, 0, 0)),
            _ctx_rows_spec(tm, D_MODEL),
            _lat_rows_spec(tm, DA_WIDTH),
            _lat_rows_spec(tm, NA_WIDTH),
            pl.BlockSpec((1, tm, 2 * D_MODEL), lambda s, i: (s, i, 0)),
            pl.BlockSpec((1, DA_WIDTH, D_MODEL), lambda s, i: (l, 0, 0)),
            pl.BlockSpec((1, NA_WIDTH, D_MODEL), lambda s, i: (l, 0, 0)),
            pl.BlockSpec((1, D_MODEL, D_MODEL), lambda s, i: (l, 0, 0)),
            pl.BlockSpec((1, 1, D_MODEL), lambda s, i: (l, 0, 0)),
            pl.BlockSpec((N_EXPERTS, D_MODEL), lambda s, i: (0, 0)),
            pl.BlockSpec((N_EXPERTS, 1), lambda s, i: (0, 0)),
        ],
        out_specs=(
            pl.BlockSpec((1, tm, D_MODEL), lambda s, i: (s, i, 0)),
            pl.BlockSpec((1, tm, D_MODEL), lambda s, i: (s, i, 0)),
            pl.BlockSpec((1, tm, LANES), lambda s, i: (s, i, 0)),
        ),
        compiler_params=_params("arbitrary", "arbitrary"),
        name="output_router",
    )(x_ctx, x_lat, mods, o_ctx, o_da, o_na, gates, wa_b, wb_b, wo_b, g_norm2, w_router_t, b_router_c)


MOE_CHUNK = EXPERTS_PER_GROUP * D_EXPERT


def _moe_kernel(h_ref, comb_ref, x1_ref, mod_ref, wg_ref, wu_ref, wd_ref, oc_ref, ol_ref):
    is_ctx = pl.program_id(0) == 0
    h = h_ref[0]
    comb = comb_ref[0]
    acc = jnp.zeros(x1_ref.shape[1:], F32)
    for c in range(N_GROUPS):
        parts = []
        for j in range(EXPERTS_PER_GROUP):
            e = c * EXPERTS_PER_GROUP + j
            g = _dot(h, wg_ref[0, e])
            u = _dot(h, wu_ref[0, e])
            parts.append(g * jax.nn.sigmoid(g) * u * comb[:, e:e + 1])
        hs = jnp.concatenate(parts, axis=1).astype(BF16)
        acc = acc + _dot(hs, wd_ref[0, c * MOE_CHUNK:(c + 1) * MOE_CHUNK, :])
    out = x1_ref[0] + mod_ref[0][5:6] * acc

    @pl.when(is_ctx)
    def _():
        oc_ref[0] = out

    @pl.when(jnp.logical_not(is_ctx))
    def _():
        ol_ref[0] = out


def _experts(l, h2, comb, x1, mods, wg_all, wu_all, wd_all):
    tm = 256
    hidden = N_EXPERTS * D_EXPERT
    once = pl.Buffered(1)
    return pl.pallas_call(
        _moe_kernel,
        out_shape=(
            jax.ShapeDtypeStruct((1, SEG_ROWS, D_MODEL), F32),
            jax.ShapeDtypeStruct((DEC_BATCH, DEC_SEQ, D_MODEL), F32),
        ),
        grid=(N_SEG, SEG_ROWS // tm),
        in_specs=[
            pl.BlockSpec((1, tm, D_MODEL), lambda s, i: (s, i, 0)),
            pl.BlockSpec((1, tm, LANES), lambda s, i: (s, i, 0)),
            pl.BlockSpec((1, tm, D_MODEL), lambda s, i: (s, i, 0)),
            pl.BlockSpec((1, N_MOD, D_MODEL), lambda s, i: (s, 0, 0)),
            pl.BlockSpec((1, N_EXPERTS, D_MODEL, D_EXPERT), lambda s, i: (l, 0, 0, 0), pipeline_mode=once),
            pl.BlockSpec((1, N_EXPERTS, D_MODEL, D_EXPERT), lambda s, i: (l, 0, 0, 0), pipeline_mode=once),
            pl.BlockSpec((1, hidden, D_MODEL), lambda s, i: (l, 0, 0), pipeline_mode=once),
        ],
        out_specs=(_ctx_rows_spec(tm, D_MODEL), _lat_rows_spec(tm, D_MODEL)),
        compiler_params=_params("arbitrary", "arbitrary"),
        name="experts",
    )(h2, comb, x1, mods, wg_all, wu_all, wd_all)


def _rope_tables():
    t = np.arange(DEC_SEQ)
    pos = np.stack([t // GRID_W, t % GRID_W], axis=-1).astype(np.float32)
    inv_freq = (ROPE_BASE ** (-np.arange(ROPE_PAIRS, dtype=np.float32) / ROPE_PAIRS)).astype(np.float32)
    ang = pos[:, :, None] * inv_freq
    lane = np.arange(LANES)
    axis = (lane % HEAD_DIM) // (2 * ROPE_PAIRS)
    pair = lane % ROPE_PAIRS
    second = ((lane // ROPE_PAIRS) % 2).astype(bool)
    a = ang[:, axis, pair]
    cos = np.cos(a)
    sin = np.sin(a)
    lat = np.concatenate([cos, np.where(second, 0.0, -sin), np.where(second, sin, 0.0)], axis=1)
    ident = np.concatenate([np.ones_like(cos), np.zeros_like(cos), np.zeros_like(cos)], axis=1)
    return jnp.asarray(np.stack([ident, lat]).astype(np.float32))


def _head_block_diag():
    r = np.arange(CHUNK // 2) // HEAD_DIM
    return jnp.asarray((r[:, None] == r[None, :]).astype(np.float32), dtype=BF16)


def kernel(x_prompt, x_sample, cache_da_k, cache_da_v, cache_na_k, cache_na_v, c, c_ctx, w_mod, b_mod, g_norm1,
           g_norm2, w_in, g_q_da, g_k_da, g_q_na, g_k_na, lam_q1, lam_k1, lam_q2, lam_k2, g_subln, rpb, w_br_a,
           w_br_b, w_out, w_router, b_router, w_gate, w_up, w_down):
    L = DEPTH
    x_ctx = x_prompt.reshape(1, SEG_ROWS, D_MODEL)
    x_lat = x_sample
    cond8 = jnp.concatenate([c_ctx[None], c, jnp.zeros((8 - N_SEG, D_MODEL), F32)], axis=0)
    w_in_b = w_in.astype(BF16)
    wa_b = w_br_a.astype(BF16)
    wb_b = w_br_b.astype(BF16)
    wo_b = w_out.astype(BF16)
    hidden = N_EXPERTS * D_EXPERT
    wg_all = w_gate.astype(BF16)
    wu_all = w_up.astype(BF16)
    wd_all = w_down.astype(BF16).reshape(L, hidden, D_MODEL)
    w_router_t = w_router.T
    b_router_c = b_router.reshape(N_EXPERTS, 1)
    head_gains = jnp.stack([jnp.tile(g, (1, CHUNK // HEAD_DIM)) for g in (g_q_da, g_k_da, g_q_na, g_k_na)],
                           axis=1).reshape(L, 4, CHUNK)
    g_norm1 = g_norm1.reshape(L, 1, D_MODEL)
    g_norm2 = g_norm2.reshape(L, 1, D_MODEL)
    g_subln = g_subln.reshape(L, 1, LANES)
    pad64 = lambda t: jnp.pad(t, ((0, 0), (0, LANES - HEAD_DIM)))
    lam_inits = jnp.asarray([0.8 - 0.6 * math.exp(-0.3 * l) for l in range(L)], F32)
    lam_tab = jnp.stack([pad64(lam_q1), pad64(lam_k1), pad64(lam_q2), pad64(lam_k2),
                         jnp.broadcast_to(lam_inits[:, None], (L, LANES)),
                         jnp.zeros((L, LANES), F32), jnp.zeros((L, LANES), F32), jnp.zeros((L, LANES), F32)],
                        axis=1)
    na_bias = _na_bias_tables(rpb)
    rope_tab = _rope_tables()
    bd = _head_block_diag()

    mods_all = _modulation(cond8, w_mod, b_mod)[:, :N_SEG].reshape(L, N_SEG, N_MOD, D_MODEL)

    caches = None
    for l in range(L):
        mods = mods_all[l]
        qkv, gates, *caches = _in_projection(l, x_ctx, x_lat, mods, g_norm1, w_in_b, head_gains, bd, rope_tab,
                                             caches)
        o_ctx = _attention_ctx(l, qkv, lam_tab, g_subln).reshape(1, SEG_ROWS, D_MODEL)
        o_da = _attention_da(l, qkv, cache_da_k, cache_da_v, lam_tab, g_subln)
        o_na = _attention_na(l, qkv, cache_na_k, cache_na_v, na_bias)
        x1, h2, comb = _output_stage(l, x_ctx, x_lat, mods, o_ctx, o_da, o_na, gates, wa_b, wb_b, wo_b, g_norm2,
                                     w_router_t, b_router_c)
        x_ctx, x_lat = _experts(l, h2, comb, x1, mods, wg_all, wu_all, wd_all)

    y_prompt = x_ctx.reshape(BATCH, SEQ, D_MODEL)
    y_sample = x_lat
    new_da_k, new_da_v, new_na_k, new_na_v = caches
    return (y_prompt, y_sample, new_da_k, new_da_v, new_na_k, new_na_v)
```

```python
import functools
import math

import numpy as np
import jax
import jax.numpy as jnp
from jax import lax
from jax.experimental import pallas as pl
from jax.experimental.pallas import tpu as pltpu

D_MODEL = 1024
BATCH = 16
SEQ = 256
DEPTH = 4
DEC_BATCH = 4
DEC_SEQ = 4096
PAST_LEN = 512
GRID_W = 64
GRID_H = DEC_SEQ // GRID_W
HEAD_DIM = 64
DA_HEADS = 4
DA_WIDTH = 512
NA_HEADS = 8
NA_WIDTH = 512
NA_ROWS = 8
NA_COLS = 16
IN_COLS = 3 * DA_WIDTH + 3 * NA_WIDTH + 2 * D_MODEL
ROPE_BASE = 10000.0
ROPE_PAIRS = HEAD_DIM // 4
N_EXPERTS = 16
N_GROUPS = 4
EXPERTS_PER_GROUP = 4
D_EXPERT = 256
N_MOD = 6
EPS = 1e-6
ATTN_SCALE = HEAD_DIM ** -0.5
Q_PRESCALE = ATTN_SCALE * math.log2(math.e)
LOG2E = math.log2(math.e)

N_SEG = 1 + DEC_BATCH
SEG_ROWS = DEC_SEQ
LANES = 128
CHUNK = 512

VMEM_LIMIT = 56 * 1024 * 1024

F32 = jnp.float32
BF16 = jnp.bfloat16


def _dot(a, b):
    return jnp.dot(a, b, preferred_element_type=F32)


def _dot_nt(a, b):
    return lax.dot_general(a, b, (((1,), (1,)), ((), ())), preferred_element_type=F32)


def _params(*sem):
    return pltpu.CompilerParams(dimension_semantics=sem, vmem_limit_bytes=VMEM_LIMIT)


def _ctx_rows_spec(tm, width):
    nt = SEG_ROWS // tm
    return pl.BlockSpec((1, tm, width), lambda s, i: (0, jnp.where(s == 0, i, nt - 1), 0))


def _lat_rows_spec(tm, width):
    return pl.BlockSpec((1, tm, width), lambda s, i: (jnp.maximum(s - 1, 0), jnp.where(s == 0, 0, i), 0))


def _mod_kernel(cond_ref, w_ref, b_ref, o_ref):
    c = cond_ref[...]
    sc = (c * jax.nn.sigmoid(c)).astype(BF16)
    o_ref[0] = _dot(sc, w_ref[0].astype(BF16)) + b_ref[0]


def _modulation(cond8, w_mod, b_mod):
    tn = 1536
    return pl.pallas_call(
        _mod_kernel,
        out_shape=jax.ShapeDtypeStruct((DEPTH, 8, N_MOD * D_MODEL), F32),
        grid=(DEPTH, N_MOD * D_MODEL // tn),
        in_specs=[
            pl.BlockSpec((8, D_MODEL), lambda l, j: (0, 0)),
            pl.BlockSpec((1, D_MODEL, tn), lambda l, j: (l, 0, j)),
            pl.BlockSpec((1, 1, tn), lambda l, j: (l, 0, j)),
        ],
        out_specs=pl.BlockSpec((1, 8, tn), lambda l, j: (l, 0, j)),
        compiler_params=_params("arbitrary", "arbitrary"),
        name="modulation",
    )(cond8, w_mod, b_mod.reshape(DEPTH, 1, N_MOD * D_MODEL))


def _store_heads(dst_ref, b, lead, t, rows, width):
    for h in range(CHUNK // width):
        dst_ref[(b, 0) + lead + (h,)] = t[rows, h * width:(h + 1) * width]


def _inproj_kernel(xc_ref, xl_ref, mod_ref, g1_ref, w_ref, hg_ref, bd_ref, rope_ref, *rest):
    qkv_ref, gate_ref, cdk_ref, cdv_ref, cnk_ref, cnv_ref = rest[-6:]
    s = pl.program_id(0)
    x = jnp.where(s == 0, xc_ref[0], xl_ref[0])
    ms = jnp.mean(x * x, axis=-1, keepdims=True)
    y = x * lax.rsqrt(ms + EPS) * g1_ref[0]
    mod = mod_ref[0]
    h = (y * (1.0 + mod[1:2]) + mod[0:1]).astype(BF16)

    rope = rope_ref[0]
    cos4 = jnp.concatenate([rope[:, 0:LANES]] * 4, axis=1)
    sin_up4 = jnp.concatenate([rope[:, LANES:2 * LANES]] * 4, axis=1)
    sin_dn4 = jnp.concatenate([rope[:, 2 * LANES:3 * LANES]] * 4, axis=1)

    def head_norm(acc, row):
        sq = (acc * acc).astype(BF16)
        half = CHUNK // 2
        ssum = jnp.concatenate([_dot(sq[:, :half], bd_ref[...]), _dot(sq[:, half:], bd_ref[...])], axis=1)
        return acc * lax.rsqrt(ssum * (1.0 / HEAD_DIM) + EPS) * hg_ref[0, row:row + 1, :]

    def rope_rot(t):
        return (t * cos4 + pltpu.roll(t, CHUNK - ROPE_PAIRS, 1) * sin_up4
                + pltpu.roll(t, ROPE_PAIRS, 1) * sin_dn4)

    def proj(c):
        return _dot(h, w_ref[0, :, c * CHUNK:(c + 1) * CHUNK])

    is_ctx = s == 0
    batches = [(b, slice(b * SEQ, (b + 1) * SEQ)) for b in range(x.shape[0] // SEQ)]

    q = rope_rot(head_norm(proj(0), 0)) * Q_PRESCALE
    qkv_ref[0, :, 0:CHUNK] = q.astype(BF16)

    k = rope_rot(head_norm(proj(1), 1))
    qkv_ref[0, :, CHUNK:2 * CHUNK] = k.astype(BF16)

    @pl.when(is_ctx)
    def _():
        for b, rows in batches:
            for m in range(2):
                half = k[:, m * (CHUNK // 2):(m + 1) * (CHUNK // 2)]
                for h in range(DA_HEADS):
                    cdk_ref[b, 0, m, h] = half[rows, h * HEAD_DIM:(h + 1) * HEAD_DIM]

    v = proj(2)
    qkv_ref[0, :, 2 * CHUNK:3 * CHUNK] = v.astype(BF16)

    @pl.when(is_ctx)
    def _():
        for b, rows in batches:
            _store_heads(cdv_ref, b, (), v, rows, 2 * HEAD_DIM)

    q = head_norm(proj(3), 2) * Q_PRESCALE
    qkv_ref[0, :, 3 * CHUNK:4 * CHUNK] = q.astype(BF16)

    k = head_norm(proj(4), 3)
    qkv_ref[0, :, 4 * CHUNK:5 * CHUNK] = k.astype(BF16)

    @pl.when(is_ctx)
    def _():
        for b, rows in batches:
            _store_heads(cnk_ref, b, (), k, rows, HEAD_DIM)

    v = proj(5)
    qkv_ref[0, :, 5 * CHUNK:6 * CHUNK] = v.astype(BF16)

    @pl.when(is_ctx)
    def _():
        for b, rows in batches:
            _store_heads(cnv_ref, b, (), v, rows, HEAD_DIM)

    for c in range(4):
        g = proj(6 + c)
        gate_ref[0, :, c * CHUNK:(c + 1) * CHUNK] = jax.nn.sigmoid(g).astype(BF16)


CACHE_SHAPES = (
    (BATCH, DEPTH, 2, DA_HEADS, SEQ, HEAD_DIM),
    (BATCH, DEPTH, DA_HEADS, SEQ, 2 * HEAD_DIM),
    (BATCH, DEPTH, NA_HEADS, SEQ, HEAD_DIM),
    (BATCH, DEPTH, NA_HEADS, SEQ, HEAD_DIM),
)


def _in_projection(l, x_ctx, x_lat, mods, g_norm1, w_in_b, head_gains, bd, rope_tab, caches):
    tm = 512
    nt = SEG_ROWS // tm
    nb = tm // SEQ

    def cache_spec(shape):
        block = (nb, 1) + shape[2:]
        zeros = (0,) * (len(shape) - 2)
        return pl.BlockSpec(block, lambda s, i: (jnp.where(s == 0, i, nt - 1), l) + zeros)

    chained = caches is not None
    cache_in = list(caches) if chained else []
    n_in = 8
    return pl.pallas_call(
        _inproj_kernel,
        out_shape=(
            jax.ShapeDtypeStruct((N_SEG, SEG_ROWS, 6 * CHUNK), BF16),
            jax.ShapeDtypeStruct((N_SEG, SEG_ROWS, 2 * D_MODEL), BF16),
        ) + tuple(jax.ShapeDtypeStruct(shape, F32) for shape in CACHE_SHAPES),
        grid=(N_SEG, nt),
        in_specs=[
            _ctx_rows_spec(tm, D_MODEL),
            _lat_rows_spec(tm, D_MODEL),
            pl.BlockSpec((1, N_MOD, D_MODEL), lambda s, i: (s, 0, 0)),
            pl.BlockSpec((1, 1, D_MODEL), lambda s, i: (l, 0, 0)),
            pl.BlockSpec((1, D_MODEL, IN_COLS), lambda s, i: (l, 0, 0), pipeline_mode=pl.Buffered(1)),
            pl.BlockSpec((1, 4, CHUNK), lambda s, i: (l, 0, 0)),
            pl.BlockSpec((CHUNK // 2, CHUNK // 2), lambda s, i: (0, 0)),
            pl.BlockSpec((1, tm, 3 * LANES), lambda s, i: (jnp.minimum(s, 1), i, 0)),
        ] + [pl.BlockSpec(memory_space=pl.ANY) for _ in cache_in],
        out_specs=(
            pl.BlockSpec((1, tm, 6 * CHUNK), lambda s, i: (s, i, 0)),
            pl.BlockSpec((1, tm, 2 * D_MODEL), lambda s, i: (s, i, 0)),
        ) + tuple(cache_spec(shape) for shape in CACHE_SHAPES),
        input_output_aliases={n_in + j: 2 + j for j in range(len(cache_in))},
        compiler_params=_params("arbitrary", "arbitrary"),
        name="in_projection",
    )(x_ctx, x_lat, mods, g_norm1, w_in_b, head_gains, bd, rope_tab, *cache_in)


def _lam_value(lam_ref):
    lp = lam_ref[0]
    s1 = jnp.sum(lp[0:1] * lp[1:2], axis=-1, keepdims=True)
    s2 = jnp.sum(lp[2:3] * lp[3:4], axis=-1, keepdims=True)
    lam_init = lp[4:5, 0:1]
    return jnp.exp(s1) - jnp.exp(s2) + lam_init, 1.0 - lam_init


def _half_masks():
    lane = lax.broadcasted_iota(jnp.int32, (1, LANES), 1)
    return (lane < HEAD_DIM, lane >= HEAD_DIM)


def _pick_head(mask, t):
    return jnp.where(mask, t, jnp.zeros_like(t))


def _sub_layer_norm(o, gsub, one_minus):
    ms = jnp.mean(o * o, axis=-1, keepdims=True)
    return o * lax.rsqrt(ms + EPS) * gsub * one_minus


def _attn_ctx_kernel(qkv_ref, lam_ref, gsub_ref, o_ref):
    lam, one_minus = _lam_value(lam_ref)
    masks = _half_masks()
    qda = qkv_ref[0, :, 0:CHUNK]
    qna = qkv_ref[0, :, 3 * CHUNK:4 * CHUNK]
    gsub = gsub_ref[0]

    def kv(c0):
        off = c0 + CHUNK if c0 < 2 * CHUNK else c0 + 2 * CHUNK
        return qkv_ref[0, :, off:off + LANES]

    def softmax_parts(sc):
        m = jnp.max(sc, axis=-1, keepdims=True)
        e = jnp.exp2(sc - m)
        return e, 1.0 / jnp.sum(e, axis=-1, keepdims=True)

    for h in range(DA_HEADS):
        p, sub = divmod(h, 2)
        q1 = _pick_head(masks[sub], qda[:, p * LANES:(p + 1) * LANES])
        q2 = _pick_head(masks[sub], qda[:, 2 * LANES + p * LANES:2 * LANES + (p + 1) * LANES])
        e1, r1 = softmax_parts(_dot_nt(q1, kv(p * LANES)))
        e2, r2 = softmax_parts(_dot_nt(q2, kv(2 * LANES + p * LANES)))
        a = e1 * r1 - e2 * (lam * r2)
        o = _dot(a.astype(BF16), kv(CHUNK + h * LANES))
        o_ref[:, h * LANES:(h + 1) * LANES] = _sub_layer_norm(o, gsub, one_minus).astype(BF16)

    for p in range(NA_HEADS // 2):
        kp = kv(2 * CHUNK + p * LANES)
        vp = kv(3 * CHUNK + p * LANES)
        both = None
        for sub in range(2):
            qm = _pick_head(masks[sub], qna[:, p * LANES:(p + 1) * LANES])
            e, r = softmax_parts(_dot_nt(qm, kp))
            o = _dot(e.astype(BF16), vp) * r
            both = o if sub == 0 else jnp.where(masks[1], o, both)
        o_ref[:, DA_WIDTH + p * LANES:DA_WIDTH + (p + 1) * LANES] = both.astype(BF16)


def _attention_ctx(l, qkv, lam_tab, g_subln):
    return pl.pallas_call(
        _attn_ctx_kernel,
        out_shape=jax.ShapeDtypeStruct((SEG_ROWS, D_MODEL), BF16),
        grid=(BATCH,),
        in_specs=[
            pl.BlockSpec((1, SEQ, 6 * CHUNK), lambda b: (0, b, 0)),
            pl.BlockSpec((1, 8, LANES), lambda b: (l, 0, 0)),
            pl.BlockSpec((1, 1, LANES), lambda b: (l, 0, 0)),
        ],
        out_specs=pl.BlockSpec((SEQ, D_MODEL), lambda b: (b, 0)),
        compiler_params=_params("arbitrary"),
        name="attention_ctx",
    )(qkv, lam_tab, g_subln)


DA_KEYS = DEC_SEQ + PAST_LEN
DA_KCHUNK = 1024
DA_TQ = 256


def _pair_heads(head_ref, lead, first):
    a = head_ref[lead + (first,)]
    b = head_ref[lead + (first + 1,)]
    return jnp.concatenate([a, b], axis=1).astype(BF16)


def _attn_da_kernel(q_ref, k_ref, v_ref, kc_ref, vc_ref, lam_ref, gsub_ref, o_ref, vt_ref, kcs_ref, e1_ref, e2_ref):
    chunks = [slice(r, r + DA_KCHUNK) for r in range(0, DEC_SEQ, DA_KCHUNK)] + [slice(DEC_SEQ, DA_KEYS)]
    n_chunks = len(chunks)

    @pl.when(pl.program_id(1) == 0)
    def _():
        for r in range(0, DEC_SEQ, PAST_LEN):
            rows = slice(r, r + PAST_LEN)
            vt_ref[:, rows] = v_ref[0, rows, :].astype(F32).T.astype(BF16)
        for h in range(DA_HEADS):
            vt_ref[h * LANES:(h + 1) * LANES, DEC_SEQ:DA_KEYS] = vc_ref[0, 0, h].T.astype(BF16)
        for m in range(2):
            for p in range(DA_HEADS // 2):
                c0 = m * 2 * LANES + p * LANES
                kcs_ref[:, c0:c0 + LANES] = _pair_heads(kc_ref, (0, 0, m), 2 * p)

    lam, one_minus = _lam_value(lam_ref)
    masks = _half_masks()
    q = q_ref[0]
    gsub = gsub_ref[0]

    def key_chunk(c, cols):
        if chunks[c].start < DEC_SEQ:
            return k_ref[0, chunks[c], cols]
        return kcs_ref[:, cols]

    def score_map(qm, cols, e_ref, weight):
        maxes, sums = [], []
        for c in range(n_chunks):
            s = _dot_nt(key_chunk(c, cols), qm)
            mc = jnp.max(s, axis=0, keepdims=True)
            e = jnp.exp2(s - mc)
            sums.append(jnp.sum(e, axis=0, keepdims=True))
            maxes.append(mc)
            e_ref[chunks[c], :] = e.astype(BF16)
        m = functools.reduce(jnp.maximum, maxes)
        facs = [jnp.exp2(mc - m) for mc in maxes]
        tot = functools.reduce(lambda a, b: a + b, [lc * fc for lc, fc in zip(sums, facs)])
        w = weight / tot
        return [(fc * w).astype(BF16) for fc in facs]

    for h in range(DA_HEADS):
        p, sub = divmod(h, 2)
        c1 = slice(p * LANES, (p + 1) * LANES)
        c2 = slice(2 * LANES + p * LANES, 2 * LANES + (p + 1) * LANES)
        f1 = score_map(_pick_head(masks[sub], q[:, c1]), c1, e1_ref, 1.0)
        f2 = score_map(_pick_head(masks[sub], q[:, c2]), c2, e2_ref, lam)
        o_t = jnp.zeros((LANES, DA_TQ), F32)
        for c, rows in enumerate(chunks):
            a = e1_ref[rows, :] * f1[c] - e2_ref[rows, :] * f2[c]
            o_t = o_t + _dot(vt_ref[h * LANES:(h + 1) * LANES, rows], a)
        o_ref[0, :, h * LANES:(h + 1) * LANES] = _sub_layer_norm(o_t.T, gsub, one_minus).astype(BF16)


def _attention_da(l, qkv, ctx_k, ctx_v, lam_tab, g_subln):
    tq = DA_TQ
    return pl.pallas_call(
        _attn_da_kernel,
        out_shape=jax.ShapeDtypeStruct((DEC_BATCH, DEC_SEQ, DA_WIDTH), BF16),
        grid=(DEC_BATCH, DEC_SEQ // tq),
        in_specs=[
            pl.BlockSpec((1, tq, CHUNK), lambda b, i: (b + 1, i, 0)),
            pl.BlockSpec((1, DEC_SEQ, CHUNK), lambda b, i: (b + 1, 0, 1), pipeline_mode=pl.Buffered(1)),
            pl.BlockSpec((1, DEC_SEQ, CHUNK), lambda b, i: (b + 1, 0, 2), pipeline_mode=pl.Buffered(1)),
            pl.BlockSpec((1, 1, 2, DA_HEADS, PAST_LEN, HEAD_DIM), lambda b, i: (b, l, 0, 0, 0, 0)),
            pl.BlockSpec((1, 1, DA_HEADS, PAST_LEN, 2 * HEAD_DIM), lambda b, i: (b, l, 0, 0, 0)),
            pl.BlockSpec((1, 8, LANES), lambda b, i: (l, 0, 0)),
            pl.BlockSpec((1, 1, LANES), lambda b, i: (l, 0, 0)),
        ],
        out_specs=pl.BlockSpec((1, tq, DA_WIDTH), lambda b, i: (b, i, 0)),
        scratch_shapes=[
            pltpu.VMEM((DA_WIDTH, DA_KEYS), BF16),
            pltpu.VMEM((PAST_LEN, DA_WIDTH), BF16),
            pltpu.VMEM((DA_KEYS, tq), BF16),
            pltpu.VMEM((DA_KEYS, tq), BF16),
        ],
        compiler_params=_params("arbitrary", "arbitrary"),
        name="attention_da",
    )(qkv, qkv, qkv, ctx_k, ctx_v, lam_tab, g_subln)


NA_QROWS = 8
NA_KROWS = 16


def _attn_na_kernel(q_ref, k_ref, v_ref, kc_ref, vc_ref, bias_ref, o_ref, kcs_ref, vcs_ref):
    i = pl.program_id(1)

    @pl.when(i == 0)
    def _():
        for p in range(NA_HEADS // 2):
            kcs_ref[:, p * LANES:(p + 1) * LANES] = _pair_heads(kc_ref, (0, 0), 2 * p)
            vcs_ref[:, p * LANES:(p + 1) * LANES] = _pair_heads(vc_ref, (0, 0), 2 * p)

    masks = _half_masks()
    q = q_ref[0]
    row0 = jnp.clip(i * NA_QROWS - NA_ROWS // 2, 0, GRID_H - NA_KROWS)
    start = pl.multiple_of(row0 * GRID_W, 256)
    nk = NA_KROWS * GRID_W

    for p in range(NA_HEADS // 2):
        c0 = p * LANES
        kw = k_ref[0, pl.ds(start, nk), c0:c0 + LANES]
        vw = v_ref[0, pl.ds(start, nk), c0:c0 + LANES]
        kc = kcs_ref[:, c0:c0 + LANES]
        vc = vcs_ref[:, c0:c0 + LANES]
        both = None
        for sub in range(2):
            qm = _pick_head(masks[sub], q[:, c0:c0 + LANES])
            s_loc = _dot_nt(qm, kw) + bias_ref[0, 0, 2 * p + sub].astype(F32)
            s_ctx = _dot_nt(qm, kc)
            m = jnp.maximum(jnp.max(s_loc, axis=-1, keepdims=True), jnp.max(s_ctx, axis=-1, keepdims=True))
            e_loc = jnp.exp2(s_loc - m)
            e_ctx = jnp.exp2(s_ctx - m)
            tot = jnp.sum(e_loc, axis=-1, keepdims=True) + jnp.sum(e_ctx, axis=-1, keepdims=True)
            o = (_dot(e_loc.astype(BF16), vw) + _dot(e_ctx.astype(BF16), vc)) * (1.0 / tot)
            both = o if sub == 0 else jnp.where(masks[1], o, both)
        o_ref[0, :, c0:c0 + LANES] = both.astype(BF16)


def _attention_na(l, qkv, ctx_k, ctx_v, bias):
    tq = NA_QROWS * GRID_W
    n_i = DEC_SEQ // tq

    def bias_idx(b, i):
        return (l, jnp.where(i == 0, 0, jnp.where(i == n_i - 1, 2, 1)), 0, 0, 0)

    return pl.pallas_call(
        _attn_na_kernel,
        out_shape=jax.ShapeDtypeStruct((DEC_BATCH, DEC_SEQ, NA_WIDTH), BF16),
        grid=(DEC_BATCH, n_i),
        in_specs=[
            pl.BlockSpec((1, tq, CHUNK), lambda b, i: (b + 1, i, 3)),
            pl.BlockSpec((1, DEC_SEQ, CHUNK), lambda b, i: (b + 1, 0, 4), pipeline_mode=pl.Buffered(1)),
            pl.BlockSpec((1, DEC_SEQ, CHUNK), lambda b, i: (b + 1, 0, 5), pipeline_mode=pl.Buffered(1)),
            pl.BlockSpec((1, 1, NA_HEADS, PAST_LEN, HEAD_DIM), lambda b, i: (b, l, 0, 0, 0)),
            pl.BlockSpec((1, 1, NA_HEADS, PAST_LEN, HEAD_DIM), lambda b, i: (b, l, 0, 0, 0)),
            pl.BlockSpec((1, 1, NA_HEADS, tq, NA_KROWS * GRID_W), bias_idx),
        ],
        out_specs=pl.BlockSpec((1, tq, NA_WIDTH), lambda b, i: (b, i, 0)),
        scratch_shapes=[
            pltpu.VMEM((PAST_LEN, NA_WIDTH), BF16),
            pltpu.VMEM((PAST_LEN, NA_WIDTH), BF16),
        ],
        compiler_params=_params("arbitrary", "arbitrary"),
        name="attention_na",
    )(qkv, qkv, qkv, ctx_k, ctx_v, bias)


def _na_bias_tables(rpb):
    qr = np.arange(NA_QROWS)
    qc = np.arange(GRID_W)
    kr = np.arange(NA_KROWS)
    kc = np.arange(GRID_W)
    c0 = np.clip(qc - NA_COLS // 2, 0, GRID_W - NA_COLS)
    col_ok = (kc[None, :] >= c0[:, None]) & (kc[None, :] < c0[:, None] + NA_COLS)
    cpad = GRID_W - NA_COLS
    rpb_c = jnp.pad(rpb, ((0, 0), (0, 0), (0, 0), (cpad, cpad)))
    by_col = jnp.stack([rpb_c[..., GRID_W - 1 - c:2 * GRID_W - 1 - c] for c in range(GRID_W)], axis=-2)
    by_col = jnp.where(col_ok, by_col * LOG2E, -jnp.inf)

    plans = []
    for i in (0, 1, GRID_H // NA_QROWS - 1):
        rows_q = NA_QROWS * i + qr
        row0_k = min(max(NA_QROWS * i - NA_ROWS // 2, 0), GRID_H - NA_KROWS)
        rows_k = row0_k + kr
        r0 = np.clip(rows_q - NA_ROWS // 2, 0, GRID_H - NA_ROWS)
        row_ok = (rows_k[None, :] >= r0[:, None]) & (rows_k[None, :] < r0[:, None] + NA_ROWS)
        slab = rows_k[None, :] - rows_q[:, None] + NA_ROWS - 1
        plans.append([[int(slab[r, c]) if row_ok[r, c] else None for c in range(NA_KROWS)]
                      for r in range(NA_QROWS)])

    def tile_kernel(tab_ref, o_ref):
        kind = pl.program_id(1)
        for which, plan in enumerate(plans):
            @pl.when(kind == which)
            def _(plan=plan):
                for r in range(NA_QROWS):
                    for c in range(NA_KROWS):
                        if plan[r][c] is None:
                            blk = jnp.full((GRID_W, GRID_W), -jnp.inf, BF16)
                        else:
                            blk = tab_ref[0, 0, plan[r][c]].astype(BF16)
                        o_ref[0, 0, 0, r * GRID_W:(r + 1) * GRID_W, c * GRID_W:(c + 1) * GRID_W] = blk

    n_slab = 2 * NA_ROWS - 1
    return pl.pallas_call(
        tile_kernel,
        out_shape=jax.ShapeDtypeStruct((DEPTH, len(plans), NA_HEADS, NA_QROWS * GRID_W, NA_KROWS * GRID_W), BF16),
        grid=(DEPTH, len(plans), NA_HEADS),
        in_specs=[pl.BlockSpec((1, 1, n_slab, GRID_W, GRID_W), lambda l, t, h: (l, h, 0, 0, 0))],
        out_specs=pl.BlockSpec((1, 1, 1, NA_QROWS * GRID_W, NA_KROWS * GRID_W), lambda l, t, h: (l, t, h, 0, 0)),
        compiler_params=_params("arbitrary", "arbitrary", "arbitrary"),
        name="na_bias_tiles",
    )(by_col)


def _first_max_of_four(vals):
    a, b, c, d = vals
    m = jnp.maximum(jnp.maximum(a, b), jnp.maximum(c, d))
    idx = jnp.where(a == m, 0.0, jnp.where(b == m, 1.0, jnp.where(c == m, 2.0, 3.0)))
    return m, idx


def _out_kernel(xc_ref, xl_ref, mod_ref, oc_ref, oda_ref, ona_ref, gate_ref, wa_ref, wb_ref, wo_ref, g2_ref,
                wrt_ref, br_ref, x1_ref, h2_ref, comb_ref):
    is_ctx = pl.program_id(0) == 0
    oc = oc_ref[0]
    oa = jnp.where(is_ctx, oc[:, :DA_WIDTH], oda_ref[0])
    ob = jnp.where(is_ctx, oc[:, DA_WIDTH:], ona_ref[0])
    ya = _dot(oa, wa_ref[0])
    yb = _dot(ob, wb_ref[0])
    g = gate_ref[0]
    mixed = (g[:, :D_MODEL].astype(F32) * ya + g[:, D_MODEL:].astype(F32) * yb).astype(BF16)
    mod = mod_ref[0]
    x1 = jnp.where(is_ctx, xc_ref[0], xl_ref[0]) + mod[2:3] * _dot(mixed, wo_ref[0])
    x1_ref[0] = x1

    ms = jnp.mean(x1 * x1, axis=-1, keepdims=True)
    h2 = x1 * lax.rsqrt(ms + EPS) * g2_ref[0] * (1.0 + mod[4:5]) + mod[3:4]
    h2_hi = h2.astype(BF16)
    h2_ref[0] = h2_hi

    h2_lo = (h2 - h2_hi.astype(F32)).astype(BF16)
    wr = wrt_ref[...]
    wr_hi = wr.astype(BF16)
    wr_lo = (wr - wr_hi.astype(F32)).astype(BF16)
    by_hi = _dot_nt(jnp.concatenate([wr_hi, wr_lo], axis=0), h2_hi)
    logits = by_hi[:N_EXPERTS] + (by_hi[N_EXPERTS:] + _dot_nt(wr_hi, h2_lo))
    scores = jax.nn.sigmoid(logits)
    sel = scores + br_ref[...]
    score_rows = [scores[e:e + 1] for e in range(N_EXPERTS)]
    sel_rows = [sel[e:e + 1] for e in range(N_EXPERTS)]

    neg = -jnp.inf
    best = None
    for grp in range(N_GROUPS):
        vals = sel_rows[grp * EXPERTS_PER_GROUP:(grp + 1) * EXPERTS_PER_GROUP]
        m1, i1 = _first_max_of_four(vals)
        rest = [jnp.where(i1 == float(j), neg, v) for j, v in enumerate(vals)]
        m2, i2 = _first_max_of_four(rest)
        cand = (m1 + m2, i1 + float(grp * EXPERTS_PER_GROUP), i2 + float(grp * EXPERTS_PER_GROUP))
        if best is None:
            best = cand
        else:
            upd = cand[0] > best[0]
            best = tuple(jnp.where(upd, n, o) for n, o in zip(cand, best))
    _, e1, e2 = best
    hits1 = [e1 == float(e) for e in range(N_EXPERTS)]
    hits2 = [e2 == float(e) for e in range(N_EXPERTS)]
    s1 = functools.reduce(lambda a, b: a + b, [jnp.where(h, r, 0.0) for h, r in zip(hits1, score_rows)])
    s2 = functools.reduce(lambda a, b: a + b, [jnp.where(h, r, 0.0) for h, r in zip(hits2, score_rows)])
    den = s1 + s2
    w1 = s1 / den
    w2 = s2 / den
    rows = [jnp.where(h1, w1, 0.0) + jnp.where(h2_, w2, 0.0) for h1, h2_ in zip(hits1, hits2)]
    rows.append(jnp.zeros((LANES - N_EXPERTS, rows[0].shape[1]), F32))
    comb_ref[0] = jnp.concatenate(rows, axis=0).T


def _output_stage(l, x_ctx, x_lat, mods, o_ctx, o_da, o_na, gates, wa_b, wb_b, wo_b, g_norm2, w_router_t,
                  b_router_c):
    tm = 256
    nt = SEG_ROWS // tm
    return pl.pallas_call(
        _out_kernel,
        out_shape=(
            jax.ShapeDtypeStruct((N_SEG, SEG_ROWS, D_MODEL), F32),
            jax.ShapeDtypeStruct((N_SEG, SEG_ROWS, D_MODEL), BF16),
            jax.ShapeDtypeStruct((N_SEG, SEG_ROWS, LANES), F32),
        ),
        grid=(N_SEG, nt),
        in_specs=[
            _ctx_rows_spec(tm, D_MODEL),
            _lat_rows_spec(tm, D_MODEL),
            pl.BlockSpec((1, N_MOD, D_MODEL), lambda s, i: (s, 0, 0)),
            _ctx_rows_spec(tm, D_MODEL),
            _lat_rows_spec(tm, DA_WIDTH),
            _lat_rows_spec(tm, NA_WIDTH),
            pl.BlockSpec((1, tm, 2 * D_MODEL), lambda s, i: (s, i, 0)),
            pl.BlockSpec((1, DA_WIDTH, D_MODEL), lambda s, i: (l, 0, 0)),
            pl.BlockSpec((1, NA_WIDTH, D_MODEL), lambda s, i: (l, 0, 0)),
            pl.BlockSpec((1, D_MODEL, D_MODEL), lambda s, i: (l, 0, 0)),
            pl.BlockSpec((1, 1, D_MODEL), lambda s, i: (l, 0, 0)),
            pl.BlockSpec((N_EXPERTS, D_MODEL), lambda s, i: (0, 0)),
            pl.BlockSpec((N_EXPERTS, 1), lambda s, i: (0, 0)),
        ],
        out_specs=(
            pl.BlockSpec((1, tm, D_MODEL), lambda s, i: (s, i, 0)),
            pl.BlockSpec((1, tm, D_MODEL), lambda s, i: (s, i, 0)),
            pl.BlockSpec((1, tm, LANES), lambda s, i: (s, i, 0)),
        ),
        compiler_params=_params("arbitrary", "arbitrary"),
        name="output_router",
    )(x_ctx, x_lat, mods, o_ctx, o_da, o_na, gates, wa_b, wb_b, wo_b, g_norm2, w_router_t, b_router_c)


MOE_CHUNK = EXPERTS_PER_GROUP * D_EXPERT


def _moe_kernel(h_ref, comb_ref, x1_ref, mod_ref, wg_ref, wu_ref, wd_ref, oc_ref, ol_ref):
    is_ctx = pl.program_id(0) == 0
    h = h_ref[0]
    comb = comb_ref[0]
    acc = jnp.zeros(x1_ref.shape[1:], F32)
    for c in range(N_GROUPS):
        parts = []
        for j in range(EXPERTS_PER_GROUP):
            e = c * EXPERTS_PER_GROUP + j
            g = _dot(h, wg_ref[0, e])
            u = _dot(h, wu_ref[0, e])
            parts.append(g * jax.nn.sigmoid(g) * u * comb[:, e:e + 1])
        hs = jnp.concatenate(parts, axis=1).astype(BF16)
        acc = acc + _dot(hs, wd_ref[0, c * MOE_CHUNK:(c + 1) * MOE_CHUNK, :])
    out = x1_ref[0] + mod_ref[0][5:6] * acc

    @pl.when(is_ctx)
    def _():
        oc_ref[0] = out

    @pl.when(jnp.logical_not(is_ctx))
    def _():
        ol_ref[0] = out


def _experts(l, h2, comb, x1, mods, wg_all, wu_all, wd_all):
    tm = 256
    hidden = N_EXPERTS * D_EXPERT
    once = pl.Buffered(1)
    return pl.pallas_call(
        _moe_kernel,
        out_shape=(
            jax.ShapeDtypeStruct((1, SEG_ROWS, D_MODEL), F32),
            jax.ShapeDtypeStruct((DEC_BATCH, DEC_SEQ, D_MODEL), F32),
        ),
        grid=(N_SEG, SEG_ROWS // tm),
        in_specs=[
            pl.BlockSpec((1, tm, D_MODEL), lambda s, i: (s, i, 0)),
            pl.BlockSpec((1, tm, LANES), lambda s, i: (s, i, 0)),
            pl.BlockSpec((1, tm, D_MODEL), lambda s, i: (s, i, 0)),
            pl.BlockSpec((1, N_MOD, D_MODEL), lambda s, i: (s, 0, 0)),
            pl.BlockSpec((1, N_EXPERTS, D_MODEL, D_EXPERT), lambda s, i: (l, 0, 0, 0), pipeline_mode=once),
            pl.BlockSpec((1, N_EXPERTS, D_MODEL, D_EXPERT), lambda s, i: (l, 0, 0, 0), pipeline_mode=once),
            pl.BlockSpec((1, hidden, D_MODEL), lambda s, i: (l, 0, 0), pipeline_mode=once),
        ],
        out_specs=(_ctx_rows_spec(tm, D_MODEL), _lat_rows_spec(tm, D_MODEL)),
        compiler_params=_params("arbitrary", "arbitrary"),
        name="experts",
    )(h2, comb, x1, mods, wg_all, wu_all, wd_all)


def _rope_tables():
    t = np.arange(DEC_SEQ)
    pos = np.stack([t // GRID_W, t % GRID_W], axis=-1).astype(np.float32)
    inv_freq = (ROPE_BASE ** (-np.arange(ROPE_PAIRS, dtype=np.float32) / ROPE_PAIRS)).astype(np.float32)
    ang = pos[:, :, None] * inv_freq
    lane = np.arange(LANES)
    axis = (lane % HEAD_DIM) // (2 * ROPE_PAIRS)
    pair = lane % ROPE_PAIRS
    second = ((lane // ROPE_PAIRS) % 2).astype(bool)
    a = ang[:, axis, pair]
    cos = np.cos(a)
    sin = np.sin(a)
    lat = np.concatenate([cos, np.where(second, 0.0, -sin), np.where(second, sin, 0.0)], axis=1)
    ident = np.concatenate([np.ones_like(cos), np.zeros_like(cos), np.zeros_like(cos)], axis=1)
    return jnp.asarray(np.stack([ident, lat]).astype(np.float32))


def _head_block_diag():
    r = np.arange(CHUNK // 2) // HEAD_DIM
    return jnp.asarray((r[:, None] == r[None, :]).astype(np.float32), dtype=BF16)


def kernel(x_prompt, x_sample, cache_da_k, cache_da_v, cache_na_k, cache_na_v, c, c_ctx, w_mod, b_mod, g_norm1,
           g_norm2, w_in, g_q_da, g_k_da, g_q_na, g_k_na, lam_q1, lam_k1, lam_q2, lam_k2, g_subln, rpb, w_br_a,
           w_br_b, w_out, w_router, b_router, w_gate, w_up, w_down):
    L = DEPTH
    x_ctx = x_prompt.reshape(1, SEG_ROWS, D_MODEL)
    x_lat = x_sample
    cond8 = jnp.concatenate([c_ctx[None], c, jnp.zeros((8 - N_SEG, D_MODEL), F32)], axis=0)
    w_in_b = w_in.astype(BF16)
    wa_b = w_br_a.astype(BF16)
    wb_b = w_br_b.astype(BF16)
    wo_b = w_out.astype(BF16)
    hidden = N_EXPERTS * D_EXPERT
    wg_all = w_gate.astype(BF16)
    wu_all = w_up.astype(BF16)
    wd_all = w_down.astype(BF16).reshape(L, hidden, D_MODEL)
    w_router_t = w_router.T
    b_router_c = b_router.reshape(N_EXPERTS, 1)
    head_gains = jnp.stack([jnp.tile(g, (1, CHUNK // HEAD_DIM)) for g in (g_q_da, g_k_da, g_q_na, g_k_na)],
                           axis=1).reshape(L, 4, CHUNK)
    g_norm1 = g_norm1.reshape(L, 1, D_MODEL)
    g_norm2 = g_norm2.reshape(L, 1, D_MODEL)
    g_subln = g_subln.reshape(L, 1, LANES)
    pad64 = lambda t: jnp.pad(t, ((0, 0), (0, LANES - HEAD_DIM)))
    lam_inits = jnp.asarray([0.8 - 0.6 * math.exp(-0.3 * l) for l in range(L)], F32)
    lam_tab = jnp.stack([pad64(lam_q1), pad64(lam_k1), pad64(lam_q2), pad64(lam_k2),
                         jnp.broadcast_to(lam_inits[:, None], (L, LANES)),
                         jnp.zeros((L, LANES), F32), jnp.zeros((L, LANES), F32), jnp.zeros((L, LANES), F32)],
                        axis=1)
    na_bias = _na_bias_tables(rpb)
    rope_tab = _rope_tables()
    bd = _head_block_diag()

    mods_all = _modulation(cond8, w_mod, b_mod)[:, :N_SEG].reshape(L, N_SEG, N_MOD, D_MODEL)

    caches = None
    for l in range(L):
        mods = mods_all[l]
        qkv, gates, *caches = _in_projection(l, x_ctx, x_lat, mods, g_norm1, w_in_b, head_gains, bd, rope_tab,
                                             caches)
        o_ctx = _attention_ctx(l, qkv, lam_tab, g_subln).reshape(1, SEG_ROWS, D_MODEL)
        o_da = _attention_da(l, qkv, cache_da_k, cache_da_v, lam_tab, g_subln)
        o_na = _attention_na(l, qkv, cache_na_k, cache_na_v, na_bias)
        x1, h2, comb = _output_stage(l, x_ctx, x_lat, mods, o_ctx, o_da, o_na, gates, wa_b, wb_b, wo_b, g_norm2,
                                     w_router_t, b_router_c)
        x_ctx, x_lat = _experts(l, h2, comb, x1, mods, wg_all, wu_all, wd_all)

    y_prompt = x_ctx.reshape(BATCH, SEQ, D_MODEL)
    y_sample = x_lat
    new_da_k, new_da_v, new_na_k, new_na_v = caches
    return (y_prompt, y_sample, new_da_k, new_da_v, new_na_k, new_na_v)
```

```python
import functools
import math

import numpy as np
import jax
import jax.numpy as jnp
from jax import lax
from jax.experimental import pallas as pl
from jax.experimental.pallas import tpu as pltpu

D_MODEL = 1024
BATCH = 16
SEQ = 256
DEPTH = 4
DEC_BATCH = 4
DEC_SEQ = 4096
PAST_LEN = 512
GRID_W = 64
GRID_H = DEC_SEQ // GRID_W
HEAD_DIM = 64
DA_HEADS = 4
DA_WIDTH = 512
NA_HEADS = 8
NA_WIDTH = 512
NA_ROWS = 8
NA_COLS = 16
IN_COLS = 3 * DA_WIDTH + 3 * NA_WIDTH + 2 * D_MODEL
ROPE_BASE = 10000.0
ROPE_PAIRS = HEAD_DIM // 4
N_EXPERTS = 16
N_GROUPS = 4
EXPERTS_PER_GROUP = 4
D_EXPERT = 256
N_MOD = 6
EPS = 1e-6
ATTN_SCALE = HEAD_DIM ** -0.5
Q_PRESCALE = ATTN_SCALE * math.log2(math.e)
LOG2E = math.log2(math.e)

N_SEG = 1 + DEC_BATCH
SEG_ROWS = DEC_SEQ
LANES = 128
CHUNK = 512

VMEM_LIMIT = 56 * 1024 * 1024

F32 = jnp.float32
BF16 = jnp.bfloat16


def _dot(a, b):
    return jnp.dot(a, b, preferred_element_type=F32)


def _dot_nt(a, b):
    return lax.dot_general(a, b, (((1,), (1,)), ((), ())), preferred_element_type=F32)


def _params(*sem):
    return pltpu.CompilerParams(dimension_semantics=sem, vmem_limit_bytes=VMEM_LIMIT)


def _ctx_rows_spec(tm, width):
    nt = SEG_ROWS // tm
    return pl.BlockSpec((1, tm, width), lambda s, i: (0, jnp.where(s == 0, i, nt - 1), 0))


def _lat_rows_spec(tm, width):
    return pl.BlockSpec((1, tm, width), lambda s, i: (jnp.maximum(s - 1, 0), jnp.where(s == 0, 0, i), 0))


def _mod_kernel(cond_ref, w_ref, b_ref, o_ref):
    c = cond_ref[...]
    sc = (c * jax.nn.sigmoid(c)).astype(BF16)
    o_ref[0] = _dot(sc, w_ref[0].astype(BF16)) + b_ref[0]


def _modulation(cond8, w_mod, b_mod):
    tn = 1536
    return pl.pallas_call(
        _mod_kernel,
        out_shape=jax.ShapeDtypeStruct((DEPTH, 8, N_MOD * D_MODEL), F32),
        grid=(DEPTH, N_MOD * D_MODEL // tn),
        in_specs=[
            pl.BlockSpec((8, D_MODEL), lambda l, j: (0, 0)),
            pl.BlockSpec((1, D_MODEL, tn), lambda l, j: (l, 0, j)),
            pl.BlockSpec((1, 1, tn), lambda l, j: (l, 0, j)),
        ],
        out_specs=pl.BlockSpec((1, 8, tn), lambda l, j: (l, 0, j)),
        compiler_params=_params("arbitrary", "arbitrary"),
        name="modulation",
    )(cond8, w_mod, b_mod.reshape(DEPTH, 1, N_MOD * D_MODEL))


def _store_heads(dst_ref, b, lead, t, rows, width):
    for h in range(CHUNK // width):
        dst_ref[(b, 0) + lead + (h,)] = t[rows, h * width:(h + 1) * width]


def _inproj_kernel(xc_ref, xl_ref, mod_ref, g1_ref, w_ref, hg_ref, bd_ref, rope_ref, *rest):
    qkv_ref, gate_ref, cdk_ref, cdv_ref, cnk_ref, cnv_ref = rest[-6:]
    s = pl.program_id(0)
    x = jnp.where(s == 0, xc_ref[0], xl_ref[0])
    ms = jnp.mean(x * x, axis=-1, keepdims=True)
    y = x * lax.rsqrt(ms + EPS) * g1_ref[0]
    mod = mod_ref[0]
    h = (y * (1.0 + mod[1:2]) + mod[0:1]).astype(BF16)

    rope = rope_ref[0]
    cos4 = jnp.concatenate([rope[:, 0:LANES]] * 4, axis=1)
    sin_up4 = jnp.concatenate([rope[:, LANES:2 * LANES]] * 4, axis=1)
    sin_dn4 = jnp.concatenate([rope[:, 2 * LANES:3 * LANES]] * 4, axis=1)

    def head_norm(acc, row):
        sq = (acc * acc).astype(BF16)
        half = CHUNK // 2
        ssum = jnp.concatenate([_dot(sq[:, :half], bd_ref[...]), _dot(sq[:, half:], bd_ref[...])], axis=1)
        return acc * lax.rsqrt(ssum * (1.0 / HEAD_DIM) + EPS) * hg_ref[0, row:row + 1, :]

    def rope_rot(t):
        return (t * cos4 + pltpu.roll(t, CHUNK - ROPE_PAIRS, 1) * sin_up4
                + pltpu.roll(t, ROPE_PAIRS, 1) * sin_dn4)

    def proj(c):
        return _dot(h, w_ref[0, :, c * CHUNK:(c + 1) * CHUNK])

    is_ctx = s == 0
    batches = [(b, slice(b * SEQ, (b + 1) * SEQ)) for b in range(x.shape[0] // SEQ)]

    q = rope_rot(head_norm(proj(0), 0)) * Q_PRESCALE
    qkv_ref[0, :, 0:CHUNK] = q.astype(BF16)

    k = rope_rot(head_norm(proj(1), 1))
    qkv_ref[0, :, CHUNK:2 * CHUNK] = k.astype(BF16)

    @pl.when(is_ctx)
    def _():
        for b, rows in batches:
            for m in range(2):
                half = k[:, m * (CHUNK // 2):(m + 1) * (CHUNK // 2)]
                for h in range(DA_HEADS):
                    cdk_ref[b, 0, m, h] = half[rows, h * HEAD_DIM:(h + 1) * HEAD_DIM]

    v = proj(2)
    qkv_ref[0, :, 2 * CHUNK:3 * CHUNK] = v.astype(BF16)

    @pl.when(is_ctx)
    def _():
        for b, rows in batches:
            _store_heads(cdv_ref, b, (), v, rows, 2 * HEAD_DIM)

    q = head_norm(proj(3), 2) * Q_PRESCALE
    qkv_ref[0, :, 3 * CHUNK:4 * CHUNK] = q.astype(BF16)

    k = head_norm(proj(4), 3)
    qkv_ref[0, :, 4 * CHUNK:5 * CHUNK] = k.astype(BF16)

    @pl.when(is_ctx)
    def _():
        for b, rows in batches:
            _store_heads(cnk_ref, b, (), k, rows, HEAD_DIM)

    v = proj(5)
    qkv_ref[0, :, 5 * CHUNK:6 * CHUNK] = v.astype(BF16)

    @pl.when(is_ctx)
    def _():
        for b, rows in batches:
            _store_heads(cnv_ref, b, (), v, rows, HEAD_DIM)

    for c in range(4):
        g = proj(6 + c)
        gate_ref[0, :, c * CHUNK:(c + 1) * CHUNK] = jax.nn.sigmoid(g).astype(BF16)


CACHE_SHAPES = (
    (BATCH, DEPTH, 2, DA_HEADS, SEQ, HEAD_DIM),
    (BATCH, DEPTH, DA_HEADS, SEQ, 2 * HEAD_DIM),
    (BATCH, DEPTH, NA_HEADS, SEQ, HEAD_DIM),
    (BATCH, DEPTH, NA_HEADS, SEQ, HEAD_DIM),
)


def _in_projection(l, x_ctx, x_lat, mods, g_norm1, w_in_b, head_gains, bd, rope_tab, caches):
    tm = 512
    nt = SEG_ROWS // tm
    nb = tm // SEQ

    def cache_spec(shape):
        block = (nb, 1) + shape[2:]
        zeros = (0,) * (len(shape) - 2)
        return pl.BlockSpec(block, lambda s, i: (jnp.where(s == 0, i, nt - 1), l) + zeros)

    chained = caches is not None
    cache_in = list(caches) if chained else []
    n_in = 8
    return pl.pallas_call(
        _inproj_kernel,
        out_shape=(
            jax.ShapeDtypeStruct((N_SEG, SEG_ROWS, 6 * CHUNK), BF16),
            jax.ShapeDtypeStruct((N_SEG, SEG_ROWS, 2 * D_MODEL), BF16),
        ) + tuple(jax.ShapeDtypeStruct(shape, F32) for shape in CACHE_SHAPES),
        grid=(N_SEG, nt),
        in_specs=[
            _ctx_rows_spec(tm, D_MODEL),
            _lat_rows_spec(tm, D_MODEL),
            pl.BlockSpec((1, N_MOD, D_MODEL), lambda s, i: (s, 0, 0)),
            pl.BlockSpec((1, 1, D_MODEL), lambda s, i: (l, 0, 0)),
            pl.BlockSpec((1, D_MODEL, IN_COLS), lambda s, i: (l, 0, 0), pipeline_mode=pl.Buffered(1)),
            pl.BlockSpec((1, 4, CHUNK), lambda s, i: (l, 0, 0)),
            pl.BlockSpec((CHUNK // 2, CHUNK // 2), lambda s, i: (0, 0)),
            pl.BlockSpec((1, tm, 3 * LANES), lambda s, i: (jnp.minimum(s, 1), i, 0)),
        ] + [pl.BlockSpec(memory_space=pl.ANY) for _ in cache_in],
        out_specs=(
            pl.BlockSpec((1, tm, 6 * CHUNK), lambda s, i: (s, i, 0)),
            pl.BlockSpec((1, tm, 2 * D_MODEL), lambda s, i: (s, i, 0)),
        ) + tuple(cache_spec(shape) for shape in CACHE_SHAPES),
        input_output_aliases={n_in + j: 2 + j for j in range(len(cache_in))},
        compiler_params=_params("arbitrary", "arbitrary"),
        name="in_projection",
    )(x_ctx, x_lat, mods, g_norm1, w_in_b, head_gains, bd, rope_tab, *cache_in)


def _lam_value(lam_ref):
    lp = lam_ref[0]
    s1 = jnp.sum(lp[0:1] * lp[1:2], axis=-1, keepdims=True)
    s2 = jnp.sum(lp[2:3] * lp[3:4], axis=-1, keepdims=True)
    lam_init = lp[4:5, 0:1]
    return jnp.exp(s1) - jnp.exp(s2) + lam_init, 1.0 - lam_init


def _half_masks():
    lane = lax.broadcasted_iota(jnp.int32, (1, LANES), 1)
    return (lane < HEAD_DIM, lane >= HEAD_DIM)


def _pick_head(mask, t):
    return jnp.where(mask, t, jnp.zeros_like(t))


def _sub_layer_norm(o, gsub, one_minus):
    ms = jnp.mean(o * o, axis=-1, keepdims=True)
    return o * lax.rsqrt(ms + EPS) * gsub * one_minus


def _attn_ctx_kernel(qkv_ref, lam_ref, gsub_ref, o_ref):
    lam, one_minus = _lam_value(lam_ref)
    masks = _half_masks()
    qda = qkv_ref[0, :, 0:CHUNK]
    qna = qkv_ref[0, :, 3 * CHUNK:4 * CHUNK]
    gsub = gsub_ref[0]

    def kv(c0):
        off = c0 + CHUNK if c0 < 2 * CHUNK else c0 + 2 * CHUNK
        return qkv_ref[0, :, off:off + LANES]

    def softmax_parts(sc):
        m = jnp.max(sc, axis=-1, keepdims=True)
        e = jnp.exp2(sc - m)
        return e, 1.0 / jnp.sum(e, axis=-1, keepdims=True)

    for h in range(DA_HEADS):
        p, sub = divmod(h, 2)
        q1 = _pick_head(masks[sub], qda[:, p * LANES:(p + 1) * LANES])
        q2 = _pick_head(masks[sub], qda[:, 2 * LANES + p * LANES:2 * LANES + (p + 1) * LANES])
        e1, r1 = softmax_parts(_dot_nt(q1, kv(p * LANES)))
        e2, r2 = softmax_parts(_dot_nt(q2, kv(2 * LANES + p * LANES)))
        a = e1 * r1 - e2 * (lam * r2)
        o = _dot(a.astype(BF16), kv(CHUNK + h * LANES))
        o_ref[:, h * LANES:(h + 1) * LANES] = _sub_layer_norm(o, gsub, one_minus).astype(BF16)

    for p in range(NA_HEADS // 2):
        kp = kv(2 * CHUNK + p * LANES)
        vp = kv(3 * CHUNK + p * LANES)
        both = None
        for sub in range(2):
            qm = _pick_head(masks[sub], qna[:, p * LANES:(p + 1) * LANES])
            e, r = softmax_parts(_dot_nt(qm, kp))
            o = _dot(e.astype(BF16), vp) * r
            both = o if sub == 0 else jnp.where(masks[1], o, both)
        o_ref[:, DA_WIDTH + p * LANES:DA_WIDTH + (p + 1) * LANES] = both.astype(BF16)


def _attention_ctx(l, qkv, lam_tab, g_subln):
    return pl.pallas_call(
        _attn_ctx_kernel,
        out_shape=jax.ShapeDtypeStruct((SEG_ROWS, D_MODEL), BF16),
        grid=(BATCH,),
        in_specs=[
            pl.BlockSpec((1, SEQ, 6 * CHUNK), lambda b: (0, b, 0)),
            pl.BlockSpec((1, 8, LANES), lambda b: (l, 0, 0)),
            pl.BlockSpec((1, 1, LANES), lambda b: (l, 0, 0)),
        ],
        out_specs=pl.BlockSpec((SEQ, D_MODEL), lambda b: (b, 0)),
        compiler_params=_params("arbitrary"),
        name="attention_ctx",
    )(qkv, lam_tab, g_subln)


DA_KEYS = DEC_SEQ + PAST_LEN
DA_KCHUNK = 1024
DA_TQ = 256


def _pair_heads(head_ref, lead, first):
    a = head_ref[lead + (first,)]
    b = head_ref[lead + (first + 1,)]
    return jnp.concatenate([a, b], axis=1).astype(BF16)


def _attn_da_kernel(q_ref, k_ref, v_ref, kc_ref, vc_ref, lam_ref, gsub_ref, o_ref, vt_ref, kcs_ref, e1_ref, e2_ref):
    chunks = [slice(r, r + DA_KCHUNK) for r in range(0, DEC_SEQ, DA_KCHUNK)] + [slice(DEC_SEQ, DA_KEYS)]
    n_chunks = len(chunks)

    @pl.when(pl.program_id(1) == 0)
    def _():
        for r in range(0, DEC_SEQ, PAST_LEN):
            rows = slice(r, r + PAST_LEN)
            vt_ref[:, rows] = v_ref[0, rows, :].astype(F32).T.astype(BF16)
        for h in range(DA_HEADS):
            vt_ref[h * LANES:(h + 1) * LANES, DEC_SEQ:DA_KEYS] = vc_ref[0, 0, h].T.astype(BF16)
        for m in range(2):
            for p in range(DA_HEADS // 2):
                c0 = m * 2 * LANES + p * LANES
                kcs_ref[:, c0:c0 + LANES] = _pair_heads(kc_ref, (0, 0, m), 2 * p)

    lam, one_minus = _lam_value(lam_ref)
    masks = _half_masks()
    q = q_ref[0]
    gsub = gsub_ref[0]

    def key_chunk(c, cols):
        if chunks[c].start < DEC_SEQ:
            return k_ref[0, chunks[c], cols]
        return kcs_ref[:, cols]

    def score_map(qm, cols, e_ref, weight):
        maxes, sums = [], []
        for c in range(n_chunks):
            s = _dot_nt(key_chunk(c, cols), qm)
            mc = jnp.max(s, axis=0, keepdims=True)
            e = jnp.exp2(s - mc)
            sums.append(jnp.sum(e, axis=0, keepdims=True))
            maxes.append(mc)
            e_ref[chunks[c], :] = e.astype(BF16)
        m = functools.reduce(jnp.maximum, maxes)
        facs = [jnp.exp2(mc - m) for mc in maxes]
        tot = functools.reduce(lambda a, b: a + b, [lc * fc for lc, fc in zip(sums, facs)])
        w = weight / tot
        return [(fc * w).astype(BF16) for fc in facs]

    for h in range(DA_HEADS):
        p, sub = divmod(h, 2)
        c1 = slice(p * LANES, (p + 1) * LANES)
        c2 = slice(2 * LANES + p * LANES, 2 * LANES + (p + 1) * LANES)
        f1 = score_map(_pick_head(masks[sub], q[:, c1]), c1, e1_ref, 1.0)
        f2 = score_map(_pick_head(masks[sub], q[:, c2]), c2, e2_ref, lam)
        o_t = jnp.zeros((LANES, DA_TQ), F32)
        for c, rows in enumerate(chunks):
            a = e1_ref[rows, :] * f1[c] - e2_ref[rows, :] * f2[c]
            o_t = o_t + _dot(vt_ref[h * LANES:(h + 1) * LANES, rows], a)
        o_ref[0, :, h * LANES:(h + 1) * LANES] = _sub_layer_norm(o_t.T, gsub, one_minus).astype(BF16)


def _attention_da(l, qkv, ctx_k, ctx_v, lam_tab, g_subln):
    tq = DA_TQ
    return pl.pallas_call(
        _attn_da_kernel,
        out_shape=jax.ShapeDtypeStruct((DEC_BATCH, DEC_SEQ, DA_WIDTH), BF16),
        grid=(DEC_BATCH, DEC_SEQ // tq),
        in_specs=[
            pl.BlockSpec((1, tq, CHUNK), lambda b, i: (b + 1, i, 0)),
            pl.BlockSpec((1, DEC_SEQ, CHUNK), lambda b, i: (b + 1, 0, 1), pipeline_mode=pl.Buffered(1)),
            pl.BlockSpec((1, DEC_SEQ, CHUNK), lambda b, i: (b + 1, 0, 2), pipeline_mode=pl.Buffered(1)),
            pl.BlockSpec((1, 1, 2, DA_HEADS, PAST_LEN, HEAD_DIM), lambda b, i: (b, l, 0, 0, 0, 0)),
            pl.BlockSpec((1, 1, DA_HEADS, PAST_LEN, 2 * HEAD_DIM), lambda b, i: (b, l, 0, 0, 0)),
            pl.BlockSpec((1, 8, LANES), lambda b, i: (l, 0, 0)),
            pl.BlockSpec((1, 1, LANES), lambda b, i: (l, 0, 0)),
        ],
        out_specs=pl.BlockSpec((1, tq, DA_WIDTH), lambda b, i: (b, i, 0)),
        scratch_shapes=[
            pltpu.VMEM((DA_WIDTH, DA_KEYS), BF16),
            pltpu.VMEM((PAST_LEN, DA_WIDTH), BF16),
            pltpu.VMEM((DA_KEYS, tq), BF16),
            pltpu.VMEM((DA_KEYS, tq), BF16),
        ],
        compiler_params=_params("arbitrary", "arbitrary"),
        name="attention_da",
    )(qkv, qkv, qkv, ctx_k, ctx_v, lam_tab, g_subln)


NA_QROWS = 8
NA_KROWS = 16


def _attn_na_kernel(q_ref, k_ref, v_ref, kc_ref, vc_ref, bias_ref, o_ref, kcs_ref, vcs_ref):
    i = pl.program_id(1)

    @pl.when(i == 0)
    def _():
        for p in range(NA_HEADS // 2):
            kcs_ref[:, p * LANES:(p + 1) * LANES] = _pair_heads(kc_ref, (0, 0), 2 * p)
            vcs_ref[:, p * LANES:(p + 1) * LANES] = _pair_heads(vc_ref, (0, 0), 2 * p)

    masks = _half_masks()
    q = q_ref[0]
    row0 = jnp.clip(i * NA_QROWS - NA_ROWS // 2, 0, GRID_H - NA_KROWS)
    start = pl.multiple_of(row0 * GRID_W, 256)
    nk = NA_KROWS * GRID_W

    for p in range(NA_HEADS // 2):
        c0 = p * LANES
        kw = k_ref[0, pl.ds(start, nk), c0:c0 + LANES]
        vw = v_ref[0, pl.ds(start, nk), c0:c0 + LANES]
        kc = kcs_ref[:, c0:c0 + LANES]
        vc = vcs_ref[:, c0:c0 + LANES]
        both = None
        for sub in range(2):
            qm = _pick_head(masks[sub], q[:, c0:c0 + LANES])
            s_loc = _dot_nt(qm, kw) + bias_ref[0, 0, 2 * p + sub].astype(F32)
            s_ctx = _dot_nt(qm, kc)
            m = jnp.maximum(jnp.max(s_loc, axis=-1, keepdims=True), jnp.max(s_ctx, axis=-1, keepdims=True))
            e_loc = jnp.exp2(s_loc - m)
            e_ctx = jnp.exp2(s_ctx - m)
            tot = jnp.sum(e_loc, axis=-1, keepdims=True) + jnp.sum(e_ctx, axis=-1, keepdims=True)
            o = (_dot(e_loc.astype(BF16), vw) + _dot(e_ctx.astype(BF16), vc)) * (1.0 / tot)
            both = o if sub == 0 else jnp.where(masks[1], o, both)
        o_ref[0, :, c0:c0 + LANES] = both.astype(BF16)


def _attention_na(l, qkv, ctx_k, ctx_v, bias):
    tq = NA_QROWS * GRID_W
    n_i = DEC_SEQ // tq

    def bias_idx(b, i):
        return (l, jnp.where(i == 0, 0, jnp.where(i == n_i - 1, 2, 1)), 0, 0, 0)

    return pl.pallas_call(
        _attn_na_kernel,
        out_shape=jax.ShapeDtypeStruct((DEC_BATCH, DEC_SEQ, NA_WIDTH), BF16),
        grid=(DEC_BATCH, n_i),
        in_specs=[
            pl.BlockSpec((1, tq, CHUNK), lambda b, i: (b + 1, i, 3)),
            pl.BlockSpec((1, DEC_SEQ, CHUNK), lambda b, i: (b + 1, 0, 4), pipeline_mode=pl.Buffered(1)),
            pl.BlockSpec((1, DEC_SEQ, CHUNK), lambda b, i: (b + 1, 0, 5), pipeline_mode=pl.Buffered(1)),
            pl.BlockSpec((1, 1, NA_HEADS, PAST_LEN, HEAD_DIM), lambda b, i: (b, l, 0, 0, 0)),
            pl.BlockSpec((1, 1, NA_HEADS, PAST_LEN, HEAD_DIM), lambda b, i: (b, l, 0, 0, 0)),
            pl.BlockSpec((1, 1, NA_HEADS, tq, NA_KROWS * GRID_W), bias_idx),
        ],
        out_specs=pl.BlockSpec((1, tq, NA_WIDTH), lambda b, i: (b, i, 0)),
        scratch_shapes=[
            pltpu.VMEM((PAST_LEN, NA_WIDTH), BF16),
            pltpu.VMEM((PAST_LEN, NA_WIDTH), BF16),
        ],
        compiler_params=_params("arbitrary", "arbitrary"),
        name="attention_na",
    )(qkv, qkv, qkv, ctx_k, ctx_v, bias)


def _na_bias_tables(rpb):
    qr = np.arange(NA_QROWS)
    qc = np.arange(GRID_W)
    kr = np.arange(NA_KROWS)
    kc = np.arange(GRID_W)
    c0 = np.clip(qc - NA_COLS // 2, 0, GRID_W - NA_COLS)
    col_ok = (kc[None, :] >= c0[:, None]) & (kc[None, :] < c0[:, None] + NA_COLS)
    cpad = GRID_W - NA_COLS
    rpb_c = jnp.pad(rpb, ((0, 0), (0, 0), (0, 0), (cpad, cpad)))
    by_col = jnp.stack([rpb_c[..., GRID_W - 1 - c:2 * GRID_W - 1 - c] for c in range(GRID_W)], axis=-2)
    by_col = jnp.where(col_ok, by_col * LOG2E, -jnp.inf)

    plans = []
    for i in (0, 1, GRID_H // NA_QROWS - 1):
        rows_q = NA_QROWS * i + qr
        row0_k = min(max(NA_QROWS * i - NA_ROWS // 2, 0), GRID_H - NA_KROWS)
        rows_k = row0_k + kr
        r0 = np.clip(rows_q - NA_ROWS // 2, 0, GRID_H - NA_ROWS)
        row_ok = (rows_k[None, :] >= r0[:, None]) & (rows_k[None, :] < r0[:, None] + NA_ROWS)
        slab = rows_k[None, :] - rows_q[:, None] + NA_ROWS - 1
        plans.append([[int(slab[r, c]) if row_ok[r, c] else None for c in range(NA_KROWS)]
                      for r in range(NA_QROWS)])

    def tile_kernel(tab_ref, o_ref):
        kind = pl.program_id(1)
        for which, plan in enumerate(plans):
            @pl.when(kind == which)
            def _(plan=plan):
                for r in range(NA_QROWS):
                    for c in range(NA_KROWS):
                        if plan[r][c] is None:
                            blk = jnp.full((GRID_W, GRID_W), -jnp.inf, BF16)
                        else:
                            blk = tab_ref[0, 0, plan[r][c]].astype(BF16)
                        o_ref[0, 0, 0, r * GRID_W:(r + 1) * GRID_W, c * GRID_W:(c + 1) * GRID_W] = blk

    n_slab = 2 * NA_ROWS - 1
    return pl.pallas_call(
        tile_kernel,
        out_shape=jax.ShapeDtypeStruct((DEPTH, len(plans), NA_HEADS, NA_QROWS * GRID_W, NA_KROWS * GRID_W), BF16),
        grid=(DEPTH, len(plans), NA_HEADS),
        in_specs=[pl.BlockSpec((1, 1, n_slab, GRID_W, GRID_W), lambda l, t, h: (l, h, 0, 0, 0))],
        out_specs=pl.BlockSpec((1, 1, 1, NA_QROWS * GRID_W, NA_KROWS * GRID_W), lambda l, t, h: (l, t, h, 0, 0)),
        compiler_params=_params("arbitrary", "arbitrary", "arbitrary"),
        name="na_bias_tiles",
    )(by_col)


def _first_max_of_four(vals):
    a, b, c, d = vals
    m = jnp.maximum(jnp.maximum(a, b), jnp.maximum(c, d))
    idx = jnp.where(a == m, 0.0, jnp.where(b == m, 1.0, jnp.where(c == m, 2.0, 3.0)))
    return m, idx


MOE_CHUNK = EXPERTS_PER_GROUP * D_EXPERT


def _out_kernel(xc_ref, xl_ref, mod_ref, oc_ref, oda_ref, ona_ref, gate_ref, wa_ref, wb_ref, wo_ref, g2_ref,
                wrt_ref, br_ref, wg_ref, wu_ref, wd_ref, yc_ref, yl_ref):
    is_ctx = pl.program_id(0) == 0
    oc = oc_ref[0]
    oa = jnp.where(is_ctx, oc[:, :DA_WIDTH], oda_ref[0])
    ob = jnp.where(is_ctx, oc[:, DA_WIDTH:], ona_ref[0])
    ya = _dot(oa, wa_ref[0])
    yb = _dot(ob, wb_ref[0])
    g = gate_ref[0]
    mixed = (g[:, :D_MODEL].astype(F32) * ya + g[:, D_MODEL:].astype(F32) * yb).astype(BF16)
    mod = mod_ref[0]
    x1 = jnp.where(is_ctx, xc_ref[0], xl_ref[0]) + mod[2:3] * _dot(mixed, wo_ref[0])

    ms = jnp.mean(x1 * x1, axis=-1, keepdims=True)
    h2 = x1 * lax.rsqrt(ms + EPS) * g2_ref[0] * (1.0 + mod[4:5]) + mod[3:4]
    h2_hi = h2.astype(BF16)

    h2_lo = (h2 - h2_hi.astype(F32)).astype(BF16)
    wr = wrt_ref[...]
    wr_hi = wr.astype(BF16)
    wr_lo = (wr - wr_hi.astype(F32)).astype(BF16)
    by_hi = _dot_nt(jnp.concatenate([wr_hi, wr_lo], axis=0), h2_hi)
    logits = by_hi[:N_EXPERTS] + (by_hi[N_EXPERTS:] + _dot_nt(wr_hi, h2_lo))
    scores = jax.nn.sigmoid(logits)
    sel = scores + br_ref[...]
    score_rows = [scores[e:e + 1] for e in range(N_EXPERTS)]
    sel_rows = [sel[e:e + 1] for e in range(N_EXPERTS)]

    neg = -jnp.inf
    best = None
    for grp in range(N_GROUPS):
        vals = sel_rows[grp * EXPERTS_PER_GROUP:(grp + 1) * EXPERTS_PER_GROUP]
        m1, i1 = _first_max_of_four(vals)
        rest = [jnp.where(i1 == float(j), neg, v) for j, v in enumerate(vals)]
        m2, i2 = _first_max_of_four(rest)
        cand = (m1 + m2, i1 + float(grp * EXPERTS_PER_GROUP), i2 + float(grp * EXPERTS_PER_GROUP))
        if best is None:
            best = cand
        else:
            upd = cand[0] > best[0]
            best = tuple(jnp.where(upd, n, o) for n, o in zip(cand, best))
    _, e1, e2 = best
    hits1 = [e1 == float(e) for e in range(N_EXPERTS)]
    hits2 = [e2 == float(e) for e in range(N_EXPERTS)]
    s1 = functools.reduce(lambda a, b: a + b, [jnp.where(h, r, 0.0) for h, r in zip(hits1, score_rows)])
    s2 = functools.reduce(lambda a, b: a + b, [jnp.where(h, r, 0.0) for h, r in zip(hits2, score_rows)])
    den = s1 + s2
    w1 = s1 / den
    w2 = s2 / den
    rows = [jnp.where(h1, w1, 0.0) + jnp.where(h2_, w2, 0.0) for h1, h2_ in zip(hits1, hits2)]
    rows.append(jnp.zeros((LANES - N_EXPERTS, rows[0].shape[1]), F32))
    comb = jnp.concatenate(rows, axis=0).T

    acc = jnp.zeros(x1.shape, F32)
    for c in range(N_GROUPS):
        parts = []
        for j in range(EXPERTS_PER_GROUP):
            e = c * EXPERTS_PER_GROUP + j
            g = _dot(h2_hi, wg_ref[0, e])
            u = _dot(h2_hi, wu_ref[0, e])
            parts.append(g * jax.nn.sigmoid(g) * u * comb[:, e:e + 1])
        hs = jnp.concatenate(parts, axis=1).astype(BF16)
        acc = acc + _dot(hs, wd_ref[0, c * MOE_CHUNK:(c + 1) * MOE_CHUNK, :])
    out = x1 + mod[5:6] * acc

    @pl.when(is_ctx)
    def _():
        yc_ref[0] = out

    @pl.when(jnp.logical_not(is_ctx))
    def _():
        yl_ref[0] = out


def _output_stage(l, x_ctx, x_lat, mods, o_ctx, o_da, o_na, gates, wa_b, wb_b, wo_b, g_norm2, w_router_t,
                  b_router_c, wg_all, wu_all, wd_all):
    tm = 256
    nt = SEG_ROWS // tm
    hidden = N_EXPERTS * D_EXPERT
    once = pl.Buffered(1)
    return pl.pallas_call(
        _out_kernel,
        out_shape=(
            jax.ShapeDtypeStruct((1, SEG_ROWS, D_MODEL), F32),
            jax.ShapeDtypeStruct((DEC_BATCH, DEC_SEQ, D_MODEL), F32),
        ),
        grid=(N_SEG, nt),
        in_specs=[
            _ctx_rows_spec(tm, D_MODEL),
            _lat_rows_spec(tm, D_MODEL),
            pl.BlockSpec((1, N_MOD, D_MODEL), lambda s, i: (s, 0, 0)),
            _ctx_rows_spec(tm, D_MODEL),
            _lat_rows_spec(tm, DA_WIDTH),
            _lat_rows_spec(tm, NA_WIDTH),
            pl.BlockSpec((1, tm, 2 * D_MODEL), lambda s, i: (s, i, 0)),
            pl.BlockSpec((1, DA_WIDTH, D_MODEL), lambda s, i: (l, 0, 0)),
            pl.BlockSpec((1, NA_WIDTH, D_MODEL), lambda s, i: (l, 0, 0)),
            pl.BlockSpec((1, D_MODEL, D_MODEL), lambda s, i: (l, 0, 0)),
            pl.BlockSpec((1, 1, D_MODEL), lambda s, i: (l, 0, 0)),
            pl.BlockSpec((N_EXPERTS, D_MODEL), lambda s, i: (0, 0)),
            pl.BlockSpec((N_EXPERTS, 1), lambda s, i: (0, 0)),
            pl.BlockSpec((1, N_EXPERTS, D_MODEL, D_EXPERT), lambda s, i: (l, 0, 0, 0), pipeline_mode=once),
            pl.BlockSpec((1, N_EXPERTS, D_MODEL, D_EXPERT), lambda s, i: (l, 0, 0, 0), pipeline_mode=once),
            pl.BlockSpec((1, hidden, D_MODEL), lambda s, i: (l, 0, 0), pipeline_mode=once),
        ],
        out_specs=(_ctx_rows_spec(tm, D_MODEL), _lat_rows_spec(tm, D_MODEL)),
        compiler_params=_params("arbitrary", "arbitrary"),
        name="output_experts",
    )(x_ctx, x_lat, mods, o_ctx, o_da, o_na, gates, wa_b, wb_b, wo_b, g_norm2, w_router_t, b_router_c,
      wg_all, wu_all, wd_all)


def _rope_tables():
    t = np.arange(DEC_SEQ)
    pos = np.stack([t // GRID_W, t % GRID_W], axis=-1).astype(np.float32)
    inv_freq = (ROPE_BASE ** (-np.arange(ROPE_PAIRS, dtype=np.float32) / ROPE_PAIRS)).astype(np.float32)
    ang = pos[:, :, None] * inv_freq
    lane = np.arange(LANES)
    axis = (lane % HEAD_DIM) // (2 * ROPE_PAIRS)
    pair = lane % ROPE_PAIRS
    second = ((lane // ROPE_PAIRS) % 2).astype(bool)
    a = ang[:, axis, pair]
    cos = np.cos(a)
    sin = np.sin(a)
    lat = np.concatenate([cos, np.where(second, 0.0, -sin), np.where(second, sin, 0.0)], axis=1)
    ident = np.concatenate([np.ones_like(cos), np.zeros_like(cos), np.zeros_like(cos)], axis=1)
    return jnp.asarray(np.stack([ident, lat]).astype(np.float32))


def _head_block_diag():
    r = np.arange(CHUNK // 2) // HEAD_DIM
    return jnp.asarray((r[:, None] == r[None, :]).astype(np.float32), dtype=BF16)


def kernel(x_prompt, x_sample, cache_da_k, cache_da_v, cache_na_k, cache_na_v, c, c_ctx, w_mod, b_mod, g_norm1,
           g_norm2, w_in, g_q_da, g_k_da, g_q_na, g_k_na, lam_q1, lam_k1, lam_q2, lam_k2, g_subln, rpb, w_br_a,
           w_br_b, w_out, w_router, b_router, w_gate, w_up, w_down):
    L = DEPTH
    x_ctx = x_prompt.reshape(1, SEG_ROWS, D_MODEL)
    x_lat = x_sample
    cond8 = jnp.concatenate([c_ctx[None], c, jnp.zeros((8 - N_SEG, D_MODEL), F32)], axis=0)
    w_in_b = w_in.astype(BF16)
    wa_b = w_br_a.astype(BF16)
    wb_b = w_br_b.astype(BF16)
    wo_b = w_out.astype(BF16)
    hidden = N_EXPERTS * D_EXPERT
    wg_all = w_gate.astype(BF16)
    wu_all = w_up.astype(BF16)
    wd_all = w_down.astype(BF16).reshape(L, hidden, D_MODEL)
    w_router_t = w_router.T
    b_router_c = b_router.reshape(N_EXPERTS, 1)
    head_gains = jnp.stack([jnp.tile(g, (1, CHUNK // HEAD_DIM)) for g in (g_q_da, g_k_da, g_q_na, g_k_na)],
                           axis=1).reshape(L, 4, CHUNK)
    g_norm1 = g_norm1.reshape(L, 1, D_MODEL)
    g_norm2 = g_norm2.reshape(L, 1, D_MODEL)
    g_subln = g_subln.reshape(L, 1, LANES)
    pad64 = lambda t: jnp.pad(t, ((0, 0), (0, LANES - HEAD_DIM)))
    lam_inits = jnp.asarray([0.8 - 0.6 * math.exp(-0.3 * l) for l in range(L)], F32)
    lam_tab = jnp.stack([pad64(lam_q1), pad64(lam_k1), pad64(lam_q2), pad64(lam_k2),
                         jnp.broadcast_to(lam_inits[:, None], (L, LANES)),
                         jnp.zeros((L, LANES), F32), jnp.zeros((L, LANES), F32), jnp.zeros((L, LANES), F32)],
                        axis=1)
    na_bias = _na_bias_tables(rpb)
    rope_tab = _rope_tables()
    bd = _head_block_diag()

    mods_all = _modulation(cond8, w_mod, b_mod)[:, :N_SEG].reshape(L, N_SEG, N_MOD, D_MODEL)

    caches = None
    for l in range(L):
        mods = mods_all[l]
        qkv, gates, *caches = _in_projection(l, x_ctx, x_lat, mods, g_norm1, w_in_b, head_gains, bd, rope_tab,
                                             caches)
        o_ctx = _attention_ctx(l, qkv, lam_tab, g_subln).reshape(1, SEG_ROWS, D_MODEL)
        o_da = _attention_da(l, qkv, cache_da_k, cache_da_v, lam_tab, g_subln)
        o_na = _attention_na(l, qkv, cache_na_k, cache_na_v, na_bias)
        x_ctx, x_lat = _output_stage(l, x_ctx, x_lat, mods, o_ctx, o_da, o_na, gates, wa_b, wb_b, wo_b, g_norm2,
                                     w_router_t, b_router_c, wg_all, wu_all, wd_all)

    y_prompt = x_ctx.reshape(BATCH, SEQ, D_MODEL)
    y_sample = x_lat
    new_da_k, new_da_v, new_na_k, new_na_v = caches
    return (y_prompt, y_sample, new_da_k, new_da_v, new_na_k, new_na_v)
```

```python
import functools
import math

import numpy as np
import jax
import jax.numpy as jnp
from jax import lax
from jax.experimental import pallas as pl
from jax.experimental.pallas import tpu as pltpu

D_MODEL = 1024
BATCH = 16
SEQ = 256
DEPTH = 4
DEC_BATCH = 4
DEC_SEQ = 4096
PAST_LEN = 512
GRID_W = 64
GRID_H = DEC_SEQ // GRID_W
HEAD_DIM = 64
DA_HEADS = 4
DA_WIDTH = 512
NA_HEADS = 8
NA_WIDTH = 512
NA_ROWS = 8
NA_COLS = 16
IN_COLS = 3 * DA_WIDTH + 3 * NA_WIDTH + 2 * D_MODEL
ROPE_BASE = 10000.0
ROPE_PAIRS = HEAD_DIM // 4
N_EXPERTS = 16
N_GROUPS = 4
EXPERTS_PER_GROUP = 4
D_EXPERT = 256
N_MOD = 6
EPS = 1e-6
ATTN_SCALE = HEAD_DIM ** -0.5
Q_PRESCALE = ATTN_SCALE * math.log2(math.e)
LOG2E = math.log2(math.e)

N_SEG = 1 + DEC_BATCH
SEG_ROWS = DEC_SEQ
LANES = 128
CHUNK = 512

VMEM_LIMIT = 56 * 1024 * 1024

F32 = jnp.float32
BF16 = jnp.bfloat16


def _dot(a, b):
    return jnp.dot(a, b, preferred_element_type=F32)


def _dot_nt(a, b):
    return lax.dot_general(a, b, (((1,), (1,)), ((), ())), preferred_element_type=F32)


def _params(*sem):
    return pltpu.CompilerParams(dimension_semantics=sem, vmem_limit_bytes=VMEM_LIMIT)


def _ctx_rows_spec(tm, width):
    nt = SEG_ROWS // tm
    return pl.BlockSpec((1, tm, width), lambda s, i: (0, jnp.where(s == 0, i, nt - 1), 0))


def _lat_rows_spec(tm, width):
    return pl.BlockSpec((1, tm, width), lambda s, i: (jnp.maximum(s - 1, 0), jnp.where(s == 0, 0, i), 0))


def _mod_kernel(cond_ref, w_ref, b_ref, o_ref):
    c = cond_ref[...]
    sc = (c * jax.nn.sigmoid(c)).astype(BF16)
    o_ref[0] = _dot(sc, w_ref[0].astype(BF16)) + b_ref[0]


def _modulation(cond8, w_mod, b_mod):
    tn = 1536
    return pl.pallas_call(
        _mod_kernel,
        out_shape=jax.ShapeDtypeStruct((DEPTH, 8, N_MOD * D_MODEL), F32),
        grid=(DEPTH, N_MOD * D_MODEL // tn),
        in_specs=[
            pl.BlockSpec((8, D_MODEL), lambda l, j: (0, 0)),
            pl.BlockSpec((1, D_MODEL, tn), lambda l, j: (l, 0, j)),
            pl.BlockSpec((1, 1, tn), lambda l, j: (l, 0, j)),
        ],
        out_specs=pl.BlockSpec((1, 8, tn), lambda l, j: (l, 0, j)),
        compiler_params=_params("arbitrary", "arbitrary"),
        name="modulation",
    )(cond8, w_mod, b_mod.reshape(DEPTH, 1, N_MOD * D_MODEL))


def _store_heads_t(dst_ref, b, t, rows):
    tt = t[rows].T
    lead = dst_ref.shape[2:-2]
    for h in range(CHUNK // HEAD_DIM):
        idx = np.unravel_index(h, lead)
        dst_ref[(b, 0) + tuple(int(i) for i in idx)] = tt[h * HEAD_DIM:(h + 1) * HEAD_DIM, :]


def _inproj_kernel(xc_ref, xl_ref, mod_ref, g1_ref, w_ref, hg_ref, bd_ref, rope_ref, *rest):
    qkv_ref, gate_ref, cdk_ref, cdv_ref, cnk_ref, cnv_ref = rest[-6:]
    s = pl.program_id(0)
    x = jnp.where(s == 0, xc_ref[0], xl_ref[0])
    ms = jnp.mean(x * x, axis=-1, keepdims=True)
    y = x * lax.rsqrt(ms + EPS) * g1_ref[0]
    mod = mod_ref[0]
    h = (y * (1.0 + mod[1:2]) + mod[0:1]).astype(BF16)

    rope = rope_ref[0]
    cos4 = jnp.concatenate([rope[:, 0:LANES]] * 4, axis=1)
    sin_up4 = jnp.concatenate([rope[:, LANES:2 * LANES]] * 4, axis=1)
    sin_dn4 = jnp.concatenate([rope[:, 2 * LANES:3 * LANES]] * 4, axis=1)

    def head_norm(acc, row):
        sq = (acc * acc).astype(BF16)
        half = CHUNK // 2
        ssum = jnp.concatenate([_dot(sq[:, :half], bd_ref[...]), _dot(sq[:, half:], bd_ref[...])], axis=1)
        return acc * lax.rsqrt(ssum * (1.0 / HEAD_DIM) + EPS) * hg_ref[0, row:row + 1, :]

    def rope_rot(t):
        return (t * cos4 + pltpu.roll(t, CHUNK - ROPE_PAIRS, 1) * sin_up4
                + pltpu.roll(t, ROPE_PAIRS, 1) * sin_dn4)

    def proj(c):
        return _dot(h, w_ref[0, :, c * CHUNK:(c + 1) * CHUNK])

    is_ctx = s == 0
    batches = [(b, slice(b * SEQ, (b + 1) * SEQ)) for b in range(x.shape[0] // SEQ)]

    q = rope_rot(head_norm(proj(0), 0)) * Q_PRESCALE
    qkv_ref[0, :, 0:CHUNK] = q.astype(BF16)

    k = rope_rot(head_norm(proj(1), 1))
    qkv_ref[0, :, CHUNK:2 * CHUNK] = k.astype(BF16)

    @pl.when(is_ctx)
    def _():
        for b, rows in batches:
            _store_heads_t(cdk_ref, b, k, rows)

    v = proj(2)
    qkv_ref[0, :, 2 * CHUNK:3 * CHUNK] = v.astype(BF16)

    @pl.when(is_ctx)
    def _():
        for b, rows in batches:
            for h in range(DA_HEADS):
                cdv_ref[b, 0, h] = v[rows, h * LANES:(h + 1) * LANES]

    q = head_norm(proj(3), 2) * Q_PRESCALE
    qkv_ref[0, :, 3 * CHUNK:4 * CHUNK] = q.astype(BF16)

    k = head_norm(proj(4), 3)
    qkv_ref[0, :, 4 * CHUNK:5 * CHUNK] = k.astype(BF16)

    @pl.when(is_ctx)
    def _():
        for b, rows in batches:
            _store_heads_t(cnk_ref, b, k, rows)

    v = proj(5)
    qkv_ref[0, :, 5 * CHUNK:6 * CHUNK] = v.astype(BF16)

    @pl.when(is_ctx)
    def _():
        for b, rows in batches:
            _store_heads_t(cnv_ref, b, v, rows)

    for c in range(4):
        g = proj(6 + c)
        gate_ref[0, :, c * CHUNK:(c + 1) * CHUNK] = jax.nn.sigmoid(g).astype(BF16)


CACHE_SHAPES = (
    (BATCH, DEPTH, 2, DA_HEADS, HEAD_DIM, SEQ),
    (BATCH, DEPTH, DA_HEADS, SEQ, 2 * HEAD_DIM),
    (BATCH, DEPTH, NA_HEADS, HEAD_DIM, SEQ),
    (BATCH, DEPTH, NA_HEADS, HEAD_DIM, SEQ),
)


def _in_projection(l, x_ctx, x_lat, mods, g_norm1, w_in_b, head_gains, bd, rope_tab, caches):
    tm = 512
    nt = SEG_ROWS // tm
    nb = tm // SEQ

    def cache_spec(shape):
        block = (nb, 1) + shape[2:]
        zeros = (0,) * (len(shape) - 2)
        return pl.BlockSpec(block, lambda s, i: (jnp.where(s == 0, i, nt - 1), l) + zeros)

    chained = caches is not None
    cache_in = list(caches) if chained else []
    n_in = 8
    return pl.pallas_call(
        _inproj_kernel,
        out_shape=(
            jax.ShapeDtypeStruct((N_SEG, SEG_ROWS, 6 * CHUNK), BF16),
            jax.ShapeDtypeStruct((N_SEG, SEG_ROWS, 2 * D_MODEL), BF16),
        ) + tuple(jax.ShapeDtypeStruct(shape, F32) for shape in CACHE_SHAPES),
        grid=(N_SEG, nt),
        in_specs=[
            _ctx_rows_spec(tm, D_MODEL),
            _lat_rows_spec(tm, D_MODEL),
            pl.BlockSpec((1, N_MOD, D_MODEL), lambda s, i: (s, 0, 0)),
            pl.BlockSpec((1, 1, D_MODEL), lambda s, i: (l, 0, 0)),
            pl.BlockSpec((1, D_MODEL, IN_COLS), lambda s, i: (l, 0, 0), pipeline_mode=pl.Buffered(1)),
            pl.BlockSpec((1, 4, CHUNK), lambda s, i: (l, 0, 0)),
            pl.BlockSpec((CHUNK // 2, CHUNK // 2), lambda s, i: (0, 0)),
            pl.BlockSpec((1, tm, 3 * LANES), lambda s, i: (jnp.minimum(s, 1), i, 0)),
        ] + [pl.BlockSpec(memory_space=pl.ANY) for _ in cache_in],
        out_specs=(
            pl.BlockSpec((1, tm, 6 * CHUNK), lambda s, i: (s, i, 0)),
            pl.BlockSpec((1, tm, 2 * D_MODEL), lambda s, i: (s, i, 0)),
        ) + tuple(cache_spec(shape) for shape in CACHE_SHAPES),
        input_output_aliases={n_in + j: 2 + j for j in range(len(cache_in))},
        compiler_params=_params("arbitrary", "arbitrary"),
        name="in_projection",
    )(x_ctx, x_lat, mods, g_norm1, w_in_b, head_gains, bd, rope_tab, *cache_in)


def _lam_value(lam_ref):
    lp = lam_ref[0]
    s1 = jnp.sum(lp[0:1] * lp[1:2], axis=-1, keepdims=True)
    s2 = jnp.sum(lp[2:3] * lp[3:4], axis=-1, keepdims=True)
    lam_init = lp[4:5, 0:1]
    return jnp.exp(s1) - jnp.exp(s2) + lam_init, 1.0 - lam_init


def _half_masks():
    lane = lax.broadcasted_iota(jnp.int32, (1, LANES), 1)
    return (lane < HEAD_DIM, lane >= HEAD_DIM)


def _pick_head(mask, t):
    return jnp.where(mask, t, jnp.zeros_like(t))


def _sub_layer_norm(o, gsub, one_minus):
    ms = jnp.mean(o * o, axis=-1, keepdims=True)
    return o * lax.rsqrt(ms + EPS) * gsub * one_minus


def _attn_ctx_kernel(qkv_ref, lam_ref, gsub_ref, o_ref):
    lam, one_minus = _lam_value(lam_ref)
    masks = _half_masks()
    qda = qkv_ref[0, :, 0:CHUNK]
    qna = qkv_ref[0, :, 3 * CHUNK:4 * CHUNK]
    gsub = gsub_ref[0]

    def kv(c0):
        off = c0 + CHUNK if c0 < 2 * CHUNK else c0 + 2 * CHUNK
        return qkv_ref[0, :, off:off + LANES]

    def softmax_parts(sc):
        m = jnp.max(sc, axis=-1, keepdims=True)
        e = jnp.exp2(sc - m)
        return e, 1.0 / jnp.sum(e, axis=-1, keepdims=True)

    for h in range(DA_HEADS):
        p, sub = divmod(h, 2)
        q1 = _pick_head(masks[sub], qda[:, p * LANES:(p + 1) * LANES])
        q2 = _pick_head(masks[sub], qda[:, 2 * LANES + p * LANES:2 * LANES + (p + 1) * LANES])
        e1, r1 = softmax_parts(_dot_nt(q1, kv(p * LANES)))
        e2, r2 = softmax_parts(_dot_nt(q2, kv(2 * LANES + p * LANES)))
        a = e1 * r1 - e2 * (lam * r2)
        o = _dot(a.astype(BF16), kv(CHUNK + h * LANES))
        o_ref[:, h * LANES:(h + 1) * LANES] = _sub_layer_norm(o, gsub, one_minus).astype(BF16)

    for p in range(NA_HEADS // 2):
        kp = kv(2 * CHUNK + p * LANES)
        vp = kv(3 * CHUNK + p * LANES)
        both = None
        for sub in range(2):
            qm = _pick_head(masks[sub], qna[:, p * LANES:(p + 1) * LANES])
            e, r = softmax_parts(_dot_nt(qm, kp))
            o = _dot(e.astype(BF16), vp) * r
            both = o if sub == 0 else jnp.where(masks[1], o, both)
        o_ref[:, DA_WIDTH + p * LANES:DA_WIDTH + (p + 1) * LANES] = both.astype(BF16)


def _attention_ctx(l, qkv, lam_tab, g_subln):
    return pl.pallas_call(
        _attn_ctx_kernel,
        out_shape=jax.ShapeDtypeStruct((SEG_ROWS, D_MODEL), BF16),
        grid=(BATCH,),
        in_specs=[
            pl.BlockSpec((1, SEQ, 6 * CHUNK), lambda b: (0, b, 0)),
            pl.BlockSpec((1, 8, LANES), lambda b: (l, 0, 0)),
            pl.BlockSpec((1, 1, LANES), lambda b: (l, 0, 0)),
        ],
        out_specs=pl.BlockSpec((SEQ, D_MODEL), lambda b: (b, 0)),
        compiler_params=_params("arbitrary"),
        name="attention_ctx",
    )(qkv, lam_tab, g_subln)


DA_KEYS = DEC_SEQ + PAST_LEN
DA_KCHUNK = 1024
DA_TQ = 256


def _pair_heads(head_ref, lead, first):
    a = head_ref[lead + (first,)]
    b = head_ref[lead + (first + 1,)]
    return jnp.concatenate([a, b], axis=0).T.astype(BF16)


def _attn_da_kernel(q_ref, k_ref, v_ref, kc_ref, vc_ref, lam_ref, gsub_ref, o_ref, vt_ref, kcs_ref, e1_ref, e2_ref):
    chunks = [slice(r, r + DA_KCHUNK) for r in range(0, DEC_SEQ, DA_KCHUNK)] + [slice(DEC_SEQ, DA_KEYS)]
    n_chunks = len(chunks)

    @pl.when(pl.program_id(1) == 0)
    def _():
        for r in range(0, DEC_SEQ, PAST_LEN):
            rows = slice(r, r + PAST_LEN)
            vt_ref[:, rows] = v_ref[0, rows, :].astype(F32).T.astype(BF16)
        for h in range(DA_HEADS):
            vt_ref[h * LANES:(h + 1) * LANES, DEC_SEQ:DA_KEYS] = vc_ref[0, 0, h].T.astype(BF16)
        for m in range(2):
            for p in range(DA_HEADS // 2):
                c0 = m * 2 * LANES + p * LANES
                kcs_ref[:, c0:c0 + LANES] = _pair_heads(kc_ref, (0, 0, m), 2 * p)

    lam, one_minus = _lam_value(lam_ref)
    masks = _half_masks()
    q = q_ref[0]
    gsub = gsub_ref[0]

    def key_chunk(c, cols):
        if chunks[c].start < DEC_SEQ:
            return k_ref[0, chunks[c], cols]
        return kcs_ref[:, cols]

    def score_map(qm, cols, e_ref, weight):
        maxes, sums = [], []
        for c in range(n_chunks):
            s = _dot_nt(key_chunk(c, cols), qm)
            mc = jnp.max(s, axis=0, keepdims=True)
            e = jnp.exp2(s - mc)
            sums.append(jnp.sum(e, axis=0, keepdims=True))
            maxes.append(mc)
            e_ref[chunks[c], :] = e.astype(BF16)
        m = functools.reduce(jnp.maximum, maxes)
        facs = [jnp.exp2(mc - m) for mc in maxes]
        tot = functools.reduce(lambda a, b: a + b, [lc * fc for lc, fc in zip(sums, facs)])
        w = weight / tot
        return [(fc * w).astype(BF16) for fc in facs]

    for h in range(DA_HEADS):
        p, sub = divmod(h, 2)
        c1 = slice(p * LANES, (p + 1) * LANES)
        c2 = slice(2 * LANES + p * LANES, 2 * LANES + (p + 1) * LANES)
        f1 = score_map(_pick_head(masks[sub], q[:, c1]), c1, e1_ref, 1.0)
        f2 = score_map(_pick_head(masks[sub], q[:, c2]), c2, e2_ref, lam)
        o_t = jnp.zeros((LANES, DA_TQ), F32)
        for c, rows in enumerate(chunks):
            a = e1_ref[rows, :] * f1[c] - e2_ref[rows, :] * f2[c]
            o_t = o_t + _dot(vt_ref[h * LANES:(h + 1) * LANES, rows], a)
        o_ref[0, :, h * LANES:(h + 1) * LANES] = _sub_layer_norm(o_t.T, gsub, one_minus).astype(BF16)


def _attention_da(l, qkv, ctx_k, ctx_v, lam_tab, g_subln):
    tq = DA_TQ
    return pl.pallas_call(
        _attn_da_kernel,
        out_shape=jax.ShapeDtypeStruct((DEC_BATCH, DEC_SEQ, DA_WIDTH), BF16),
        grid=(DEC_BATCH, DEC_SEQ // tq),
        in_specs=[
            pl.BlockSpec((1, tq, CHUNK), lambda b, i: (b + 1, i, 0)),
            pl.BlockSpec((1, DEC_SEQ, CHUNK), lambda b, i: (b + 1, 0, 1), pipeline_mode=pl.Buffered(1)),
            pl.BlockSpec((1, DEC_SEQ, CHUNK), lambda b, i: (b + 1, 0, 2), pipeline_mode=pl.Buffered(1)),
            pl.BlockSpec((1, 1, 2, DA_HEADS, HEAD_DIM, PAST_LEN), lambda b, i: (b, l, 0, 0, 0, 0)),
            pl.BlockSpec((1, 1, DA_HEADS, PAST_LEN, 2 * HEAD_DIM), lambda b, i: (b, l, 0, 0, 0)),
            pl.BlockSpec((1, 8, LANES), lambda b, i: (l, 0, 0)),
            pl.BlockSpec((1, 1, LANES), lambda b, i: (l, 0, 0)),
        ],
        out_specs=pl.BlockSpec((1, tq, DA_WIDTH), lambda b, i: (b, i, 0)),
        scratch_shapes=[
            pltpu.VMEM((DA_WIDTH, DA_KEYS), BF16),
            pltpu.VMEM((PAST_LEN, DA_WIDTH), BF16),
            pltpu.VMEM((DA_KEYS, tq), BF16),
            pltpu.VMEM((DA_KEYS, tq), BF16),
        ],
        compiler_params=_params("arbitrary", "arbitrary"),
        name="attention_da",
    )(qkv, qkv, qkv, ctx_k, ctx_v, lam_tab, g_subln)


NA_QROWS = 8
NA_KROWS = 16


def _attn_na_kernel(q_ref, k_ref, v_ref, kc_ref, vc_ref, bias_ref, o_ref, kcs_ref, vcs_ref):
    i = pl.program_id(1)

    @pl.when(i == 0)
    def _():
        for p in range(NA_HEADS // 2):
            kcs_ref[:, p * LANES:(p + 1) * LANES] = _pair_heads(kc_ref, (0, 0), 2 * p)
            vcs_ref[:, p * LANES:(p + 1) * LANES] = _pair_heads(vc_ref, (0, 0), 2 * p)

    masks = _half_masks()
    q = q_ref[0]
    row0 = jnp.clip(i * NA_QROWS - NA_ROWS // 2, 0, GRID_H - NA_KROWS)
    start = pl.multiple_of(row0 * GRID_W, 256)
    nk = NA_KROWS * GRID_W

    for p in range(NA_HEADS // 2):
        c0 = p * LANES
        kw = k_ref[0, pl.ds(start, nk), c0:c0 + LANES]
        vw = v_ref[0, pl.ds(start, nk), c0:c0 + LANES]
        kc = kcs_ref[:, c0:c0 + LANES]
        vc = vcs_ref[:, c0:c0 + LANES]
        both = None
        for sub in range(2):
            qm = _pick_head(masks[sub], q[:, c0:c0 + LANES])
            s_loc = _dot_nt(qm, kw) + bias_ref[0, 0, 2 * p + sub].astype(F32)
            s_ctx = _dot_nt(qm, kc)
            m = jnp.maximum(jnp.max(s_loc, axis=-1, keepdims=True), jnp.max(s_ctx, axis=-1, keepdims=True))
            e_loc = jnp.exp2(s_loc - m)
            e_ctx = jnp.exp2(s_ctx - m)
            tot = jnp.sum(e_loc, axis=-1, keepdims=True) + jnp.sum(e_ctx, axis=-1, keepdims=True)
            o = (_dot(e_loc.astype(BF16), vw) + _dot(e_ctx.astype(BF16), vc)) * (1.0 / tot)
            both = o if sub == 0 else jnp.where(masks[1], o, both)
        o_ref[0, :, c0:c0 + LANES] = both.astype(BF16)


def _attention_na(l, qkv, ctx_k, ctx_v, bias):
    tq = NA_QROWS * GRID_W
    n_i = DEC_SEQ // tq

    def bias_idx(b, i):
        return (l, jnp.where(i == 0, 0, jnp.where(i == n_i - 1, 2, 1)), 0, 0, 0)

    return pl.pallas_call(
        _attn_na_kernel,
        out_shape=jax.ShapeDtypeStruct((DEC_BATCH, DEC_SEQ, NA_WIDTH), BF16),
        grid=(DEC_BATCH, n_i),
        in_specs=[
            pl.BlockSpec((1, tq, CHUNK), lambda b, i: (b + 1, i, 3)),
            pl.BlockSpec((1, DEC_SEQ, CHUNK), lambda b, i: (b + 1, 0, 4), pipeline_mode=pl.Buffered(1)),
            pl.BlockSpec((1, DEC_SEQ, CHUNK), lambda b, i: (b + 1, 0, 5), pipeline_mode=pl.Buffered(1)),
            pl.BlockSpec((1, 1, NA_HEADS, HEAD_DIM, PAST_LEN), lambda b, i: (b, l, 0, 0, 0)),
            pl.BlockSpec((1, 1, NA_HEADS, HEAD_DIM, PAST_LEN), lambda b, i: (b, l, 0, 0, 0)),
            pl.BlockSpec((1, 1, NA_HEADS, tq, NA_KROWS * GRID_W), bias_idx),
        ],
        out_specs=pl.BlockSpec((1, tq, NA_WIDTH), lambda b, i: (b, i, 0)),
        scratch_shapes=[
            pltpu.VMEM((PAST_LEN, NA_WIDTH), BF16),
            pltpu.VMEM((PAST_LEN, NA_WIDTH), BF16),
        ],
        compiler_params=_params("arbitrary", "arbitrary"),
        name="attention_na",
    )(qkv, qkv, qkv, ctx_k, ctx_v, bias)


def _na_bias_tables(rpb):
    qr = np.arange(NA_QROWS)
    qc = np.arange(GRID_W)
    kr = np.arange(NA_KROWS)
    kc = np.arange(GRID_W)
    c0 = np.clip(qc - NA_COLS // 2, 0, GRID_W - NA_COLS)
    col_ok = (kc[None, :] >= c0[:, None]) & (kc[None, :] < c0[:, None] + NA_COLS)
    cpad = GRID_W - NA_COLS
    rpb_c = jnp.pad(rpb, ((0, 0), (0, 0), (0, 0), (cpad, cpad)))
    by_col = jnp.stack([rpb_c[..., GRID_W - 1 - c:2 * GRID_W - 1 - c] for c in range(GRID_W)], axis=-2)
    by_col = jnp.where(col_ok, by_col * LOG2E, -jnp.inf)

    plans = []
    for i in (0, 1, GRID_H // NA_QROWS - 1):
        rows_q = NA_QROWS * i + qr
        row0_k = min(max(NA_QROWS * i - NA_ROWS // 2, 0), GRID_H - NA_KROWS)
        rows_k = row0_k + kr
        r0 = np.clip(rows_q - NA_ROWS // 2, 0, GRID_H - NA_ROWS)
        row_ok = (rows_k[None, :] >= r0[:, None]) & (rows_k[None, :] < r0[:, None] + NA_ROWS)
        slab = rows_k[None, :] - rows_q[:, None] + NA_ROWS - 1
        plans.append([[int(slab[r, c]) if row_ok[r, c] else None for c in range(NA_KROWS)]
                      for r in range(NA_QROWS)])

    def tile_kernel(tab_ref, o_ref):
        kind = pl.program_id(1)
        for which, plan in enumerate(plans):
            @pl.when(kind == which)
            def _(plan=plan):
                for r in range(NA_QROWS):
                    for c in range(NA_KROWS):
                        if plan[r][c] is None:
                            blk = jnp.full((GRID_W, GRID_W), -jnp.inf, BF16)
                        else:
                            blk = tab_ref[0, 0, plan[r][c]].astype(BF16)
                        o_ref[0, 0, 0, r * GRID_W:(r + 1) * GRID_W, c * GRID_W:(c + 1) * GRID_W] = blk

    n_slab = 2 * NA_ROWS - 1
    return pl.pallas_call(
        tile_kernel,
        out_shape=jax.ShapeDtypeStruct((DEPTH, len(plans), NA_HEADS, NA_QROWS * GRID_W, NA_KROWS * GRID_W), BF16),
        grid=(DEPTH, len(plans), NA_HEADS),
        in_specs=[pl.BlockSpec((1, 1, n_slab, GRID_W, GRID_W), lambda l, t, h: (l, h, 0, 0, 0))],
        out_specs=pl.BlockSpec((1, 1, 1, NA_QROWS * GRID_W, NA_KROWS * GRID_W), lambda l, t, h: (l, t, h, 0, 0)),
        compiler_params=_params("arbitrary", "arbitrary", "arbitrary"),
        name="na_bias_tiles",
    )(by_col)


def _first_max_of_four(vals):
    a, b, c, d = vals
    m = jnp.maximum(jnp.maximum(a, b), jnp.maximum(c, d))
    idx = jnp.where(a == m, 0.0, jnp.where(b == m, 1.0, jnp.where(c == m, 2.0, 3.0)))
    return m, idx


MOE_CHUNK = EXPERTS_PER_GROUP * D_EXPERT


def _out_kernel(xc_ref, xl_ref, mod_ref, oc_ref, oda_ref, ona_ref, gate_ref, wa_ref, wb_ref, wo_ref, g2_ref,
                wrt_ref, br_ref, wg_ref, wu_ref, wd_ref, yc_ref, yl_ref):
    is_ctx = pl.program_id(0) == 0
    oc = oc_ref[0]
    oa = jnp.where(is_ctx, oc[:, :DA_WIDTH], oda_ref[0])
    ob = jnp.where(is_ctx, oc[:, DA_WIDTH:], ona_ref[0])
    ya = _dot(oa, wa_ref[0])
    yb = _dot(ob, wb_ref[0])
    g = gate_ref[0]
    mixed = (g[:, :D_MODEL].astype(F32) * ya + g[:, D_MODEL:].astype(F32) * yb).astype(BF16)
    mod = mod_ref[0]
    x1 = jnp.where(is_ctx, xc_ref[0], xl_ref[0]) + mod[2:3] * _dot(mixed, wo_ref[0])

    ms = jnp.mean(x1 * x1, axis=-1, keepdims=True)
    h2 = x1 * lax.rsqrt(ms + EPS) * g2_ref[0] * (1.0 + mod[4:5]) + mod[3:4]
    h2_hi = h2.astype(BF16)

    h2_lo = (h2 - h2_hi.astype(F32)).astype(BF16)
    wr = wrt_ref[...]
    wr_hi = wr.astype(BF16)
    wr_lo = (wr - wr_hi.astype(F32)).astype(BF16)
    by_hi = _dot_nt(jnp.concatenate([wr_hi, wr_lo], axis=0), h2_hi)
    logits = by_hi[:N_EXPERTS] + (by_hi[N_EXPERTS:] + _dot_nt(wr_hi, h2_lo))
    scores = jax.nn.sigmoid(logits)
    sel = scores + br_ref[...]
    score_rows = [scores[e:e + 1] for e in range(N_EXPERTS)]
    sel_rows = [sel[e:e + 1] for e in range(N_EXPERTS)]

    neg = -jnp.inf
    best = None
    for grp in range(N_GROUPS):
        vals = sel_rows[grp * EXPERTS_PER_GROUP:(grp + 1) * EXPERTS_PER_GROUP]
        m1, i1 = _first_max_of_four(vals)
        rest = [jnp.where(i1 == float(j), neg, v) for j, v in enumerate(vals)]
        m2, i2 = _first_max_of_four(rest)
        cand = (m1 + m2, i1 + float(grp * EXPERTS_PER_GROUP), i2 + float(grp * EXPERTS_PER_GROUP))
        if best is None:
            best = cand
        else:
            upd = cand[0] > best[0]
            best = tuple(jnp.where(upd, n, o) for n, o in zip(cand, best))
    _, e1, e2 = best
    hits1 = [e1 == float(e) for e in range(N_EXPERTS)]
    hits2 = [e2 == float(e) for e in range(N_EXPERTS)]
    s1 = functools.reduce(lambda a, b: a + b, [jnp.where(h, r, 0.0) for h, r in zip(hits1, score_rows)])
    s2 = functools.reduce(lambda a, b: a + b, [jnp.where(h, r, 0.0) for h, r in zip(hits2, score_rows)])
    den = s1 + s2
    w1 = s1 / den
    w2 = s2 / den
    rows = [jnp.where(h1, w1, 0.0) + jnp.where(h2_, w2, 0.0) for h1, h2_ in zip(hits1, hits2)]
    rows.append(jnp.zeros((LANES - N_EXPERTS, rows[0].shape[1]), F32))
    comb = jnp.concatenate(rows, axis=0).T

    acc = jnp.zeros(x1.shape, F32)
    for c in range(N_GROUPS):
        parts = []
        for j in range(EXPERTS_PER_GROUP):
            e = c * EXPERTS_PER_GROUP + j
            g = _dot(h2_hi, wg_ref[0, e])
            u = _dot(h2_hi, wu_ref[0, e])
            parts.append(g * jax.nn.sigmoid(g) * u * comb[:, e:e + 1])
        hs = jnp.concatenate(parts, axis=1).astype(BF16)
        acc = acc + _dot(hs, wd_ref[0, c * MOE_CHUNK:(c + 1) * MOE_CHUNK, :])
    out = x1 + mod[5:6] * acc

    @pl.when(is_ctx)
    def _():
        yc_ref[0] = out

    @pl.when(jnp.logical_not(is_ctx))
    def _():
        yl_ref[0] = out


def _output_stage(l, x_ctx, x_lat, mods, o_ctx, o_da, o_na, gates, wa_b, wb_b, wo_b, g_norm2, w_router_t,
                  b_router_c, wg_all, wu_all, wd_all):
    tm = 256
    nt = SEG_ROWS // tm
    hidden = N_EXPERTS * D_EXPERT
    once = pl.Buffered(1)
    return pl.pallas_call(
        _out_kernel,
        out_shape=(
            jax.ShapeDtypeStruct((1, SEG_ROWS, D_MODEL), F32),
            jax.ShapeDtypeStruct((DEC_BATCH, DEC_SEQ, D_MODEL), F32),
        ),
        grid=(N_SEG, nt),
        in_specs=[
            _ctx_rows_spec(tm, D_MODEL),
            _lat_rows_spec(tm, D_MODEL),
            pl.BlockSpec((1, N_MOD, D_MODEL), lambda s, i: (s, 0, 0)),
            _ctx_rows_spec(tm, D_MODEL),
            _lat_rows_spec(tm, DA_WIDTH),
            _lat_rows_spec(tm, NA_WIDTH),
            pl.BlockSpec((1, tm, 2 * D_MODEL), lambda s, i: (s, i, 0)),
            pl.BlockSpec((1, DA_WIDTH, D_MODEL), lambda s, i: (l, 0, 0)),
            pl.BlockSpec((1, NA_WIDTH, D_MODEL), lambda s, i: (l, 0, 0)),
            pl.BlockSpec((1, D_MODEL, D_MODEL), lambda s, i: (l, 0, 0)),
            pl.BlockSpec((1, 1, D_MODEL), lambda s, i: (l, 0, 0)),
            pl.BlockSpec((N_EXPERTS, D_MODEL), lambda s, i: (0, 0)),
            pl.BlockSpec((N_EXPERTS, 1), lambda s, i: (0, 0)),
            pl.BlockSpec((1, N_EXPERTS, D_MODEL, D_EXPERT), lambda s, i: (l, 0, 0, 0), pipeline_mode=once),
            pl.BlockSpec((1, N_EXPERTS, D_MODEL, D_EXPERT), lambda s, i: (l, 0, 0, 0), pipeline_mode=once),
            pl.BlockSpec((1, hidden, D_MODEL), lambda s, i: (l, 0, 0), pipeline_mode=once),
        ],
        out_specs=(_ctx_rows_spec(tm, D_MODEL), _lat_rows_spec(tm, D_MODEL)),
        compiler_params=_params("arbitrary", "arbitrary"),
        name="output_experts",
    )(x_ctx, x_lat, mods, o_ctx, o_da, o_na, gates, wa_b, wb_b, wo_b, g_norm2, w_router_t, b_router_c,
      wg_all, wu_all, wd_all)


def _rope_tables():
    t = np.arange(DEC_SEQ)
    pos = np.stack([t // GRID_W, t % GRID_W], axis=-1).astype(np.float32)
    inv_freq = (ROPE_BASE ** (-np.arange(ROPE_PAIRS, dtype=np.float32) / ROPE_PAIRS)).astype(np.float32)
    ang = pos[:, :, None] * inv_freq
    lane = np.arange(LANES)
    axis = (lane % HEAD_DIM) // (2 * ROPE_PAIRS)
    pair = lane % ROPE_PAIRS
    second = ((lane // ROPE_PAIRS) % 2).astype(bool)
    a = ang[:, axis, pair]
    cos = np.cos(a)
    sin = np.sin(a)
    lat = np.concatenate([cos, np.where(second, 0.0, -sin), np.where(second, sin, 0.0)], axis=1)
    ident = np.concatenate([np.ones_like(cos), np.zeros_like(cos), np.zeros_like(cos)], axis=1)
    return jnp.asarray(np.stack([ident, lat]).astype(np.float32))


def _head_block_diag():
    r = np.arange(CHUNK // 2) // HEAD_DIM
    return jnp.asarray((r[:, None] == r[None, :]).astype(np.float32), dtype=BF16)


def kernel(x_prompt, x_sample, cache_da_k, cache_da_v, cache_na_k, cache_na_v, c, c_ctx, w_mod, b_mod, g_norm1,
           g_norm2, w_in, g_q_da, g_k_da, g_q_na, g_k_na, lam_q1, lam_k1, lam_q2, lam_k2, g_subln, rpb, w_br_a,
           w_br_b, w_out, w_router, b_router, w_gate, w_up, w_down):
    L = DEPTH
    x_ctx = x_prompt.reshape(1, SEG_ROWS, D_MODEL)
    x_lat = x_sample
    cond8 = jnp.concatenate([c_ctx[None], c, jnp.zeros((8 - N_SEG, D_MODEL), F32)], axis=0)
    w_in_b = w_in.astype(BF16)
    wa_b = w_br_a.astype(BF16)
    wb_b = w_br_b.astype(BF16)
    wo_b = w_out.astype(BF16)
    hidden = N_EXPERTS * D_EXPERT
    wg_all = w_gate.astype(BF16)
    wu_all = w_up.astype(BF16)
    wd_all = w_down.astype(BF16).reshape(L, hidden, D_MODEL)
    w_router_t = w_router.T
    b_router_c = b_router.reshape(N_EXPERTS, 1)
    head_gains = jnp.stack([jnp.tile(g, (1, CHUNK // HEAD_DIM)) for g in (g_q_da, g_k_da, g_q_na, g_k_na)],
                           axis=1).reshape(L, 4, CHUNK)
    g_norm1 = g_norm1.reshape(L, 1, D_MODEL)
    g_norm2 = g_norm2.reshape(L, 1, D_MODEL)
    g_subln = g_subln.reshape(L, 1, LANES)
    pad64 = lambda t: jnp.pad(t, ((0, 0), (0, LANES - HEAD_DIM)))
    lam_inits = jnp.asarray([0.8 - 0.6 * math.exp(-0.3 * l) for l in range(L)], F32)
    lam_tab = jnp.stack([pad64(lam_q1), pad64(lam_k1), pad64(lam_q2), pad64(lam_k2),
                         jnp.broadcast_to(lam_inits[:, None], (L, LANES)),
                         jnp.zeros((L, LANES), F32), jnp.zeros((L, LANES), F32), jnp.zeros((L, LANES), F32)],
                        axis=1)
    cache_da_k_t = jnp.swapaxes(cache_da_k, -1, -2)
    cache_na_k_t = jnp.swapaxes(cache_na_k, -1, -2)
    cache_na_v_t = jnp.swapaxes(cache_na_v, -1, -2)
    na_bias = _na_bias_tables(rpb)
    rope_tab = _rope_tables()
    bd = _head_block_diag()

    mods_all = _modulation(cond8, w_mod, b_mod)[:, :N_SEG].reshape(L, N_SEG, N_MOD, D_MODEL)

    caches = None
    for l in range(L):
        mods = mods_all[l]
        qkv, gates, *caches = _in_projection(l, x_ctx, x_lat, mods, g_norm1, w_in_b, head_gains, bd, rope_tab,
                                             caches)
        o_ctx = _attention_ctx(l, qkv, lam_tab, g_subln).reshape(1, SEG_ROWS, D_MODEL)
        o_da = _attention_da(l, qkv, cache_da_k_t, cache_da_v, lam_tab, g_subln)
        o_na = _attention_na(l, qkv, cache_na_k_t, cache_na_v_t, na_bias)
        x_ctx, x_lat = _output_stage(l, x_ctx, x_lat, mods, o_ctx, o_da, o_na, gates, wa_b, wb_b, wo_b, g_norm2,
                                     w_router_t, b_router_c, wg_all, wu_all, wd_all)

    y_prompt = x_ctx.reshape(BATCH, SEQ, D_MODEL)
    y_sample = x_lat
    da_k_t, new_da_v, na_k_t, na_v_t = caches
    return (y_prompt, y_sample, jnp.swapaxes(da_k_t, -1, -2), new_da_v, jnp.swapaxes(na_k_t, -1, -2),
            jnp.swapaxes(na_v_t, -1, -2))
```

```python
import functools
import math

import numpy as np
import jax
import jax.numpy as jnp
from jax import lax
from jax.experimental import pallas as pl
from jax.experimental.pallas import tpu as pltpu

D_MODEL = 1024
BATCH = 16
SEQ = 256
DEPTH = 4
DEC_BATCH = 4
DEC_SEQ = 4096
PAST_LEN = 512
GRID_W = 64
GRID_H = DEC_SEQ // GRID_W
HEAD_DIM = 64
DA_HEADS = 4
DA_WIDTH = 512
NA_HEADS = 8
NA_WIDTH = 512
NA_ROWS = 8
NA_COLS = 16
IN_COLS = 3 * DA_WIDTH + 3 * NA_WIDTH + 2 * D_MODEL
ROPE_BASE = 10000.0
ROPE_PAIRS = HEAD_DIM // 4
N_EXPERTS = 16
N_GROUPS = 4
EXPERTS_PER_GROUP = 4
D_EXPERT = 256
N_MOD = 6
EPS = 1e-6
ATTN_SCALE = HEAD_DIM ** -0.5
Q_PRESCALE = ATTN_SCALE * math.log2(math.e)
LOG2E = math.log2(math.e)

N_SEG = 1 + DEC_BATCH
SEG_ROWS = DEC_SEQ
LANES = 128
CHUNK = 512

VMEM_LIMIT = 56 * 1024 * 1024

F32 = jnp.float32
BF16 = jnp.bfloat16


def _dot(a, b):
    return jnp.dot(a, b, preferred_element_type=F32)


def _dot_nt(a, b):
    return lax.dot_general(a, b, (((1,), (1,)), ((), ())), preferred_element_type=F32)


def _params(*sem):
    return pltpu.CompilerParams(dimension_semantics=sem, vmem_limit_bytes=VMEM_LIMIT)


def _ctx_rows_spec(tm, width):
    nt = SEG_ROWS // tm
    return pl.BlockSpec((1, tm, width), lambda s, i: (0, jnp.where(s == 0, i, nt - 1), 0))


def _lat_rows_spec(tm, width):
    return pl.BlockSpec((1, tm, width), lambda s, i: (jnp.maximum(s - 1, 0), jnp.where(s == 0, 0, i), 0))


def _mod_kernel(cond_ref, w_ref, b_ref, o_ref):
    c = cond_ref[...]
    sc = (c * jax.nn.sigmoid(c)).astype(BF16)
    o_ref[0] = _dot(sc, w_ref[0].astype(BF16)) + b_ref[0]


def _modulation(cond8, w_mod, b_mod):
    tn = 1536
    return pl.pallas_call(
        _mod_kernel,
        out_shape=jax.ShapeDtypeStruct((DEPTH, 8, N_MOD * D_MODEL), F32),
        grid=(DEPTH, N_MOD * D_MODEL // tn),
        in_specs=[
            pl.BlockSpec((8, D_MODEL), lambda l, j: (0, 0)),
            pl.BlockSpec((1, D_MODEL, tn), lambda l, j: (l, 0, j)),
            pl.BlockSpec((1, 1, tn), lambda l, j: (l, 0, j)),
        ],
        out_specs=pl.BlockSpec((1, 8, tn), lambda l, j: (l, 0, j)),
        compiler_params=_params("arbitrary", "arbitrary"),
        name="modulation",
    )(cond8, w_mod, b_mod.reshape(DEPTH, 1, N_MOD * D_MODEL))


def _store_heads_t(dst_ref, b, t, rows):
    tt = t[rows].T
    lead = dst_ref.shape[2:-2]
    for h in range(CHUNK // HEAD_DIM):
        idx = np.unravel_index(h, lead)
        dst_ref[(b, 0) + tuple(int(i) for i in idx)] = tt[h * HEAD_DIM:(h + 1) * HEAD_DIM, :]


def _inproj_kernel(xc_ref, xl_ref, mod_ref, g1_ref, w_ref, hg_ref, bd_ref, rope_ref, *rest):
    qkv_ref, gate_ref, cdk_ref, cdv_ref, cnk_ref, cnv_ref = rest[-6:]
    s = pl.program_id(0)
    x = jnp.where(s == 0, xc_ref[0], xl_ref[0])
    ms = jnp.mean(x * x, axis=-1, keepdims=True)
    y = x * lax.rsqrt(ms + EPS) * g1_ref[0]
    mod = mod_ref[0]
    h = (y * (1.0 + mod[1:2]) + mod[0:1]).astype(BF16)

    rope = rope_ref[0]
    cos4 = jnp.concatenate([rope[:, 0:LANES]] * 4, axis=1)
    sin_up4 = jnp.concatenate([rope[:, LANES:2 * LANES]] * 4, axis=1)
    sin_dn4 = jnp.concatenate([rope[:, 2 * LANES:3 * LANES]] * 4, axis=1)

    def head_norm(acc, row):
        sq = (acc * acc).astype(BF16)
        half = CHUNK // 2
        ssum = jnp.concatenate([_dot(sq[:, :half], bd_ref[...]), _dot(sq[:, half:], bd_ref[...])], axis=1)
        return acc * lax.rsqrt(ssum * (1.0 / HEAD_DIM) + EPS) * hg_ref[0, row:row + 1, :]

    def rope_rot(t):
        return (t * cos4 + pltpu.roll(t, CHUNK - ROPE_PAIRS, 1) * sin_up4
                + pltpu.roll(t, ROPE_PAIRS, 1) * sin_dn4)

    def proj(c):
        return _dot(h, w_ref[0, :, c * CHUNK:(c + 1) * CHUNK])

    is_ctx = s == 0
    batches = [(b, slice(b * SEQ, (b + 1) * SEQ)) for b in range(x.shape[0] // SEQ)]

    q = rope_rot(head_norm(proj(0), 0)) * Q_PRESCALE
    qkv_ref[0, :, 0:CHUNK] = q.astype(BF16)

    k = rope_rot(head_norm(proj(1), 1))
    qkv_ref[0, :, CHUNK:2 * CHUNK] = k.astype(BF16)

    @pl.when(is_ctx)
    def _():
        for b, rows in batches:
            _store_heads_t(cdk_ref, b, k, rows)

    v = proj(2)
    qkv_ref[0, :, 2 * CHUNK:3 * CHUNK] = v.astype(BF16)

    @pl.when(is_ctx)
    def _():
        for b, rows in batches:
            for h in range(DA_HEADS):
                cdv_ref[b, 0, h] = v[rows, h * LANES:(h + 1) * LANES]

    q = head_norm(proj(3), 2) * Q_PRESCALE
    qkv_ref[0, :, 3 * CHUNK:4 * CHUNK] = q.astype(BF16)

    k = head_norm(proj(4), 3)
    qkv_ref[0, :, 4 * CHUNK:5 * CHUNK] = k.astype(BF16)

    @pl.when(is_ctx)
    def _():
        for b, rows in batches:
            _store_heads_t(cnk_ref, b, k, rows)

    v = proj(5)
    qkv_ref[0, :, 5 * CHUNK:6 * CHUNK] = v.astype(BF16)

    @pl.when(is_ctx)
    def _():
        for b, rows in batches:
            _store_heads_t(cnv_ref, b, v, rows)

    for c in range(4):
        g = proj(6 + c)
        gate_ref[0, :, c * CHUNK:(c + 1) * CHUNK] = jax.nn.sigmoid(g).astype(BF16)


CACHE_SHAPES = (
    (BATCH, DEPTH, 2, DA_HEADS, HEAD_DIM, SEQ),
    (BATCH, DEPTH, DA_HEADS, SEQ, 2 * HEAD_DIM),
    (BATCH, DEPTH, NA_HEADS, HEAD_DIM, SEQ),
    (BATCH, DEPTH, NA_HEADS, HEAD_DIM, SEQ),
)


def _in_projection(l, x_ctx, x_lat, mods, g_norm1, w_in_b, head_gains, bd, rope_tab, caches):
    tm = 512
    nt = SEG_ROWS // tm
    nb = tm // SEQ

    def cache_spec(shape):
        block = (nb, 1) + shape[2:]
        zeros = (0,) * (len(shape) - 2)
        return pl.BlockSpec(block, lambda s, i: (jnp.where(s == 0, i, nt - 1), l) + zeros)

    chained = caches is not None
    cache_in = list(caches) if chained else []
    n_in = 8
    return pl.pallas_call(
        _inproj_kernel,
        out_shape=(
            jax.ShapeDtypeStruct((N_SEG, SEG_ROWS, 6 * CHUNK), BF16),
            jax.ShapeDtypeStruct((N_SEG, SEG_ROWS, 2 * D_MODEL), BF16),
        ) + tuple(jax.ShapeDtypeStruct(shape, F32) for shape in CACHE_SHAPES),
        grid=(N_SEG, nt),
        in_specs=[
            _ctx_rows_spec(tm, D_MODEL),
            _lat_rows_spec(tm, D_MODEL),
            pl.BlockSpec((1, N_MOD, D_MODEL), lambda s, i: (s, 0, 0)),
            pl.BlockSpec((1, 1, D_MODEL), lambda s, i: (l, 0, 0)),
            pl.BlockSpec((1, D_MODEL, IN_COLS), lambda s, i: (l, 0, 0), pipeline_mode=pl.Buffered(1)),
            pl.BlockSpec((1, 4, CHUNK), lambda s, i: (l, 0, 0)),
            pl.BlockSpec((CHUNK // 2, CHUNK // 2), lambda s, i: (0, 0)),
            pl.BlockSpec((1, tm, 3 * LANES), lambda s, i: (jnp.minimum(s, 1), i, 0)),
        ] + [pl.BlockSpec(memory_space=pl.ANY) for _ in cache_in],
        out_specs=(
            pl.BlockSpec((1, tm, 6 * CHUNK), lambda s, i: (s, i, 0)),
            pl.BlockSpec((1, tm, 2 * D_MODEL), lambda s, i: (s, i, 0)),
        ) + tuple(cache_spec(shape) for shape in CACHE_SHAPES),
        input_output_aliases={n_in + j: 2 + j for j in range(len(cache_in))},
        compiler_params=_params("arbitrary", "arbitrary"),
        name="in_projection",
    )(x_ctx, x_lat, mods, g_norm1, w_in_b, head_gains, bd, rope_tab, *cache_in)


def _lam_value(lam_ref):
    lp = lam_ref[0]
    s1 = jnp.sum(lp[0:1] * lp[1:2], axis=-1, keepdims=True)
    s2 = jnp.sum(lp[2:3] * lp[3:4], axis=-1, keepdims=True)
    lam_init = lp[4:5, 0:1]
    return jnp.exp(s1) - jnp.exp(s2) + lam_init, 1.0 - lam_init


def _half_masks():
    lane = lax.broadcasted_iota(jnp.int32, (1, LANES), 1)
    return (lane < HEAD_DIM, lane >= HEAD_DIM)


def _pick_head(mask, t):
    return jnp.where(mask, t, jnp.zeros_like(t))


def _sub_layer_norm(o, gsub, one_minus):
    ms = jnp.mean(o * o, axis=-1, keepdims=True)
    return o * lax.rsqrt(ms + EPS) * gsub * one_minus


def _attn_ctx_kernel(qkv_ref, lam_ref, gsub_ref, o_ref):
    lam, one_minus = _lam_value(lam_ref)
    masks = _half_masks()
    qda = qkv_ref[0, :, 0:CHUNK]
    qna = qkv_ref[0, :, 3 * CHUNK:4 * CHUNK]
    gsub = gsub_ref[0]

    def kv(c0):
        off = c0 + CHUNK if c0 < 2 * CHUNK else c0 + 2 * CHUNK
        return qkv_ref[0, :, off:off + LANES]

    def softmax_parts(sc):
        m = jnp.max(sc, axis=-1, keepdims=True)
        e = jnp.exp2(sc - m)
        return e, 1.0 / jnp.sum(e, axis=-1, keepdims=True)

    for h in range(DA_HEADS):
        p, sub = divmod(h, 2)
        q1 = _pick_head(masks[sub], qda[:, p * LANES:(p + 1) * LANES])
        q2 = _pick_head(masks[sub], qda[:, 2 * LANES + p * LANES:2 * LANES + (p + 1) * LANES])
        e1, r1 = softmax_parts(_dot_nt(q1, kv(p * LANES)))
        e2, r2 = softmax_parts(_dot_nt(q2, kv(2 * LANES + p * LANES)))
        a = e1 * r1 - e2 * (lam * r2)
        o = _dot(a.astype(BF16), kv(CHUNK + h * LANES))
        o_ref[:, h * LANES:(h + 1) * LANES] = _sub_layer_norm(o, gsub, one_minus).astype(BF16)

    for p in range(NA_HEADS // 2):
        kp = kv(2 * CHUNK + p * LANES)
        vp = kv(3 * CHUNK + p * LANES)
        both = None
        for sub in range(2):
            qm = _pick_head(masks[sub], qna[:, p * LANES:(p + 1) * LANES])
            e, r = softmax_parts(_dot_nt(qm, kp))
            o = _dot(e.astype(BF16), vp) * r
            both = o if sub == 0 else jnp.where(masks[1], o, both)
        o_ref[:, DA_WIDTH + p * LANES:DA_WIDTH + (p + 1) * LANES] = both.astype(BF16)


def _attention_ctx(l, qkv, lam_tab, g_subln):
    return pl.pallas_call(
        _attn_ctx_kernel,
        out_shape=jax.ShapeDtypeStruct((SEG_ROWS, D_MODEL), BF16),
        grid=(BATCH,),
        in_specs=[
            pl.BlockSpec((1, SEQ, 6 * CHUNK), lambda b: (0, b, 0)),
            pl.BlockSpec((1, 8, LANES), lambda b: (l, 0, 0)),
            pl.BlockSpec((1, 1, LANES), lambda b: (l, 0, 0)),
        ],
        out_specs=pl.BlockSpec((SEQ, D_MODEL), lambda b: (b, 0)),
        compiler_params=_params("arbitrary"),
        name="attention_ctx",
    )(qkv, lam_tab, g_subln)


DA_KEYS = DEC_SEQ + PAST_LEN
DA_KCHUNK = 1024
DA_TQ = 256


def _pair_heads(head_ref, lead, first):
    a = head_ref[lead + (first,)]
    b = head_ref[lead + (first + 1,)]
    return jnp.concatenate([a, b], axis=0).T.astype(BF16)


def _attn_da_kernel(q_ref, k_ref, v_ref, kc_ref, vc_ref, lam_ref, gsub_ref, o_ref, vt_ref, kcs_ref, e1_ref, e2_ref):
    chunks = [slice(r, r + DA_KCHUNK) for r in range(0, DEC_SEQ, DA_KCHUNK)] + [slice(DEC_SEQ, DA_KEYS)]
    n_chunks = len(chunks)

    @pl.when(pl.program_id(1) == 0)
    def _():
        for r in range(0, DEC_SEQ, PAST_LEN):
            rows = slice(r, r + PAST_LEN)
            vt_ref[:, rows] = v_ref[0, rows, :].astype(F32).T.astype(BF16)
        for h in range(DA_HEADS):
            vt_ref[h * LANES:(h + 1) * LANES, DEC_SEQ:DA_KEYS] = vc_ref[0, 0, h].T.astype(BF16)
        for m in range(2):
            for p in range(DA_HEADS // 2):
                c0 = m * 2 * LANES + p * LANES
                kcs_ref[:, c0:c0 + LANES] = _pair_heads(kc_ref, (0, 0, m), 2 * p)

    lam, one_minus = _lam_value(lam_ref)
    masks = _half_masks()
    q = q_ref[0]
    gsub = gsub_ref[0]

    def key_chunk(c, cols):
        if chunks[c].start < DEC_SEQ:
            return k_ref[0, chunks[c], cols]
        return kcs_ref[:, cols]

    def score_map(qm, cols, e_ref, weight):
        maxes, sums = [], []
        for c in range(n_chunks):
            s = _dot_nt(key_chunk(c, cols), qm)
            mc = jnp.max(s, axis=0, keepdims=True)
            e = jnp.exp2(s - mc)
            sums.append(jnp.sum(e, axis=0, keepdims=True))
            maxes.append(mc)
            e_ref[chunks[c], :] = e.astype(BF16)
        m = functools.reduce(jnp.maximum, maxes)
        facs = [jnp.exp2(mc - m) for mc in maxes]
        tot = functools.reduce(lambda a, b: a + b, [lc * fc for lc, fc in zip(sums, facs)])
        w = weight / tot
        return [(fc * w).astype(BF16) for fc in facs]

    for h in range(DA_HEADS):
        p, sub = divmod(h, 2)
        c1 = slice(p * LANES, (p + 1) * LANES)
        c2 = slice(2 * LANES + p * LANES, 2 * LANES + (p + 1) * LANES)
        f1 = score_map(_pick_head(masks[sub], q[:, c1]), c1, e1_ref, 1.0)
        f2 = score_map(_pick_head(masks[sub], q[:, c2]), c2, e2_ref, lam)
        o_t = jnp.zeros((LANES, DA_TQ), F32)
        for c, rows in enumerate(chunks):
            a = e1_ref[rows, :] * f1[c] - e2_ref[rows, :] * f2[c]
            o_t = o_t + _dot(vt_ref[h * LANES:(h + 1) * LANES, rows], a)
        o_ref[0, :, h * LANES:(h + 1) * LANES] = _sub_layer_norm(o_t.T, gsub, one_minus).astype(BF16)


def _attention_da(l, qkv, ctx_k, ctx_v, lam_tab, g_subln):
    tq = DA_TQ
    return pl.pallas_call(
        _attn_da_kernel,
        out_shape=jax.ShapeDtypeStruct((DEC_BATCH, DEC_SEQ, DA_WIDTH), BF16),
        grid=(DEC_BATCH, DEC_SEQ // tq),
        in_specs=[
            pl.BlockSpec((1, tq, CHUNK), lambda b, i: (b + 1, i, 0)),
            pl.BlockSpec((1, DEC_SEQ, CHUNK), lambda b, i: (b + 1, 0, 1), pipeline_mode=pl.Buffered(1)),
            pl.BlockSpec((1, DEC_SEQ, CHUNK), lambda b, i: (b + 1, 0, 2), pipeline_mode=pl.Buffered(1)),
            pl.BlockSpec((1, 1, 2, DA_HEADS, HEAD_DIM, PAST_LEN), lambda b, i: (b, l, 0, 0, 0, 0)),
            pl.BlockSpec((1, 1, DA_HEADS, PAST_LEN, 2 * HEAD_DIM), lambda b, i: (b, l, 0, 0, 0)),
            pl.BlockSpec((1, 8, LANES), lambda b, i: (l, 0, 0)),
            pl.BlockSpec((1, 1, LANES), lambda b, i: (l, 0, 0)),
        ],
        out_specs=pl.BlockSpec((1, tq, DA_WIDTH), lambda b, i: (b, i, 0)),
        scratch_shapes=[
            pltpu.VMEM((DA_WIDTH, DA_KEYS), BF16),
            pltpu.VMEM((PAST_LEN, DA_WIDTH), BF16),
            pltpu.VMEM((DA_KEYS, tq), BF16),
            pltpu.VMEM((DA_KEYS, tq), BF16),
        ],
        compiler_params=_params("arbitrary", "arbitrary"),
        name="attention_da",
    )(qkv, qkv, qkv, ctx_k, ctx_v, lam_tab, g_subln)


NA_QROWS = 8
NA_KROWS = 16


def _attn_na_kernel(q_ref, k_ref, v_ref, kc_ref, vc_ref, bias_ref, o_ref, kcs_ref, vcs_ref):
    i = pl.program_id(1)

    @pl.when(i == 0)
    def _():
        for p in range(NA_HEADS // 2):
            kcs_ref[:, p * LANES:(p + 1) * LANES] = _pair_heads(kc_ref, (0, 0), 2 * p)
            vcs_ref[:, p * LANES:(p + 1) * LANES] = _pair_heads(vc_ref, (0, 0), 2 * p)

    masks = _half_masks()
    q = q_ref[0]
    row0 = jnp.clip(i * NA_QROWS - NA_ROWS // 2, 0, GRID_H - NA_KROWS)
    start = pl.multiple_of(row0 * GRID_W, 256)
    nk = NA_KROWS * GRID_W

    for p in range(NA_HEADS // 2):
        c0 = p * LANES
        kw = k_ref[0, pl.ds(start, nk), c0:c0 + LANES]
        vw = v_ref[0, pl.ds(start, nk), c0:c0 + LANES]
        kc = kcs_ref[:, c0:c0 + LANES]
        vc = vcs_ref[:, c0:c0 + LANES]
        both = None
        for sub in range(2):
            qm = _pick_head(masks[sub], q[:, c0:c0 + LANES])
            s_loc = _dot_nt(qm, kw) + bias_ref[0, 0, 2 * p + sub].astype(F32)
            s_ctx = _dot_nt(qm, kc)
            m = jnp.maximum(jnp.max(s_loc, axis=-1, keepdims=True), jnp.max(s_ctx, axis=-1, keepdims=True))
            e_loc = jnp.exp2(s_loc - m)
            e_ctx = jnp.exp2(s_ctx - m)
            tot = jnp.sum(e_loc, axis=-1, keepdims=True) + jnp.sum(e_ctx, axis=-1, keepdims=True)
            o = (_dot(e_loc.astype(BF16), vw) + _dot(e_ctx.astype(BF16), vc)) * (1.0 / tot)
            both = o if sub == 0 else jnp.where(masks[1], o, both)
        o_ref[0, :, c0:c0 + LANES] = both.astype(BF16)


def _attention_na(l, qkv, ctx_k, ctx_v, bias):
    tq = NA_QROWS * GRID_W
    n_i = DEC_SEQ // tq

    def bias_idx(b, i):
        return (l, jnp.where(i == 0, 0, jnp.where(i == n_i - 1, 2, 1)), 0, 0, 0)

    return pl.pallas_call(
        _attn_na_kernel,
        out_shape=jax.ShapeDtypeStruct((DEC_BATCH, DEC_SEQ, NA_WIDTH), BF16),
        grid=(DEC_BATCH, n_i),
        in_specs=[
            pl.BlockSpec((1, tq, CHUNK), lambda b, i: (b + 1, i, 3)),
            pl.BlockSpec((1, DEC_SEQ, CHUNK), lambda b, i: (b + 1, 0, 4), pipeline_mode=pl.Buffered(1)),
            pl.BlockSpec((1, DEC_SEQ, CHUNK), lambda b, i: (b + 1, 0, 5), pipeline_mode=pl.Buffered(1)),
            pl.BlockSpec((1, 1, NA_HEADS, HEAD_DIM, PAST_LEN), lambda b, i: (b, l, 0, 0, 0)),
            pl.BlockSpec((1, 1, NA_HEADS, HEAD_DIM, PAST_LEN), lambda b, i: (b, l, 0, 0, 0)),
            pl.BlockSpec((1, 1, NA_HEADS, tq, NA_KROWS * GRID_W), bias_idx),
        ],
        out_specs=pl.BlockSpec((1, tq, NA_WIDTH), lambda b, i: (b, i, 0)),
        scratch_shapes=[
            pltpu.VMEM((PAST_LEN, NA_WIDTH), BF16),
            pltpu.VMEM((PAST_LEN, NA_WIDTH), BF16),
        ],
        compiler_params=_params("arbitrary", "arbitrary"),
        name="attention_na",
    )(qkv, qkv, qkv, ctx_k, ctx_v, bias)


def _na_bias_tables(rpb):
    qr = np.arange(NA_QROWS)
    qc = np.arange(GRID_W)
    kr = np.arange(NA_KROWS)
    kc = np.arange(GRID_W)
    c0 = np.clip(qc - NA_COLS // 2, 0, GRID_W - NA_COLS)
    col_ok = (kc[None, :] >= c0[:, None]) & (kc[None, :] < c0[:, None] + NA_COLS)
    cpad = GRID_W - NA_COLS
    rpb_c = jnp.pad(rpb, ((0, 0), (0, 0), (0, 0), (cpad, cpad)))
    by_col = jnp.stack([rpb_c[..., GRID_W - 1 - c:2 * GRID_W - 1 - c] for c in range(GRID_W)], axis=-2)
    by_col = jnp.where(col_ok, by_col * LOG2E, -jnp.inf)

    plans = []
    for i in (0, 1, GRID_H // NA_QROWS - 1):
        rows_q = NA_QROWS * i + qr
        row0_k = min(max(NA_QROWS * i - NA_ROWS // 2, 0), GRID_H - NA_KROWS)
        rows_k = row0_k + kr
        r0 = np.clip(rows_q - NA_ROWS // 2, 0, GRID_H - NA_ROWS)
        row_ok = (rows_k[None, :] >= r0[:, None]) & (rows_k[None, :] < r0[:, None] + NA_ROWS)
        slab = rows_k[None, :] - rows_q[:, None] + NA_ROWS - 1
        plans.append([[int(slab[r, c]) if row_ok[r, c] else None for c in range(NA_KROWS)]
                      for r in range(NA_QROWS)])

    heads_per_step = NA_HEADS // 2

    def tile_kernel(tab_ref, o_ref):
        kind = pl.program_id(1)
        for which, plan in enumerate(plans):
            @pl.when(kind == which)
            def _(plan=plan):
                for h in range(heads_per_step):
                    for r in range(NA_QROWS):
                        for c in range(NA_KROWS):
                            if plan[r][c] is None:
                                blk = jnp.full((GRID_W, GRID_W), -jnp.inf, BF16)
                            else:
                                blk = tab_ref[0, h, plan[r][c]].astype(BF16)
                            o_ref[0, 0, h, r * GRID_W:(r + 1) * GRID_W, c * GRID_W:(c + 1) * GRID_W] = blk

    n_slab = 2 * NA_ROWS - 1
    return pl.pallas_call(
        tile_kernel,
        out_shape=jax.ShapeDtypeStruct((DEPTH, len(plans), NA_HEADS, NA_QROWS * GRID_W, NA_KROWS * GRID_W), BF16),
        grid=(DEPTH, len(plans), NA_HEADS // heads_per_step),
        in_specs=[pl.BlockSpec((1, heads_per_step, n_slab, GRID_W, GRID_W), lambda l, t, h: (l, h, 0, 0, 0))],
        out_specs=pl.BlockSpec((1, 1, heads_per_step, NA_QROWS * GRID_W, NA_KROWS * GRID_W),
                               lambda l, t, h: (l, t, h, 0, 0)),
        compiler_params=_params("arbitrary", "arbitrary", "arbitrary"),
        name="na_bias_tiles",
    )(by_col)


def _first_max_of_four(vals):
    a, b, c, d = vals
    m = jnp.maximum(jnp.maximum(a, b), jnp.maximum(c, d))
    idx = jnp.where(a == m, 0.0, jnp.where(b == m, 1.0, jnp.where(c == m, 2.0, 3.0)))
    return m, idx


MOE_CHUNK = EXPERTS_PER_GROUP * D_EXPERT


def _out_kernel(xc_ref, xl_ref, mod_ref, oc_ref, oda_ref, ona_ref, gate_ref, wa_ref, wb_ref, wo_ref, g2_ref,
                wrt_ref, br_ref, wg_ref, wu_ref, wd_ref, yc_ref, yl_ref):
    is_ctx = pl.program_id(0) == 0
    oc = oc_ref[0]
    oa = jnp.where(is_ctx, oc[:, :DA_WIDTH], oda_ref[0])
    ob = jnp.where(is_ctx, oc[:, DA_WIDTH:], ona_ref[0])
    ya = _dot(oa, wa_ref[0])
    yb = _dot(ob, wb_ref[0])
    g = gate_ref[0]
    mixed = (g[:, :D_MODEL].astype(F32) * ya + g[:, D_MODEL:].astype(F32) * yb).astype(BF16)
    mod = mod_ref[0]
    x1 = jnp.where(is_ctx, xc_ref[0], xl_ref[0]) + mod[2:3] * _dot(mixed, wo_ref[0])

    ms = jnp.mean(x1 * x1, axis=-1, keepdims=True)
    h2 = x1 * lax.rsqrt(ms + EPS) * g2_ref[0] * (1.0 + mod[4:5]) + mod[3:4]
    h2_hi = h2.astype(BF16)

    h2_lo = (h2 - h2_hi.astype(F32)).astype(BF16)
    wr = wrt_ref[...]
    wr_hi = wr.astype(BF16)
    wr_lo = (wr - wr_hi.astype(F32)).astype(BF16)
    by_hi = _dot_nt(jnp.concatenate([wr_hi, wr_lo], axis=0), h2_hi)
    logits = by_hi[:N_EXPERTS] + (by_hi[N_EXPERTS:] + _dot_nt(wr_hi, h2_lo))
    scores = jax.nn.sigmoid(logits)
    sel = scores + br_ref[...]
    score_rows = [scores[e:e + 1] for e in range(N_EXPERTS)]
    sel_rows = [sel[e:e + 1] for e in range(N_EXPERTS)]

    neg = -jnp.inf
    best = None
    for grp in range(N_GROUPS):
        vals = sel_rows[grp * EXPERTS_PER_GROUP:(grp + 1) * EXPERTS_PER_GROUP]
        m1, i1 = _first_max_of_four(vals)
        rest = [jnp.where(i1 == float(j), neg, v) for j, v in enumerate(vals)]
        m2, i2 = _first_max_of_four(rest)
        cand = (m1 + m2, i1 + float(grp * EXPERTS_PER_GROUP), i2 + float(grp * EXPERTS_PER_GROUP))
        if best is None:
            best = cand
        else:
            upd = cand[0] > best[0]
            best = tuple(jnp.where(upd, n, o) for n, o in zip(cand, best))
    _, e1, e2 = best
    hits1 = [e1 == float(e) for e in range(N_EXPERTS)]
    hits2 = [e2 == float(e) for e in range(N_EXPERTS)]
    s1 = functools.reduce(lambda a, b: a + b, [jnp.where(h, r, 0.0) for h, r in zip(hits1, score_rows)])
    s2 = functools.reduce(lambda a, b: a + b, [jnp.where(h, r, 0.0) for h, r in zip(hits2, score_rows)])
    den = s1 + s2
    w1 = s1 / den
    w2 = s2 / den
    rows = [jnp.where(h1, w1, 0.0) + jnp.where(h2_, w2, 0.0) for h1, h2_ in zip(hits1, hits2)]
    rows.append(jnp.zeros((LANES - N_EXPERTS, rows[0].shape[1]), F32))
    comb = jnp.concatenate(rows, axis=0).T

    acc = jnp.zeros(x1.shape, F32)
    for c in range(N_GROUPS):
        parts = []
        for j in range(EXPERTS_PER_GROUP):
            e = c * EXPERTS_PER_GROUP + j
            g = _dot(h2_hi, wg_ref[0, e])
            u = _dot(h2_hi, wu_ref[0, e])
            parts.append(g * jax.nn.sigmoid(g) * u * comb[:, e:e + 1])
        hs = jnp.concatenate(parts, axis=1).astype(BF16)
        acc = acc + _dot(hs, wd_ref[0, c * MOE_CHUNK:(c + 1) * MOE_CHUNK, :])
    out = x1 + mod[5:6] * acc

    @pl.when(is_ctx)
    def _():
        yc_ref[0] = out

    @pl.when(jnp.logical_not(is_ctx))
    def _():
        yl_ref[0] = out


def _output_stage(l, x_ctx, x_lat, mods, o_ctx, o_da, o_na, gates, wa_b, wb_b, wo_b, g_norm2, w_router_t,
                  b_router_c, wg_all, wu_all, wd_all):
    tm = 256
    nt = SEG_ROWS // tm
    hidden = N_EXPERTS * D_EXPERT
    once = pl.Buffered(1)
    return pl.pallas_call(
        _out_kernel,
        out_shape=(
            jax.ShapeDtypeStruct((1, SEG_ROWS, D_MODEL), F32),
            jax.ShapeDtypeStruct((DEC_BATCH, DEC_SEQ, D_MODEL), F32),
        ),
        grid=(N_SEG, nt),
        in_specs=[
            _ctx_rows_spec(tm, D_MODEL),
            _lat_rows_spec(tm, D_MODEL),
            pl.BlockSpec((1, N_MOD, D_MODEL), lambda s, i: (s, 0, 0)),
            _ctx_rows_spec(tm, D_MODEL),
            _lat_rows_spec(tm, DA_WIDTH),
            _lat_rows_spec(tm, NA_WIDTH),
            pl.BlockSpec((1, tm, 2 * D_MODEL), lambda s, i: (s, i, 0)),
            pl.BlockSpec((1, DA_WIDTH, D_MODEL), lambda s, i: (l, 0, 0)),
            pl.BlockSpec((1, NA_WIDTH, D_MODEL), lambda s, i: (l, 0, 0)),
            pl.BlockSpec((1, D_MODEL, D_MODEL), lambda s, i: (l, 0, 0)),
            pl.BlockSpec((1, 1, D_MODEL), lambda s, i: (l, 0, 0)),
            pl.BlockSpec((N_EXPERTS, D_MODEL), lambda s, i: (0, 0)),
            pl.BlockSpec((N_EXPERTS, 1), lambda s, i: (0, 0)),
            pl.BlockSpec((1, N_EXPERTS, D_MODEL, D_EXPERT), lambda s, i: (l, 0, 0, 0), pipeline_mode=once),
            pl.BlockSpec((1, N_EXPERTS, D_MODEL, D_EXPERT), lambda s, i: (l, 0, 0, 0), pipeline_mode=once),
            pl.BlockSpec((1, hidden, D_MODEL), lambda s, i: (l, 0, 0), pipeline_mode=once),
        ],
        out_specs=(_ctx_rows_spec(tm, D_MODEL), _lat_rows_spec(tm, D_MODEL)),
        compiler_params=_params("arbitrary", "arbitrary"),
        name="output_experts",
    )(x_ctx, x_lat, mods, o_ctx, o_da, o_na, gates, wa_b, wb_b, wo_b, g_norm2, w_router_t, b_router_c,
      wg_all, wu_all, wd_all)


def _rope_tables():
    t = np.arange(DEC_SEQ)
    pos = np.stack([t // GRID_W, t % GRID_W], axis=-1).astype(np.float32)
    inv_freq = (ROPE_BASE ** (-np.arange(ROPE_PAIRS, dtype=np.float32) / ROPE_PAIRS)).astype(np.float32)
    ang = pos[:, :, None] * inv_freq
    lane = np.arange(LANES)
    axis = (lane % HEAD_DIM) // (2 * ROPE_PAIRS)
    pair = lane % ROPE_PAIRS
    second = ((lane // ROPE_PAIRS) % 2).astype(bool)
    a = ang[:, axis, pair]
    cos = np.cos(a)
    sin = np.sin(a)
    lat = np.concatenate([cos, np.where(second, 0.0, -sin), np.where(second, sin, 0.0)], axis=1)
    ident = np.concatenate([np.ones_like(cos), np.zeros_like(cos), np.zeros_like(cos)], axis=1)
    return jnp.asarray(np.stack([ident, lat]).astype(np.float32))


def _head_block_diag():
    r = np.arange(CHUNK // 2) // HEAD_DIM
    return jnp.asarray((r[:, None] == r[None, :]).astype(np.float32), dtype=BF16)


def kernel(x_prompt, x_sample, cache_da_k, cache_da_v, cache_na_k, cache_na_v, c, c_ctx, w_mod, b_mod, g_norm1,
           g_norm2, w_in, g_q_da, g_k_da, g_q_na, g_k_na, lam_q1, lam_k1, lam_q2, lam_k2, g_subln, rpb, w_br_a,
           w_br_b, w_out, w_router, b_router, w_gate, w_up, w_down):
    L = DEPTH
    x_ctx = x_prompt.reshape(1, SEG_ROWS, D_MODEL)
    x_lat = x_sample
    cond8 = jnp.concatenate([c_ctx[None], c, jnp.zeros((8 - N_SEG, D_MODEL), F32)], axis=0)
    w_in_b = w_in.astype(BF16)
    wa_b = w_br_a.astype(BF16)
    wb_b = w_br_b.astype(BF16)
    wo_b = w_out.astype(BF16)
    hidden = N_EXPERTS * D_EXPERT
    wg_all = w_gate.astype(BF16)
    wu_all = w_up.astype(BF16)
    wd_all = w_down.astype(BF16).reshape(L, hidden, D_MODEL)
    w_router_t = w_router.T
    b_router_c = b_router.reshape(N_EXPERTS, 1)
    head_gains = jnp.stack([jnp.tile(g, (1, CHUNK // HEAD_DIM)) for g in (g_q_da, g_k_da, g_q_na, g_k_na)],
                           axis=1).reshape(L, 4, CHUNK)
    g_norm1 = g_norm1.reshape(L, 1, D_MODEL)
    g_norm2 = g_norm2.reshape(L, 1, D_MODEL)
    g_subln = g_subln.reshape(L, 1, LANES)
    pad64 = lambda t: jnp.pad(t, ((0, 0), (0, LANES - HEAD_DIM)))
    lam_inits = jnp.asarray([0.8 - 0.6 * math.exp(-0.3 * l) for l in range(L)], F32)
    lam_tab = jnp.stack([pad64(lam_q1), pad64(lam_k1), pad64(lam_q2), pad64(lam_k2),
                         jnp.broadcast_to(lam_inits[:, None], (L, LANES)),
                         jnp.zeros((L, LANES), F32), jnp.zeros((L, LANES), F32), jnp.zeros((L, LANES), F32)],
                        axis=1)
    cache_da_k_t = jnp.swapaxes(cache_da_k, -1, -2)
    cache_na_k_t = jnp.swapaxes(cache_na_k, -1, -2)
    cache_na_v_t = jnp.swapaxes(cache_na_v, -1, -2)
    na_bias = _na_bias_tables(rpb)
    rope_tab = _rope_tables()
    bd = _head_block_diag()

    mods_all = _modulation(cond8, w_mod, b_mod)[:, :N_SEG].reshape(L, N_SEG, N_MOD, D_MODEL)

    caches = None
    for l in range(L):
        mods = mods_all[l]
        qkv, gates, *caches = _in_projection(l, x_ctx, x_lat, mods, g_norm1, w_in_b, head_gains, bd, rope_tab,
                                             caches)
        o_ctx = _attention_ctx(l, qkv, lam_tab, g_subln).reshape(1, SEG_ROWS, D_MODEL)
        o_da = _attention_da(l, qkv, cache_da_k_t, cache_da_v, lam_tab, g_subln)
        o_na = _attention_na(l, qkv, cache_na_k_t, cache_na_v_t, na_bias)
        x_ctx, x_lat = _output_stage(l, x_ctx, x_lat, mods, o_ctx, o_da, o_na, gates, wa_b, wb_b, wo_b, g_norm2,
                                     w_router_t, b_router_c, wg_all, wu_all, wd_all)

    y_prompt = x_ctx.reshape(BATCH, SEQ, D_MODEL)
    y_sample = x_lat
    da_k_t, new_da_v, na_k_t, na_v_t = caches
    return (y_prompt, y_sample, jnp.swapaxes(da_k_t, -1, -2), new_da_v, jnp.swapaxes(na_k_t, -1, -2),
            jnp.swapaxes(na_v_t, -1, -2))
```

```python
import functools
import math

import numpy as np
import jax
import jax.numpy as jnp
from jax import lax
from jax.experimental import pallas as pl
from jax.experimental.pallas import tpu as pltpu

D_MODEL = 1024
BATCH = 16
SEQ = 256
DEPTH = 4
DEC_BATCH = 4
DEC_SEQ = 4096
PAST_LEN = 512
GRID_W = 64
GRID_H = DEC_SEQ // GRID_W
HEAD_DIM = 64
DA_HEADS = 4
DA_WIDTH = 512
NA_HEADS = 8
NA_WIDTH = 512
NA_ROWS = 8
NA_COLS = 16
IN_COLS = 3 * DA_WIDTH + 3 * NA_WIDTH + 2 * D_MODEL
ROPE_BASE = 10000.0
ROPE_PAIRS = HEAD_DIM // 4
N_EXPERTS = 16
N_GROUPS = 4
EXPERTS_PER_GROUP = 4
D_EXPERT = 256
N_MOD = 6
EPS = 1e-6
ATTN_SCALE = HEAD_DIM ** -0.5
Q_PRESCALE = ATTN_SCALE * math.log2(math.e)
LOG2E = math.log2(math.e)

N_SEG = 1 + DEC_BATCH
SEG_ROWS = DEC_SEQ
LANES = 128
CHUNK = 512

VMEM_LIMIT = 56 * 1024 * 1024

F32 = jnp.float32
BF16 = jnp.bfloat16


def _dot(a, b):
    return jnp.dot(a, b, preferred_element_type=F32)


def _dot_nt(a, b):
    return lax.dot_general(a, b, (((1,), (1,)), ((), ())), preferred_element_type=F32)


def _params(*sem):
    return pltpu.CompilerParams(dimension_semantics=sem, vmem_limit_bytes=VMEM_LIMIT)


def _ctx_rows_spec(tm, width):
    nt = SEG_ROWS // tm
    return pl.BlockSpec((1, tm, width), lambda s, i: (0, jnp.where(s == 0, i, nt - 1), 0))


def _lat_rows_spec(tm, width):
    return pl.BlockSpec((1, tm, width), lambda s, i: (jnp.maximum(s - 1, 0), jnp.where(s == 0, 0, i), 0))


def _mod_kernel(cond_ref, w_ref, b_ref, o_ref):
    c = cond_ref[...]
    sc = (c * jax.nn.sigmoid(c)).astype(BF16)
    o_ref[0] = _dot(sc, w_ref[0].astype(BF16)) + b_ref[0]


def _modulation(cond8, w_mod, b_mod):
    tn = 1536
    return pl.pallas_call(
        _mod_kernel,
        out_shape=jax.ShapeDtypeStruct((DEPTH, 8, N_MOD * D_MODEL), F32),
        grid=(DEPTH, N_MOD * D_MODEL // tn),
        in_specs=[
            pl.BlockSpec((8, D_MODEL), lambda l, j: (0, 0)),
            pl.BlockSpec((1, D_MODEL, tn), lambda l, j: (l, 0, j)),
            pl.BlockSpec((1, 1, tn), lambda l, j: (l, 0, j)),
        ],
        out_specs=pl.BlockSpec((1, 8, tn), lambda l, j: (l, 0, j)),
        compiler_params=_params("arbitrary", "arbitrary"),
        name="modulation",
    )(cond8, w_mod, b_mod.reshape(DEPTH, 1, N_MOD * D_MODEL))


def _store_heads_t(dst_ref, b, t, rows):
    tt = t[rows].T
    lead = dst_ref.shape[2:-2]
    for h in range(CHUNK // HEAD_DIM):
        idx = np.unravel_index(h, lead)
        dst_ref[(b, 0) + tuple(int(i) for i in idx)] = tt[h * HEAD_DIM:(h + 1) * HEAD_DIM, :]


def _inproj_kernel(xc_ref, xl_ref, mod_ref, g1_ref, w_ref, hg_ref, bd_ref, rope_ref, *rest):
    qkv_ref, gate_ref, cdk_ref, cdv_ref, cnk_ref, cnv_ref = rest[-6:]
    s = pl.program_id(0)
    x = jnp.where(s == 0, xc_ref[0], xl_ref[0])
    ms = jnp.mean(x * x, axis=-1, keepdims=True)
    y = x * lax.rsqrt(ms + EPS) * g1_ref[0]
    mod = mod_ref[0]
    h = (y * (1.0 + mod[1:2]) + mod[0:1]).astype(BF16)

    rope = rope_ref[0]
    cos4 = jnp.concatenate([rope[:, 0:LANES]] * 4, axis=1)
    sin_up4 = jnp.concatenate([rope[:, LANES:2 * LANES]] * 4, axis=1)
    sin_dn4 = jnp.concatenate([rope[:, 2 * LANES:3 * LANES]] * 4, axis=1)

    def head_norm(acc, row):
        sq = (acc * acc).astype(BF16)
        half = CHUNK // 2
        ssum = jnp.concatenate([_dot(sq[:, :half], bd_ref[...]), _dot(sq[:, half:], bd_ref[...])], axis=1)
        return acc * lax.rsqrt(ssum * (1.0 / HEAD_DIM) + EPS) * hg_ref[0, row:row + 1, :]

    def rope_rot(t):
        return (t * cos4 + pltpu.roll(t, CHUNK - ROPE_PAIRS, 1) * sin_up4
                + pltpu.roll(t, ROPE_PAIRS, 1) * sin_dn4)

    def proj(c):
        return _dot(h, w_ref[0, :, c * CHUNK:(c + 1) * CHUNK])

    is_ctx = s == 0
    batches = [(b, slice(b * SEQ, (b + 1) * SEQ)) for b in range(x.shape[0] // SEQ)]

    q = rope_rot(head_norm(proj(0), 0)) * Q_PRESCALE
    qkv_ref[0, :, 0:CHUNK] = q.astype(BF16)

    k = rope_rot(head_norm(proj(1), 1))
    qkv_ref[0, :, CHUNK:2 * CHUNK] = k.astype(BF16)

    @pl.when(is_ctx)
    def _():
        for b, rows in batches:
            _store_heads_t(cdk_ref, b, k, rows)

    v = proj(2)
    qkv_ref[0, :, 2 * CHUNK:3 * CHUNK] = v.astype(BF16)

    @pl.when(is_ctx)
    def _():
        for b, rows in batches:
            for h in range(DA_HEADS):
                cdv_ref[b, 0, h] = v[rows, h * LANES:(h + 1) * LANES]

    q = head_norm(proj(3), 2) * Q_PRESCALE
    qkv_ref[0, :, 3 * CHUNK:4 * CHUNK] = q.astype(BF16)

    k = head_norm(proj(4), 3)
    qkv_ref[0, :, 4 * CHUNK:5 * CHUNK] = k.astype(BF16)

    @pl.when(is_ctx)
    def _():
        for b, rows in batches:
            _store_heads_t(cnk_ref, b, k, rows)

    v = proj(5)
    qkv_ref[0, :, 5 * CHUNK:6 * CHUNK] = v.astype(BF16)

    @pl.when(is_ctx)
    def _():
        for b, rows in batches:
            _store_heads_t(cnv_ref, b, v, rows)

    for c in range(4):
        g = proj(6 + c)
        gate_ref[0, :, c * CHUNK:(c + 1) * CHUNK] = jax.nn.sigmoid(g).astype(BF16)


CACHE_SHAPES = (
    (BATCH, DEPTH, 2, DA_HEADS, HEAD_DIM, SEQ),
    (BATCH, DEPTH, DA_HEADS, SEQ, 2 * HEAD_DIM),
    (BATCH, DEPTH, NA_HEADS, HEAD_DIM, SEQ),
    (BATCH, DEPTH, NA_HEADS, HEAD_DIM, SEQ),
)


def _in_projection(l, x_ctx, x_lat, mods, g_norm1, w_in_b, head_gains, bd, rope_tab, caches):
    tm = 512
    nt = SEG_ROWS // tm
    nb = tm // SEQ

    def cache_spec(shape):
        block = (nb, 1) + shape[2:]
        zeros = (0,) * (len(shape) - 2)
        return pl.BlockSpec(block, lambda s, i: (jnp.where(s == 0, i, nt - 1), l) + zeros)

    chained = caches is not None
    cache_in = list(caches) if chained else []
    n_in = 8
    return pl.pallas_call(
        _inproj_kernel,
        out_shape=(
            jax.ShapeDtypeStruct((N_SEG, SEG_ROWS, 6 * CHUNK), BF16),
            jax.ShapeDtypeStruct((N_SEG, SEG_ROWS, 2 * D_MODEL), BF16),
        ) + tuple(jax.ShapeDtypeStruct(shape, F32) for shape in CACHE_SHAPES),
        grid=(N_SEG, nt),
        in_specs=[
            _ctx_rows_spec(tm, D_MODEL),
            _lat_rows_spec(tm, D_MODEL),
            pl.BlockSpec((1, N_MOD, D_MODEL), lambda s, i: (s, 0, 0)),
            pl.BlockSpec((1, 1, D_MODEL), lambda s, i: (l, 0, 0)),
            pl.BlockSpec((1, D_MODEL, IN_COLS), lambda s, i: (l, 0, 0), pipeline_mode=pl.Buffered(1)),
            pl.BlockSpec((1, 4, CHUNK), lambda s, i: (l, 0, 0)),
            pl.BlockSpec((CHUNK // 2, CHUNK // 2), lambda s, i: (0, 0)),
            pl.BlockSpec((1, tm, 3 * LANES), lambda s, i: (jnp.minimum(s, 1), i, 0)),
        ] + [pl.BlockSpec(memory_space=pl.ANY) for _ in cache_in],
        out_specs=(
            pl.BlockSpec((1, tm, 6 * CHUNK), lambda s, i: (s, i, 0)),
            pl.BlockSpec((1, tm, 2 * D_MODEL), lambda s, i: (s, i, 0)),
        ) + tuple(cache_spec(shape) for shape in CACHE_SHAPES),
        input_output_aliases={n_in + j: 2 + j for j in range(len(cache_in))},
        compiler_params=_params("arbitrary", "arbitrary"),
        name="in_projection",
    )(x_ctx, x_lat, mods, g_norm1, w_in_b, head_gains, bd, rope_tab, *cache_in)


def _lam_value(lam_ref):
    lp = lam_ref[0]
    s1 = jnp.sum(lp[0:1] * lp[1:2], axis=-1, keepdims=True)
    s2 = jnp.sum(lp[2:3] * lp[3:4], axis=-1, keepdims=True)
    lam_init = lp[4:5, 0:1]
    return jnp.exp(s1) - jnp.exp(s2) + lam_init, 1.0 - lam_init


def _half_masks():
    lane = lax.broadcasted_iota(jnp.int32, (1, LANES), 1)
    return (lane < HEAD_DIM, lane >= HEAD_DIM)


def _pick_head(mask, t):
    return jnp.where(mask, t, jnp.zeros_like(t))


def _sub_layer_norm(o, gsub, one_minus):
    ms = jnp.mean(o * o, axis=-1, keepdims=True)
    return o * lax.rsqrt(ms + EPS) * gsub * one_minus


def _attn_ctx_kernel(qkv_ref, lam_ref, gsub_ref, o_ref):
    lam, one_minus = _lam_value(lam_ref)
    masks = _half_masks()
    qda = qkv_ref[0, :, 0:CHUNK]
    qna = qkv_ref[0, :, 3 * CHUNK:4 * CHUNK]
    gsub = gsub_ref[0]

    def kv(c0):
        off = c0 + CHUNK if c0 < 2 * CHUNK else c0 + 2 * CHUNK
        return qkv_ref[0, :, off:off + LANES]

    def softmax_parts(sc):
        m = jnp.max(sc, axis=-1, keepdims=True)
        e = jnp.exp2(sc - m)
        return e, 1.0 / jnp.sum(e, axis=-1, keepdims=True)

    for h in range(DA_HEADS):
        p, sub = divmod(h, 2)
        q1 = _pick_head(masks[sub], qda[:, p * LANES:(p + 1) * LANES])
        q2 = _pick_head(masks[sub], qda[:, 2 * LANES + p * LANES:2 * LANES + (p + 1) * LANES])
        e1, r1 = softmax_parts(_dot_nt(q1, kv(p * LANES)))
        e2, r2 = softmax_parts(_dot_nt(q2, kv(2 * LANES + p * LANES)))
        a = e1 * r1 - e2 * (lam * r2)
        o = _dot(a.astype(BF16), kv(CHUNK + h * LANES))
        o_ref[:, h * LANES:(h + 1) * LANES] = _sub_layer_norm(o, gsub, one_minus).astype(BF16)

    for p in range(NA_HEADS // 2):
        kp = kv(2 * CHUNK + p * LANES)
        vp = kv(3 * CHUNK + p * LANES)
        both = None
        for sub in range(2):
            qm = _pick_head(masks[sub], qna[:, p * LANES:(p + 1) * LANES])
            e, r = softmax_parts(_dot_nt(qm, kp))
            o = _dot(e.astype(BF16), vp) * r
            both = o if sub == 0 else jnp.where(masks[1], o, both)
        o_ref[:, DA_WIDTH + p * LANES:DA_WIDTH + (p + 1) * LANES] = both.astype(BF16)


def _attention_ctx(l, qkv, lam_tab, g_subln):
    return pl.pallas_call(
        _attn_ctx_kernel,
        out_shape=jax.ShapeDtypeStruct((SEG_ROWS, D_MODEL), BF16),
        grid=(BATCH,),
        in_specs=[
            pl.BlockSpec((1, SEQ, 6 * CHUNK), lambda b: (0, b, 0)),
            pl.BlockSpec((1, 8, LANES), lambda b: (l, 0, 0)),
            pl.BlockSpec((1, 1, LANES), lambda b: (l, 0, 0)),
        ],
        out_specs=pl.BlockSpec((SEQ, D_MODEL), lambda b: (b, 0)),
        compiler_params=_params("arbitrary"),
        name="attention_ctx",
    )(qkv, lam_tab, g_subln)


DA_KEYS = DEC_SEQ + PAST_LEN
DA_KCHUNK = 1024
DA_TQ = 512


def _pair_heads(head_ref, lead, first):
    a = head_ref[lead + (first,)]
    b = head_ref[lead + (first + 1,)]
    return jnp.concatenate([a, b], axis=0).T.astype(BF16)


def _attn_da_kernel(q_ref, k_ref, v_ref, kc_ref, vc_ref, lam_ref, gsub_ref, o_ref, vt_ref, kcs_ref, e1_ref, e2_ref):
    chunks = [slice(r, r + DA_KCHUNK) for r in range(0, DEC_SEQ, DA_KCHUNK)] + [slice(DEC_SEQ, DA_KEYS)]
    n_chunks = len(chunks)

    @pl.when(pl.program_id(1) == 0)
    def _():
        for r in range(0, DEC_SEQ, PAST_LEN):
            rows = slice(r, r + PAST_LEN)
            vt_ref[:, rows] = v_ref[0, rows, :].astype(F32).T.astype(BF16)
        for h in range(DA_HEADS):
            vt_ref[h * LANES:(h + 1) * LANES, DEC_SEQ:DA_KEYS] = vc_ref[0, 0, h].T.astype(BF16)
        for m in range(2):
            for p in range(DA_HEADS // 2):
                c0 = m * 2 * LANES + p * LANES
                kcs_ref[:, c0:c0 + LANES] = _pair_heads(kc_ref, (0, 0, m), 2 * p)

    lam, one_minus = _lam_value(lam_ref)
    masks = _half_masks()
    q = q_ref[0]
    gsub = gsub_ref[0]

    def key_chunk(c, cols):
        if chunks[c].start < DEC_SEQ:
            return k_ref[0, chunks[c], cols]
        return kcs_ref[:, cols]

    def score_map(qm, cols, e_ref, weight):
        maxes, sums = [], []
        for c in range(n_chunks):
            s = _dot_nt(key_chunk(c, cols), qm)
            mc = jnp.max(s, axis=0, keepdims=True)
            e = jnp.exp2(s - mc)
            sums.append(jnp.sum(e, axis=0, keepdims=True))
            maxes.append(mc)
            e_ref[chunks[c], :] = e.astype(BF16)
        m = functools.reduce(jnp.maximum, maxes)
        facs = [jnp.exp2(mc - m) for mc in maxes]
        tot = functools.reduce(lambda a, b: a + b, [lc * fc for lc, fc in zip(sums, facs)])
        w = weight / tot
        return [(fc * w).astype(BF16) for fc in facs]

    for h in range(DA_HEADS):
        p, sub = divmod(h, 2)
        c1 = slice(p * LANES, (p + 1) * LANES)
        c2 = slice(2 * LANES + p * LANES, 2 * LANES + (p + 1) * LANES)
        f1 = score_map(_pick_head(masks[sub], q[:, c1]), c1, e1_ref, 1.0)
        f2 = score_map(_pick_head(masks[sub], q[:, c2]), c2, e2_ref, lam)
        o_t = jnp.zeros((LANES, DA_TQ), F32)
        for c, rows in enumerate(chunks):
            a = e1_ref[rows, :] * f1[c] - e2_ref[rows, :] * f2[c]
            o_t = o_t + _dot(vt_ref[h * LANES:(h + 1) * LANES, rows], a)
        o_ref[0, :, h * LANES:(h + 1) * LANES] = _sub_layer_norm(o_t.T, gsub, one_minus).astype(BF16)


def _attention_da(l, qkv, ctx_k, ctx_v, lam_tab, g_subln):
    tq = DA_TQ
    return pl.pallas_call(
        _attn_da_kernel,
        out_shape=jax.ShapeDtypeStruct((DEC_BATCH, DEC_SEQ, DA_WIDTH), BF16),
        grid=(DEC_BATCH, DEC_SEQ // tq),
        in_specs=[
            pl.BlockSpec((1, tq, CHUNK), lambda b, i: (b + 1, i, 0)),
            pl.BlockSpec((1, DEC_SEQ, CHUNK), lambda b, i: (b + 1, 0, 1), pipeline_mode=pl.Buffered(1)),
            pl.BlockSpec((1, DEC_SEQ, CHUNK), lambda b, i: (b + 1, 0, 2), pipeline_mode=pl.Buffered(1)),
            pl.BlockSpec((1, 1, 2, DA_HEADS, HEAD_DIM, PAST_LEN), lambda b, i: (b, l, 0, 0, 0, 0)),
            pl.BlockSpec((1, 1, DA_HEADS, PAST_LEN, 2 * HEAD_DIM), lambda b, i: (b, l, 0, 0, 0)),
            pl.BlockSpec((1, 8, LANES), lambda b, i: (l, 0, 0)),
            pl.BlockSpec((1, 1, LANES), lambda b, i: (l, 0, 0)),
        ],
        out_specs=pl.BlockSpec((1, tq, DA_WIDTH), lambda b, i: (b, i, 0)),
        scratch_shapes=[
            pltpu.VMEM((DA_WIDTH, DA_KEYS), BF16),
            pltpu.VMEM((PAST_LEN, DA_WIDTH), BF16),
            pltpu.VMEM((DA_KEYS, tq), BF16),
            pltpu.VMEM((DA_KEYS, tq), BF16),
        ],
        compiler_params=_params("arbitrary", "arbitrary"),
        name="attention_da",
    )(qkv, qkv, qkv, ctx_k, ctx_v, lam_tab, g_subln)


NA_QROWS = 8
NA_KROWS = 16


def _attn_na_kernel(q_ref, k_ref, v_ref, kc_ref, vc_ref, bias_ref, o_ref, kcs_ref, vcs_ref):
    i = pl.program_id(1)

    @pl.when(i == 0)
    def _():
        for p in range(NA_HEADS // 2):
            kcs_ref[:, p * LANES:(p + 1) * LANES] = _pair_heads(kc_ref, (0, 0), 2 * p)
            vcs_ref[:, p * LANES:(p + 1) * LANES] = _pair_heads(vc_ref, (0, 0), 2 * p)

    masks = _half_masks()
    q = q_ref[0]
    row0 = jnp.clip(i * NA_QROWS - NA_ROWS // 2, 0, GRID_H - NA_KROWS)
    start = pl.multiple_of(row0 * GRID_W, 256)
    nk = NA_KROWS * GRID_W

    for p in range(NA_HEADS // 2):
        c0 = p * LANES
        kw = k_ref[0, pl.ds(start, nk), c0:c0 + LANES]
        vw = v_ref[0, pl.ds(start, nk), c0:c0 + LANES]
        kc = kcs_ref[:, c0:c0 + LANES]
        vc = vcs_ref[:, c0:c0 + LANES]
        both = None
        for sub in range(2):
            qm = _pick_head(masks[sub], q[:, c0:c0 + LANES])
            s_loc = _dot_nt(qm, kw) + bias_ref[0, 0, 2 * p + sub].astype(F32)
            s_ctx = _dot_nt(qm, kc)
            m = jnp.maximum(jnp.max(s_loc, axis=-1, keepdims=True), jnp.max(s_ctx, axis=-1, keepdims=True))
            e_loc = jnp.exp2(s_loc - m)
            e_ctx = jnp.exp2(s_ctx - m)
            tot = jnp.sum(e_loc, axis=-1, keepdims=True) + jnp.sum(e_ctx, axis=-1, keepdims=True)
            o = (_dot(e_loc.astype(BF16), vw) + _dot(e_ctx.astype(BF16), vc)) * (1.0 / tot)
            both = o if sub == 0 else jnp.where(masks[1], o, both)
        o_ref[0, :, c0:c0 + LANES] = both.astype(BF16)


def _attention_na(l, qkv, ctx_k, ctx_v, bias):
    tq = NA_QROWS * GRID_W
    n_i = DEC_SEQ // tq

    def bias_idx(b, i):
        return (l, jnp.where(i == 0, 0, jnp.where(i == n_i - 1, 2, 1)), 0, 0, 0)

    return pl.pallas_call(
        _attn_na_kernel,
        out_shape=jax.ShapeDtypeStruct((DEC_BATCH, DEC_SEQ, NA_WIDTH), BF16),
        grid=(DEC_BATCH, n_i),
        in_specs=[
            pl.BlockSpec((1, tq, CHUNK), lambda b, i: (b + 1, i, 3)),
            pl.BlockSpec((1, DEC_SEQ, CHUNK), lambda b, i: (b + 1, 0, 4), pipeline_mode=pl.Buffered(1)),
            pl.BlockSpec((1, DEC_SEQ, CHUNK), lambda b, i: (b + 1, 0, 5), pipeline_mode=pl.Buffered(1)),
            pl.BlockSpec((1, 1, NA_HEADS, HEAD_DIM, PAST_LEN), lambda b, i: (b, l, 0, 0, 0)),
            pl.BlockSpec((1, 1, NA_HEADS, HEAD_DIM, PAST_LEN), lambda b, i: (b, l, 0, 0, 0)),
            pl.BlockSpec((1, 1, NA_HEADS, tq, NA_KROWS * GRID_W), bias_idx),
        ],
        out_specs=pl.BlockSpec((1, tq, NA_WIDTH), lambda b, i: (b, i, 0)),
        scratch_shapes=[
            pltpu.VMEM((PAST_LEN, NA_WIDTH), BF16),
            pltpu.VMEM((PAST_LEN, NA_WIDTH), BF16),
        ],
        compiler_params=_params("arbitrary", "arbitrary"),
        name="attention_na",
    )(qkv, qkv, qkv, ctx_k, ctx_v, bias)


def _na_bias_tables(rpb):
    qr = np.arange(NA_QROWS)
    qc = np.arange(GRID_W)
    kr = np.arange(NA_KROWS)
    kc = np.arange(GRID_W)
    c0 = np.clip(qc - NA_COLS // 2, 0, GRID_W - NA_COLS)
    col_ok = (kc[None, :] >= c0[:, None]) & (kc[None, :] < c0[:, None] + NA_COLS)
    cpad = GRID_W - NA_COLS
    rpb_c = jnp.pad(rpb, ((0, 0), (0, 0), (0, 0), (cpad, cpad)))
    by_col = jnp.stack([rpb_c[..., GRID_W - 1 - c:2 * GRID_W - 1 - c] for c in range(GRID_W)], axis=-2)
    by_col = jnp.where(col_ok, by_col * LOG2E, -jnp.inf)

    plans = []
    for i in (0, 1, GRID_H // NA_QROWS - 1):
        rows_q = NA_QROWS * i + qr
        row0_k = min(max(NA_QROWS * i - NA_ROWS // 2, 0), GRID_H - NA_KROWS)
        rows_k = row0_k + kr
        r0 = np.clip(rows_q - NA_ROWS // 2, 0, GRID_H - NA_ROWS)
        row_ok = (rows_k[None, :] >= r0[:, None]) & (rows_k[None, :] < r0[:, None] + NA_ROWS)
        slab = rows_k[None, :] - rows_q[:, None] + NA_ROWS - 1
        plans.append([[int(slab[r, c]) if row_ok[r, c] else None for c in range(NA_KROWS)]
                      for r in range(NA_QROWS)])

    heads_per_step = NA_HEADS // 2

    def tile_kernel(tab_ref, o_ref):
        kind = pl.program_id(1)
        for which, plan in enumerate(plans):
            @pl.when(kind == which)
            def _(plan=plan):
                for h in range(heads_per_step):
                    for r in range(NA_QROWS):
                        for c in range(NA_KROWS):
                            if plan[r][c] is None:
                                blk = jnp.full((GRID_W, GRID_W), -jnp.inf, BF16)
                            else:
                                blk = tab_ref[0, h, plan[r][c]].astype(BF16)
                            o_ref[0, 0, h, r * GRID_W:(r + 1) * GRID_W, c * GRID_W:(c + 1) * GRID_W] = blk

    n_slab = 2 * NA_ROWS - 1
    return pl.pallas_call(
        tile_kernel,
        out_shape=jax.ShapeDtypeStruct((DEPTH, len(plans), NA_HEADS, NA_QROWS * GRID_W, NA_KROWS * GRID_W), BF16),
        grid=(DEPTH, len(plans), NA_HEADS // heads_per_step),
        in_specs=[pl.BlockSpec((1, heads_per_step, n_slab, GRID_W, GRID_W), lambda l, t, h: (l, h, 0, 0, 0))],
        out_specs=pl.BlockSpec((1, 1, heads_per_step, NA_QROWS * GRID_W, NA_KROWS * GRID_W),
                               lambda l, t, h: (l, t, h, 0, 0)),
        compiler_params=_params("arbitrary", "arbitrary", "arbitrary"),
        name="na_bias_tiles",
    )(by_col)


def _first_max_of_four(vals):
    a, b, c, d = vals
    m = jnp.maximum(jnp.maximum(a, b), jnp.maximum(c, d))
    idx = jnp.where(a == m, 0.0, jnp.where(b == m, 1.0, jnp.where(c == m, 2.0, 3.0)))
    return m, idx


MOE_CHUNK = EXPERTS_PER_GROUP * D_EXPERT


def _out_kernel(xc_ref, xl_ref, mod_ref, oc_ref, oda_ref, ona_ref, gate_ref, wa_ref, wb_ref, wo_ref, g2_ref,
                wrt_ref, br_ref, wg_ref, wu_ref, wd_ref, yc_ref, yl_ref):
    is_ctx = pl.program_id(0) == 0
    oc = oc_ref[0]
    oa = jnp.where(is_ctx, oc[:, :DA_WIDTH], oda_ref[0])
    ob = jnp.where(is_ctx, oc[:, DA_WIDTH:], ona_ref[0])
    ya = _dot(oa, wa_ref[0])
    yb = _dot(ob, wb_ref[0])
    g = gate_ref[0]
    mixed = (g[:, :D_MODEL].astype(F32) * ya + g[:, D_MODEL:].astype(F32) * yb).astype(BF16)
    mod = mod_ref[0]
    x1 = jnp.where(is_ctx, xc_ref[0], xl_ref[0]) + mod[2:3] * _dot(mixed, wo_ref[0])

    ms = jnp.mean(x1 * x1, axis=-1, keepdims=True)
    h2 = x1 * lax.rsqrt(ms + EPS) * g2_ref[0] * (1.0 + mod[4:5]) + mod[3:4]
    h2_hi = h2.astype(BF16)

    h2_lo = (h2 - h2_hi.astype(F32)).astype(BF16)
    wr = wrt_ref[...]
    wr_hi = wr.astype(BF16)
    wr_lo = (wr - wr_hi.astype(F32)).astype(BF16)
    by_hi = _dot_nt(jnp.concatenate([wr_hi, wr_lo], axis=0), h2_hi)
    logits = by_hi[:N_EXPERTS] + (by_hi[N_EXPERTS:] + _dot_nt(wr_hi, h2_lo))
    scores = jax.nn.sigmoid(logits)
    sel = scores + br_ref[...]
    score_rows = [scores[e:e + 1] for e in range(N_EXPERTS)]
    sel_rows = [sel[e:e + 1] for e in range(N_EXPERTS)]

    neg = -jnp.inf
    best = None
    for grp in range(N_GROUPS):
        vals = sel_rows[grp * EXPERTS_PER_GROUP:(grp + 1) * EXPERTS_PER_GROUP]
        m1, i1 = _first_max_of_four(vals)
        rest = [jnp.where(i1 == float(j), neg, v) for j, v in enumerate(vals)]
        m2, i2 = _first_max_of_four(rest)
        cand = (m1 + m2, i1 + float(grp * EXPERTS_PER_GROUP), i2 + float(grp * EXPERTS_PER_GROUP))
        if best is None:
            best = cand
        else:
            upd = cand[0] > best[0]
            best = tuple(jnp.where(upd, n, o) for n, o in zip(cand, best))
    _, e1, e2 = best
    hits1 = [e1 == float(e) for e in range(N_EXPERTS)]
    hits2 = [e2 == float(e) for e in range(N_EXPERTS)]
    s1 = functools.reduce(lambda a, b: a + b, [jnp.where(h, r, 0.0) for h, r in zip(hits1, score_rows)])
    s2 = functools.reduce(lambda a, b: a + b, [jnp.where(h, r, 0.0) for h, r in zip(hits2, score_rows)])
    den = s1 + s2
    w1 = s1 / den
    w2 = s2 / den
    rows = [jnp.where(h1, w1, 0.0) + jnp.where(h2_, w2, 0.0) for h1, h2_ in zip(hits1, hits2)]
    rows.append(jnp.zeros((LANES - N_EXPERTS, rows[0].shape[1]), F32))
    comb = jnp.concatenate(rows, axis=0).T

    acc = jnp.zeros(x1.shape, F32)
    for c in range(N_GROUPS):
        parts = []
        for j in range(EXPERTS_PER_GROUP):
            e = c * EXPERTS_PER_GROUP + j
            g = _dot(h2_hi, wg_ref[0, e])
            u = _dot(h2_hi, wu_ref[0, e])
            parts.append(g * jax.nn.sigmoid(g) * u * comb[:, e:e + 1])
        hs = jnp.concatenate(parts, axis=1).astype(BF16)
        acc = acc + _dot(hs, wd_ref[0, c * MOE_CHUNK:(c + 1) * MOE_CHUNK, :])
    out = x1 + mod[5:6] * acc

    @pl.when(is_ctx)
    def _():
        yc_ref[0] = out

    @pl.when(jnp.logical_not(is_ctx))
    def _():
        yl_ref[0] = out


def _output_stage(l, x_ctx, x_lat, mods, o_ctx, o_da, o_na, gates, wa_b, wb_b, wo_b, g_norm2, w_router_t,
                  b_router_c, wg_all, wu_all, wd_all):
    tm = 256
    nt = SEG_ROWS // tm
    hidden = N_EXPERTS * D_EXPERT
    once = pl.Buffered(1)
    return pl.pallas_call(
        _out_kernel,
        out_shape=(
            jax.ShapeDtypeStruct((1, SEG_ROWS, D_MODEL), F32),
            jax.ShapeDtypeStruct((DEC_BATCH, DEC_SEQ, D_MODEL), F32),
        ),
        grid=(N_SEG, nt),
        in_specs=[
            _ctx_rows_spec(tm, D_MODEL),
            _lat_rows_spec(tm, D_MODEL),
            pl.BlockSpec((1, N_MOD, D_MODEL), lambda s, i: (s, 0, 0)),
            _ctx_rows_spec(tm, D_MODEL),
            _lat_rows_spec(tm, DA_WIDTH),
            _lat_rows_spec(tm, NA_WIDTH),
            pl.BlockSpec((1, tm, 2 * D_MODEL), lambda s, i: (s, i, 0)),
            pl.BlockSpec((1, DA_WIDTH, D_MODEL), lambda s, i: (l, 0, 0)),
            pl.BlockSpec((1, NA_WIDTH, D_MODEL), lambda s, i: (l, 0, 0)),
            pl.BlockSpec((1, D_MODEL, D_MODEL), lambda s, i: (l, 0, 0)),
            pl.BlockSpec((1, 1, D_MODEL), lambda s, i: (l, 0, 0)),
            pl.BlockSpec((N_EXPERTS, D_MODEL), lambda s, i: (0, 0)),
            pl.BlockSpec((N_EXPERTS, 1), lambda s, i: (0, 0)),
            pl.BlockSpec((1, N_EXPERTS, D_MODEL, D_EXPERT), lambda s, i: (l, 0, 0, 0), pipeline_mode=once),
            pl.BlockSpec((1, N_EXPERTS, D_MODEL, D_EXPERT), lambda s, i: (l, 0, 0, 0), pipeline_mode=once),
            pl.BlockSpec((1, hidden, D_MODEL), lambda s, i: (l, 0, 0), pipeline_mode=once),
        ],
        out_specs=(_ctx_rows_spec(tm, D_MODEL), _lat_rows_spec(tm, D_MODEL)),
        compiler_params=_params("arbitrary", "arbitrary"),
        name="output_experts",
    )(x_ctx, x_lat, mods, o_ctx, o_da, o_na, gates, wa_b, wb_b, wo_b, g_norm2, w_router_t, b_router_c,
      wg_all, wu_all, wd_all)


def _rope_tables():
    t = np.arange(DEC_SEQ)
    pos = np.stack([t // GRID_W, t % GRID_W], axis=-1).astype(np.float32)
    inv_freq = (ROPE_BASE ** (-np.arange(ROPE_PAIRS, dtype=np.float32) / ROPE_PAIRS)).astype(np.float32)
    ang = pos[:, :, None] * inv_freq
    lane = np.arange(LANES)
    axis = (lane % HEAD_DIM) // (2 * ROPE_PAIRS)
    pair = lane % ROPE_PAIRS
    second = ((lane // ROPE_PAIRS) % 2).astype(bool)
    a = ang[:, axis, pair]
    cos = np.cos(a)
    sin = np.sin(a)
    lat = np.concatenate([cos, np.where(second, 0.0, -sin), np.where(second, sin, 0.0)], axis=1)
    ident = np.concatenate([np.ones_like(cos), np.zeros_like(cos), np.zeros_like(cos)], axis=1)
    return jnp.asarray(np.stack([ident, lat]).astype(np.float32))


def _head_block_diag():
    r = np.arange(CHUNK // 2) // HEAD_DIM
    return jnp.asarray((r[:, None] == r[None, :]).astype(np.float32), dtype=BF16)


def kernel(x_prompt, x_sample, cache_da_k, cache_da_v, cache_na_k, cache_na_v, c, c_ctx, w_mod, b_mod, g_norm1,
           g_norm2, w_in, g_q_da, g_k_da, g_q_na, g_k_na, lam_q1, lam_k1, lam_q2, lam_k2, g_subln, rpb, w_br_a,
           w_br_b, w_out, w_router, b_router, w_gate, w_up, w_down):
    L = DEPTH
    x_ctx = x_prompt.reshape(1, SEG_ROWS, D_MODEL)
    x_lat = x_sample
    cond8 = jnp.concatenate([c_ctx[None], c, jnp.zeros((8 - N_SEG, D_MODEL), F32)], axis=0)
    w_in_b = w_in.astype(BF16)
    wa_b = w_br_a.astype(BF16)
    wb_b = w_br_b.astype(BF16)
    wo_b = w_out.astype(BF16)
    hidden = N_EXPERTS * D_EXPERT
    wg_all = w_gate.astype(BF16)
    wu_all = w_up.astype(BF16)
    wd_all = w_down.astype(BF16).reshape(L, hidden, D_MODEL)
    w_router_t = w_router.T
    b_router_c = b_router.reshape(N_EXPERTS, 1)
    head_gains = jnp.stack([jnp.tile(g, (1, CHUNK // HEAD_DIM)) for g in (g_q_da, g_k_da, g_q_na, g_k_na)],
                           axis=1).reshape(L, 4, CHUNK)
    g_norm1 = g_norm1.reshape(L, 1, D_MODEL)
    g_norm2 = g_norm2.reshape(L, 1, D_MODEL)
    g_subln = g_subln.reshape(L, 1, LANES)
    pad64 = lambda t: jnp.pad(t, ((0, 0), (0, LANES - HEAD_DIM)))
    lam_inits = jnp.asarray([0.8 - 0.6 * math.exp(-0.3 * l) for l in range(L)], F32)
    lam_tab = jnp.stack([pad64(lam_q1), pad64(lam_k1), pad64(lam_q2), pad64(lam_k2),
                         jnp.broadcast_to(lam_inits[:, None], (L, LANES)),
                         jnp.zeros((L, LANES), F32), jnp.zeros((L, LANES), F32), jnp.zeros((L, LANES), F32)],
                        axis=1)
    cache_da_k_t = jnp.swapaxes(cache_da_k, -1, -2)
    cache_na_k_t = jnp.swapaxes(cache_na_k, -1, -2)
    cache_na_v_t = jnp.swapaxes(cache_na_v, -1, -2)
    na_bias = _na_bias_tables(rpb)
    rope_tab = _rope_tables()
    bd = _head_block_diag()

    mods_all = _modulation(cond8, w_mod, b_mod)[:, :N_SEG].reshape(L, N_SEG, N_MOD, D_MODEL)

    caches = None
    for l in range(L):
        mods = mods_all[l]
        qkv, gates, *caches = _in_projection(l, x_ctx, x_lat, mods, g_norm1, w_in_b, head_gains, bd, rope_tab,
                                             caches)
        o_ctx = _attention_ctx(l, qkv, lam_tab, g_subln).reshape(1, SEG_ROWS, D_MODEL)
        o_da = _attention_da(l, qkv, cache_da_k_t, cache_da_v, lam_tab, g_subln)
        o_na = _attention_na(l, qkv, cache_na_k_t, cache_na_v_t, na_bias)
        x_ctx, x_lat = _output_stage(l, x_ctx, x_lat, mods, o_ctx, o_da, o_na, gates, wa_b, wb_b, wo_b, g_norm2,
                                     w_router_t, b_router_c, wg_all, wu_all, wd_all)

    y_prompt = x_ctx.reshape(BATCH, SEQ, D_MODEL)
    y_sample = x_lat
    da_k_t, new_da_v, na_k_t, na_v_t = caches
    return (y_prompt, y_sample, jnp.swapaxes(da_k_t, -1, -2), new_da_v, jnp.swapaxes(na_k_t, -1, -2),
            jnp.swapaxes(na_v_t, -1, -2))
```

```python
import functools
import math

import numpy as np
import jax
import jax.numpy as jnp
from jax import lax
from jax.experimental import pallas as pl
from jax.experimental.pallas import tpu as pltpu

D_MODEL = 1024
BATCH = 16
SEQ = 256
DEPTH = 4
DEC_BATCH = 4
DEC_SEQ = 4096
PAST_LEN = 512
GRID_W = 64
GRID_H = DEC_SEQ // GRID_W
HEAD_DIM = 64
DA_HEADS = 4
DA_WIDTH = 512
NA_HEADS = 8
NA_WIDTH = 512
NA_ROWS = 8
NA_COLS = 16
IN_COLS = 3 * DA_WIDTH + 3 * NA_WIDTH + 2 * D_MODEL
ROPE_BASE = 10000.0
ROPE_PAIRS = HEAD_DIM // 4
N_EXPERTS = 16
N_GROUPS = 4
EXPERTS_PER_GROUP = 4
D_EXPERT = 256
N_MOD = 6
EPS = 1e-6
ATTN_SCALE = HEAD_DIM ** -0.5
Q_PRESCALE = ATTN_SCALE * math.log2(math.e)
LOG2E = math.log2(math.e)

N_SEG = 1 + DEC_BATCH
SEG_ROWS = DEC_SEQ
LANES = 128
CHUNK = 512

VMEM_LIMIT = 56 * 1024 * 1024

F32 = jnp.float32
BF16 = jnp.bfloat16


def _dot(a, b):
    return jnp.dot(a, b, preferred_element_type=F32)


def _dot_nt(a, b):
    return lax.dot_general(a, b, (((1,), (1,)), ((), ())), preferred_element_type=F32)


def _params(*sem):
    return pltpu.CompilerParams(dimension_semantics=sem, vmem_limit_bytes=VMEM_LIMIT)


def _ctx_rows_spec(tm, width):
    nt = SEG_ROWS // tm
    return pl.BlockSpec((1, tm, width), lambda s, i: (0, jnp.where(s == 0, i, nt - 1), 0))


def _lat_rows_spec(tm, width):
    return pl.BlockSpec((1, tm, width), lambda s, i: (jnp.maximum(s - 1, 0), jnp.where(s == 0, 0, i), 0))


def _mod_kernel(cond_ref, w_ref, b_ref, o_ref):
    c = cond_ref[...]
    sc = (c * jax.nn.sigmoid(c)).astype(BF16)
    o_ref[0] = _dot(sc, w_ref[0].astype(BF16)) + b_ref[0]


def _modulation(cond8, w_mod, b_mod):
    tn = 1536
    return pl.pallas_call(
        _mod_kernel,
        out_shape=jax.ShapeDtypeStruct((DEPTH, 8, N_MOD * D_MODEL), F32),
        grid=(DEPTH, N_MOD * D_MODEL // tn),
        in_specs=[
            pl.BlockSpec((8, D_MODEL), lambda l, j: (0, 0)),
            pl.BlockSpec((1, D_MODEL, tn), lambda l, j: (l, 0, j)),
            pl.BlockSpec((1, 1, tn), lambda l, j: (l, 0, j)),
        ],
        out_specs=pl.BlockSpec((1, 8, tn), lambda l, j: (l, 0, j)),
        compiler_params=_params("arbitrary", "arbitrary"),
        name="modulation",
    )(cond8, w_mod, b_mod.reshape(DEPTH, 1, N_MOD * D_MODEL))


def _store_heads_t(dst_ref, b, t, rows):
    tt = t[rows].T
    lead = dst_ref.shape[2:-2]
    for h in range(CHUNK // HEAD_DIM):
        idx = np.unravel_index(h, lead)
        dst_ref[(b, 0) + tuple(int(i) for i in idx)] = tt[h * HEAD_DIM:(h + 1) * HEAD_DIM, :]


def _inproj_kernel(xc_ref, xl_ref, mod_ref, g1_ref, w_ref, hg_ref, bd_ref, rope_ref, *rest):
    qkv_ref, gate_ref, cdk_ref, cdv_ref, cnk_ref, cnv_ref = rest[-6:]
    s = pl.program_id(0)
    x = jnp.where(s == 0, xc_ref[0], xl_ref[0])
    ms = jnp.mean(x * x, axis=-1, keepdims=True)
    y = x * lax.rsqrt(ms + EPS) * g1_ref[0]
    mod = mod_ref[0]
    h = (y * (1.0 + mod[1:2]) + mod[0:1]).astype(BF16)

    rope = rope_ref[0]
    cos4 = jnp.concatenate([rope[:, 0:LANES]] * 4, axis=1)
    sin_up4 = jnp.concatenate([rope[:, LANES:2 * LANES]] * 4, axis=1)
    sin_dn4 = jnp.concatenate([rope[:, 2 * LANES:3 * LANES]] * 4, axis=1)

    def head_norm(acc, row):
        sq = (acc * acc).astype(BF16)
        half = CHUNK // 2
        ssum = jnp.concatenate([_dot(sq[:, :half], bd_ref[...]), _dot(sq[:, half:], bd_ref[...])], axis=1)
        return acc * lax.rsqrt(ssum * (1.0 / HEAD_DIM) + EPS) * hg_ref[0, row:row + 1, :]

    def rope_rot(t):
        return (t * cos4 + pltpu.roll(t, CHUNK - ROPE_PAIRS, 1) * sin_up4
                + pltpu.roll(t, ROPE_PAIRS, 1) * sin_dn4)

    def proj(c):
        return _dot(h, w_ref[0, :, c * CHUNK:(c + 1) * CHUNK])

    is_ctx = s == 0
    batches = [(b, slice(b * SEQ, (b + 1) * SEQ)) for b in range(x.shape[0] // SEQ)]

    q = rope_rot(head_norm(proj(0), 0)) * Q_PRESCALE
    qkv_ref[0, :, 0:CHUNK] = q.astype(BF16)

    k = rope_rot(head_norm(proj(1), 1))
    qkv_ref[0, :, CHUNK:2 * CHUNK] = k.astype(BF16)

    @pl.when(is_ctx)
    def _():
        for b, rows in batches:
            _store_heads_t(cdk_ref, b, k, rows)

    v = proj(2)
    qkv_ref[0, :, 2 * CHUNK:3 * CHUNK] = v.astype(BF16)

    @pl.when(is_ctx)
    def _():
        for b, rows in batches:
            for h in range(DA_HEADS):
                cdv_ref[b, 0, h] = v[rows, h * LANES:(h + 1) * LANES]

    q = head_norm(proj(3), 2) * Q_PRESCALE
    qkv_ref[0, :, 3 * CHUNK:4 * CHUNK] = q.astype(BF16)

    k = head_norm(proj(4), 3)
    qkv_ref[0, :, 4 * CHUNK:5 * CHUNK] = k.astype(BF16)

    @pl.when(is_ctx)
    def _():
        for b, rows in batches:
            _store_heads_t(cnk_ref, b, k, rows)

    v = proj(5)
    qkv_ref[0, :, 5 * CHUNK:6 * CHUNK] = v.astype(BF16)

    @pl.when(is_ctx)
    def _():
        for b, rows in batches:
            _store_heads_t(cnv_ref, b, v, rows)

    for c in range(4):
        g = proj(6 + c)
        gate_ref[0, :, c * CHUNK:(c + 1) * CHUNK] = jax.nn.sigmoid(g).astype(BF16)


CACHE_SHAPES = (
    (BATCH, DEPTH, 2, DA_HEADS, HEAD_DIM, SEQ),
    (BATCH, DEPTH, DA_HEADS, SEQ, 2 * HEAD_DIM),
    (BATCH, DEPTH, NA_HEADS, HEAD_DIM, SEQ),
    (BATCH, DEPTH, NA_HEADS, HEAD_DIM, SEQ),
)


def _in_projection(l, x_ctx, x_lat, mods, g_norm1, w_in_b, head_gains, bd, rope_tab, caches):
    tm = 512
    nt = SEG_ROWS // tm
    nb = tm // SEQ

    def cache_spec(shape):
        block = (nb, 1) + shape[2:]
        zeros = (0,) * (len(shape) - 2)
        return pl.BlockSpec(block, lambda s, i: (jnp.where(s == 0, i, nt - 1), l) + zeros)

    chained = caches is not None
    cache_in = list(caches) if chained else []
    n_in = 8
    return pl.pallas_call(
        _inproj_kernel,
        out_shape=(
            jax.ShapeDtypeStruct((N_SEG, SEG_ROWS, 6 * CHUNK), BF16),
            jax.ShapeDtypeStruct((N_SEG, SEG_ROWS, 2 * D_MODEL), BF16),
        ) + tuple(jax.ShapeDtypeStruct(shape, F32) for shape in CACHE_SHAPES),
        grid=(N_SEG, nt),
        in_specs=[
            _ctx_rows_spec(tm, D_MODEL),
            _lat_rows_spec(tm, D_MODEL),
            pl.BlockSpec((1, N_MOD, D_MODEL), lambda s, i: (s, 0, 0)),
            pl.BlockSpec((1, 1, D_MODEL), lambda s, i: (l, 0, 0)),
            pl.BlockSpec((1, D_MODEL, IN_COLS), lambda s, i: (l, 0, 0), pipeline_mode=pl.Buffered(1)),
            pl.BlockSpec((1, 4, CHUNK), lambda s, i: (l, 0, 0)),
            pl.BlockSpec((CHUNK // 2, CHUNK // 2), lambda s, i: (0, 0)),
            pl.BlockSpec((1, tm, 3 * LANES), lambda s, i: (jnp.minimum(s, 1), i, 0)),
        ] + [pl.BlockSpec(memory_space=pl.ANY) for _ in cache_in],
        out_specs=(
            pl.BlockSpec((1, tm, 6 * CHUNK), lambda s, i: (s, i, 0)),
            pl.BlockSpec((1, tm, 2 * D_MODEL), lambda s, i: (s, i, 0)),
        ) + tuple(cache_spec(shape) for shape in CACHE_SHAPES),
        input_output_aliases={n_in + j: 2 + j for j in range(len(cache_in))},
        compiler_params=_params("arbitrary", "arbitrary"),
        name="in_projection",
    )(x_ctx, x_lat, mods, g_norm1, w_in_b, head_gains, bd, rope_tab, *cache_in)


def _lam_value(lam_ref):
    lp = lam_ref[0]
    s1 = jnp.sum(lp[0:1] * lp[1:2], axis=-1, keepdims=True)
    s2 = jnp.sum(lp[2:3] * lp[3:4], axis=-1, keepdims=True)
    lam_init = lp[4:5, 0:1]
    return jnp.exp(s1) - jnp.exp(s2) + lam_init, 1.0 - lam_init


def _half_masks():
    lane = lax.broadcasted_iota(jnp.int32, (1, LANES), 1)
    return (lane < HEAD_DIM, lane >= HEAD_DIM)


def _pick_head(mask, t):
    return jnp.where(mask, t, jnp.zeros_like(t))


def _sub_layer_norm(o, gsub, one_minus):
    ms = jnp.mean(o * o, axis=-1, keepdims=True)
    return o * lax.rsqrt(ms + EPS) * gsub * one_minus


def _attn_ctx_kernel(qkv_ref, lam_ref, gsub_ref, o_ref):
    lam, one_minus = _lam_value(lam_ref)
    masks = _half_masks()
    qda = qkv_ref[0, :, 0:CHUNK]
    qna = qkv_ref[0, :, 3 * CHUNK:4 * CHUNK]
    gsub = gsub_ref[0]

    def kv(c0):
        off = c0 + CHUNK if c0 < 2 * CHUNK else c0 + 2 * CHUNK
        return qkv_ref[0, :, off:off + LANES]

    def softmax_parts(sc):
        m = jnp.max(sc, axis=-1, keepdims=True)
        e = jnp.exp2(sc - m)
        return e, 1.0 / jnp.sum(e, axis=-1, keepdims=True)

    for h in range(DA_HEADS):
        p, sub = divmod(h, 2)
        q1 = _pick_head(masks[sub], qda[:, p * LANES:(p + 1) * LANES])
        q2 = _pick_head(masks[sub], qda[:, 2 * LANES + p * LANES:2 * LANES + (p + 1) * LANES])
        e1, r1 = softmax_parts(_dot_nt(q1, kv(p * LANES)))
        e2, r2 = softmax_parts(_dot_nt(q2, kv(2 * LANES + p * LANES)))
        a = e1 * r1 - e2 * (lam * r2)
        o = _dot(a.astype(BF16), kv(CHUNK + h * LANES))
        o_ref[:, h * LANES:(h + 1) * LANES] = _sub_layer_norm(o, gsub, one_minus).astype(BF16)

    for p in range(NA_HEADS // 2):
        kp = kv(2 * CHUNK + p * LANES)
        vp = kv(3 * CHUNK + p * LANES)
        both = None
        for sub in range(2):
            qm = _pick_head(masks[sub], qna[:, p * LANES:(p + 1) * LANES])
            e, r = softmax_parts(_dot_nt(qm, kp))
            o = _dot(e.astype(BF16), vp) * r
            both = o if sub == 0 else jnp.where(masks[1], o, both)
        o_ref[:, DA_WIDTH + p * LANES:DA_WIDTH + (p + 1) * LANES] = both.astype(BF16)


def _attention_ctx(l, qkv, lam_tab, g_subln):
    return pl.pallas_call(
        _attn_ctx_kernel,
        out_shape=jax.ShapeDtypeStruct((SEG_ROWS, D_MODEL), BF16),
        grid=(BATCH,),
        in_specs=[
            pl.BlockSpec((1, SEQ, 6 * CHUNK), lambda b: (0, b, 0)),
            pl.BlockSpec((1, 8, LANES), lambda b: (l, 0, 0)),
            pl.BlockSpec((1, 1, LANES), lambda b: (l, 0, 0)),
        ],
        out_specs=pl.BlockSpec((SEQ, D_MODEL), lambda b: (b, 0)),
        compiler_params=_params("arbitrary"),
        name="attention_ctx",
    )(qkv, lam_tab, g_subln)


DA_KEYS = DEC_SEQ + PAST_LEN
DA_KCHUNK = 1024
DA_TQ = 256


def _pair_heads(head_ref, lead, first):
    a = head_ref[lead + (first,)]
    b = head_ref[lead + (first + 1,)]
    return jnp.concatenate([a, b], axis=0).T.astype(BF16)


def _attn_da_kernel(q_ref, k_ref, v_ref, kc_ref, vc_ref, lam_ref, gsub_ref, o_ref, vt_ref, kcs_ref, e1_ref, e2_ref):
    chunks = [slice(r, r + DA_KCHUNK) for r in range(0, DEC_SEQ, DA_KCHUNK)] + [slice(DEC_SEQ, DA_KEYS)]
    n_chunks = len(chunks)

    @pl.when(pl.program_id(1) == 0)
    def _():
        for r in range(0, DEC_SEQ, PAST_LEN):
            rows = slice(r, r + PAST_LEN)
            vt_ref[:, rows] = v_ref[0, rows, :].astype(F32).T.astype(BF16)
        for h in range(DA_HEADS):
            vt_ref[h * LANES:(h + 1) * LANES, DEC_SEQ:DA_KEYS] = vc_ref[0, 0, h].T.astype(BF16)
        for m in range(2):
            for p in range(DA_HEADS // 2):
                c0 = m * 2 * LANES + p * LANES
                kcs_ref[:, c0:c0 + LANES] = _pair_heads(kc_ref, (0, 0, m), 2 * p)

    lam, one_minus = _lam_value(lam_ref)
    masks = _half_masks()
    q = q_ref[0]
    gsub = gsub_ref[0]

    def key_chunk(c, cols):
        if chunks[c].start < DEC_SEQ:
            return k_ref[0, chunks[c], cols]
        return kcs_ref[:, cols]

    def score_map(qm, cols, e_ref, weight):
        maxes, sums = [], []
        for c in range(n_chunks):
            s = _dot_nt(key_chunk(c, cols), qm)
            mc = jnp.max(s, axis=0, keepdims=True)
            e = jnp.exp2(s - mc)
            sums.append(jnp.sum(e, axis=0, keepdims=True))
            maxes.append(mc)
            e_ref[chunks[c], :] = e.astype(BF16)
        m = functools.reduce(jnp.maximum, maxes)
        facs = [jnp.exp2(mc - m) for mc in maxes]
        tot = functools.reduce(lambda a, b: a + b, [lc * fc for lc, fc in zip(sums, facs)])
        w = weight / tot
        return [(fc * w).astype(BF16) for fc in facs]

    for h in range(DA_HEADS):
        p, sub = divmod(h, 2)
        c1 = slice(p * LANES, (p + 1) * LANES)
        c2 = slice(2 * LANES + p * LANES, 2 * LANES + (p + 1) * LANES)
        f1 = score_map(_pick_head(masks[sub], q[:, c1]), c1, e1_ref, 1.0)
        f2 = score_map(_pick_head(masks[sub], q[:, c2]), c2, e2_ref, lam)
        o_t = jnp.zeros((LANES, DA_TQ), F32)
        for c, rows in enumerate(chunks):
            a = e1_ref[rows, :] * f1[c] - e2_ref[rows, :] * f2[c]
            o_t = o_t + _dot(vt_ref[h * LANES:(h + 1) * LANES, rows], a)
        o_ref[0, :, h * LANES:(h + 1) * LANES] = _sub_layer_norm(o_t.T, gsub, one_minus).astype(BF16)


def _attention_da(l, qkv, ctx_k, ctx_v, lam_tab, g_subln):
    tq = DA_TQ
    return pl.pallas_call(
        _attn_da_kernel,
        out_shape=jax.ShapeDtypeStruct((DEC_BATCH, DEC_SEQ, DA_WIDTH), BF16),
        grid=(DEC_BATCH, DEC_SEQ // tq),
        in_specs=[
            pl.BlockSpec((1, tq, CHUNK), lambda b, i: (b + 1, i, 0)),
            pl.BlockSpec((1, DEC_SEQ, CHUNK), lambda b, i: (b + 1, 0, 1), pipeline_mode=pl.Buffered(1)),
            pl.BlockSpec((1, DEC_SEQ, CHUNK), lambda b, i: (b + 1, 0, 2), pipeline_mode=pl.Buffered(1)),
            pl.BlockSpec((1, 1, 2, DA_HEADS, HEAD_DIM, PAST_LEN), lambda b, i: (b, l, 0, 0, 0, 0)),
            pl.BlockSpec((1, 1, DA_HEADS, PAST_LEN, 2 * HEAD_DIM), lambda b, i: (b, l, 0, 0, 0)),
            pl.BlockSpec((1, 8, LANES), lambda b, i: (l, 0, 0)),
            pl.BlockSpec((1, 1, LANES), lambda b, i: (l, 0, 0)),
        ],
        out_specs=pl.BlockSpec((1, tq, DA_WIDTH), lambda b, i: (b, i, 0)),
        scratch_shapes=[
            pltpu.VMEM((DA_WIDTH, DA_KEYS), BF16),
            pltpu.VMEM((PAST_LEN, DA_WIDTH), BF16),
            pltpu.VMEM((DA_KEYS, tq), BF16),
            pltpu.VMEM((DA_KEYS, tq), BF16),
        ],
        compiler_params=_params("arbitrary", "arbitrary"),
        name="attention_da",
    )(qkv, qkv, qkv, ctx_k, ctx_v, lam_tab, g_subln)


NA_QROWS = 4
NA_KROWS = 12


def _attn_na_kernel(q_ref, k_ref, v_ref, kc_ref, vc_ref, bias_ref, o_ref, kcs_ref, vcs_ref):
    i = pl.program_id(1)

    @pl.when(i == 0)
    def _():
        for p in range(NA_HEADS // 2):
            kcs_ref[:, p * LANES:(p + 1) * LANES] = _pair_heads(kc_ref, (0, 0), 2 * p)
            vcs_ref[:, p * LANES:(p + 1) * LANES] = _pair_heads(vc_ref, (0, 0), 2 * p)

    masks = _half_masks()
    q = q_ref[0]
    row0 = jnp.clip(i * NA_QROWS - NA_ROWS // 2, 0, GRID_H - NA_KROWS)
    start = pl.multiple_of(row0 * GRID_W, 256)
    nk = NA_KROWS * GRID_W

    for p in range(NA_HEADS // 2):
        c0 = p * LANES
        kw = k_ref[0, pl.ds(start, nk), c0:c0 + LANES]
        vw = v_ref[0, pl.ds(start, nk), c0:c0 + LANES]
        kc = kcs_ref[:, c0:c0 + LANES]
        vc = vcs_ref[:, c0:c0 + LANES]
        both = None
        for sub in range(2):
            qm = _pick_head(masks[sub], q[:, c0:c0 + LANES])
            s_loc = _dot_nt(qm, kw) + bias_ref[0, 0, 2 * p + sub].astype(F32)
            s_ctx = _dot_nt(qm, kc)
            m = jnp.maximum(jnp.max(s_loc, axis=-1, keepdims=True), jnp.max(s_ctx, axis=-1, keepdims=True))
            e_loc = jnp.exp2(s_loc - m)
            e_ctx = jnp.exp2(s_ctx - m)
            tot = jnp.sum(e_loc, axis=-1, keepdims=True) + jnp.sum(e_ctx, axis=-1, keepdims=True)
            o = (_dot(e_loc.astype(BF16), vw) + _dot(e_ctx.astype(BF16), vc)) * (1.0 / tot)
            both = o if sub == 0 else jnp.where(masks[1], o, both)
        o_ref[0, :, c0:c0 + LANES] = both.astype(BF16)


def _attention_na(l, qkv, ctx_k, ctx_v, bias):
    tq = NA_QROWS * GRID_W
    n_i = DEC_SEQ // tq

    def bias_idx(b, i):
        return (l, jnp.where(i == 0, 0, jnp.where(i == n_i - 1, 2, 1)), 0, 0, 0)

    return pl.pallas_call(
        _attn_na_kernel,
        out_shape=jax.ShapeDtypeStruct((DEC_BATCH, DEC_SEQ, NA_WIDTH), BF16),
        grid=(DEC_BATCH, n_i),
        in_specs=[
            pl.BlockSpec((1, tq, CHUNK), lambda b, i: (b + 1, i, 3)),
            pl.BlockSpec((1, DEC_SEQ, CHUNK), lambda b, i: (b + 1, 0, 4), pipeline_mode=pl.Buffered(1)),
            pl.BlockSpec((1, DEC_SEQ, CHUNK), lambda b, i: (b + 1, 0, 5), pipeline_mode=pl.Buffered(1)),
            pl.BlockSpec((1, 1, NA_HEADS, HEAD_DIM, PAST_LEN), lambda b, i: (b, l, 0, 0, 0)),
            pl.BlockSpec((1, 1, NA_HEADS, HEAD_DIM, PAST_LEN), lambda b, i: (b, l, 0, 0, 0)),
            pl.BlockSpec((1, 1, NA_HEADS, tq, NA_KROWS * GRID_W), bias_idx),
        ],
        out_specs=pl.BlockSpec((1, tq, NA_WIDTH), lambda b, i: (b, i, 0)),
        scratch_shapes=[
            pltpu.VMEM((PAST_LEN, NA_WIDTH), BF16),
            pltpu.VMEM((PAST_LEN, NA_WIDTH), BF16),
        ],
        compiler_params=_params("arbitrary", "arbitrary"),
        name="attention_na",
    )(qkv, qkv, qkv, ctx_k, ctx_v, bias)


def _na_bias_tables(rpb):
    qr = np.arange(NA_QROWS)
    qc = np.arange(GRID_W)
    kr = np.arange(NA_KROWS)
    kc = np.arange(GRID_W)
    c0 = np.clip(qc - NA_COLS // 2, 0, GRID_W - NA_COLS)
    col_ok = (kc[None, :] >= c0[:, None]) & (kc[None, :] < c0[:, None] + NA_COLS)
    cpad = GRID_W - NA_COLS
    rpb_c = jnp.pad(rpb, ((0, 0), (0, 0), (0, 0), (cpad, cpad)))
    by_col = jnp.stack([rpb_c[..., GRID_W - 1 - c:2 * GRID_W - 1 - c] for c in range(GRID_W)], axis=-2)
    by_col = jnp.where(col_ok, by_col * LOG2E, -jnp.inf)

    plans = []
    for i in (0, 1, GRID_H // NA_QROWS - 1):
        rows_q = NA_QROWS * i + qr
        row0_k = min(max(NA_QROWS * i - NA_ROWS // 2, 0), GRID_H - NA_KROWS)
        rows_k = row0_k + kr
        r0 = np.clip(rows_q - NA_ROWS // 2, 0, GRID_H - NA_ROWS)
        row_ok = (rows_k[None, :] >= r0[:, None]) & (rows_k[None, :] < r0[:, None] + NA_ROWS)
        slab = rows_k[None, :] - rows_q[:, None] + NA_ROWS - 1
        plans.append([[int(slab[r, c]) if row_ok[r, c] else None for c in range(NA_KROWS)]
                      for r in range(NA_QROWS)])

    heads_per_step = NA_HEADS // 2

    def tile_kernel(tab_ref, o_ref):
        kind = pl.program_id(1)
        for which, plan in enumerate(plans):
            @pl.when(kind == which)
            def _(plan=plan):
                for h in range(heads_per_step):
                    for r in range(NA_QROWS):
                        for c in range(NA_KROWS):
                            if plan[r][c] is None:
                                blk = jnp.full((GRID_W, GRID_W), -jnp.inf, BF16)
                            else:
                                blk = tab_ref[0, h, plan[r][c]].astype(BF16)
                            o_ref[0, 0, h, r * GRID_W:(r + 1) * GRID_W, c * GRID_W:(c + 1) * GRID_W] = blk

    n_slab = 2 * NA_ROWS - 1
    return pl.pallas_call(
        tile_kernel,
        out_shape=jax.ShapeDtypeStruct((DEPTH, len(plans), NA_HEADS, NA_QROWS * GRID_W, NA_KROWS * GRID_W), BF16),
        grid=(DEPTH, len(plans), NA_HEADS // heads_per_step),
        in_specs=[pl.BlockSpec((1, heads_per_step, n_slab, GRID_W, GRID_W), lambda l, t, h: (l, h, 0, 0, 0))],
        out_specs=pl.BlockSpec((1, 1, heads_per_step, NA_QROWS * GRID_W, NA_KROWS * GRID_W),
                               lambda l, t, h: (l, t, h, 0, 0)),
        compiler_params=_params("arbitrary", "arbitrary", "arbitrary"),
        name="na_bias_tiles",
    )(by_col)


def _first_max_of_four(vals):
    a, b, c, d = vals
    m = jnp.maximum(jnp.maximum(a, b), jnp.maximum(c, d))
    idx = jnp.where(a == m, 0.0, jnp.where(b == m, 1.0, jnp.where(c == m, 2.0, 3.0)))
    return m, idx


MOE_CHUNK = EXPERTS_PER_GROUP * D_EXPERT


def _out_kernel(xc_ref, xl_ref, mod_ref, oc_ref, oda_ref, ona_ref, gate_ref, wa_ref, wb_ref, wo_ref, g2_ref,
                wrt_ref, br_ref, wg_ref, wu_ref, wd_ref, yc_ref, yl_ref):
    is_ctx = pl.program_id(0) == 0
    oc = oc_ref[0]
    oa = jnp.where(is_ctx, oc[:, :DA_WIDTH], oda_ref[0])
    ob = jnp.where(is_ctx, oc[:, DA_WIDTH:], ona_ref[0])
    ya = _dot(oa, wa_ref[0])
    yb = _dot(ob, wb_ref[0])
    g = gate_ref[0]
    mixed = (g[:, :D_MODEL].astype(F32) * ya + g[:, D_MODEL:].astype(F32) * yb).astype(BF16)
    mod = mod_ref[0]
    x1 = jnp.where(is_ctx, xc_ref[0], xl_ref[0]) + mod[2:3] * _dot(mixed, wo_ref[0])

    ms = jnp.mean(x1 * x1, axis=-1, keepdims=True)
    h2 = x1 * lax.rsqrt(ms + EPS) * g2_ref[0] * (1.0 + mod[4:5]) + mod[3:4]
    h2_hi = h2.astype(BF16)

    h2_lo = (h2 - h2_hi.astype(F32)).astype(BF16)
    wr = wrt_ref[...]
    wr_hi = wr.astype(BF16)
    wr_lo = (wr - wr_hi.astype(F32)).astype(BF16)
    by_hi = _dot_nt(jnp.concatenate([wr_hi, wr_lo], axis=0), h2_hi)
    logits = by_hi[:N_EXPERTS] + (by_hi[N_EXPERTS:] + _dot_nt(wr_hi, h2_lo))
    scores = jax.nn.sigmoid(logits)
    sel = scores + br_ref[...]
    score_rows = [scores[e:e + 1] for e in range(N_EXPERTS)]
    sel_rows = [sel[e:e + 1] for e in range(N_EXPERTS)]

    neg = -jnp.inf
    best = None
    for grp in range(N_GROUPS):
        vals = sel_rows[grp * EXPERTS_PER_GROUP:(grp + 1) * EXPERTS_PER_GROUP]
        m1, i1 = _first_max_of_four(vals)
        rest = [jnp.where(i1 == float(j), neg, v) for j, v in enumerate(vals)]
        m2, i2 = _first_max_of_four(rest)
        cand = (m1 + m2, i1 + float(grp * EXPERTS_PER_GROUP), i2 + float(grp * EXPERTS_PER_GROUP))
        if best is None:
            best = cand
        else:
            upd = cand[0] > best[0]
            best = tuple(jnp.where(upd, n, o) for n, o in zip(cand, best))
    _, e1, e2 = best
    hits1 = [e1 == float(e) for e in range(N_EXPERTS)]
    hits2 = [e2 == float(e) for e in range(N_EXPERTS)]
    s1 = functools.reduce(lambda a, b: a + b, [jnp.where(h, r, 0.0) for h, r in zip(hits1, score_rows)])
    s2 = functools.reduce(lambda a, b: a + b, [jnp.where(h, r, 0.0) for h, r in zip(hits2, score_rows)])
    den = s1 + s2
    w1 = s1 / den
    w2 = s2 / den
    rows = [jnp.where(h1, w1, 0.0) + jnp.where(h2_, w2, 0.0) for h1, h2_ in zip(hits1, hits2)]
    rows.append(jnp.zeros((LANES - N_EXPERTS, rows[0].shape[1]), F32))
    comb = jnp.concatenate(rows, axis=0).T

    acc = jnp.zeros(x1.shape, F32)
    for c in range(N_GROUPS):
        parts = []
        for j in range(EXPERTS_PER_GROUP):
            e = c * EXPERTS_PER_GROUP + j
            g = _dot(h2_hi, wg_ref[0, e])
            u = _dot(h2_hi, wu_ref[0, e])
            parts.append(g * jax.nn.sigmoid(g) * u * comb[:, e:e + 1])
        hs = jnp.concatenate(parts, axis=1).astype(BF16)
        acc = acc + _dot(hs, wd_ref[0, c * MOE_CHUNK:(c + 1) * MOE_CHUNK, :])
    out = x1 + mod[5:6] * acc

    @pl.when(is_ctx)
    def _():
        yc_ref[0] = out

    @pl.when(jnp.logical_not(is_ctx))
    def _():
        yl_ref[0] = out


def _output_stage(l, x_ctx, x_lat, mods, o_ctx, o_da, o_na, gates, wa_b, wb_b, wo_b, g_norm2, w_router_t,
                  b_router_c, wg_all, wu_all, wd_all):
    tm = 256
    nt = SEG_ROWS // tm
    hidden = N_EXPERTS * D_EXPERT
    once = pl.Buffered(1)
    return pl.pallas_call(
        _out_kernel,
        out_shape=(
            jax.ShapeDtypeStruct((1, SEG_ROWS, D_MODEL), F32),
            jax.ShapeDtypeStruct((DEC_BATCH, DEC_SEQ, D_MODEL), F32),
        ),
        grid=(N_SEG, nt),
        in_specs=[
            _ctx_rows_spec(tm, D_MODEL),
            _lat_rows_spec(tm, D_MODEL),
            pl.BlockSpec((1, N_MOD, D_MODEL), lambda s, i: (s, 0, 0)),
            _ctx_rows_spec(tm, D_MODEL),
            _lat_rows_spec(tm, DA_WIDTH),
            _lat_rows_spec(tm, NA_WIDTH),
            pl.BlockSpec((1, tm, 2 * D_MODEL), lambda s, i: (s, i, 0)),
            pl.BlockSpec((1, DA_WIDTH, D_MODEL), lambda s, i: (l, 0, 0)),
            pl.BlockSpec((1, NA_WIDTH, D_MODEL), lambda s, i: (l, 0, 0)),
            pl.BlockSpec((1, D_MODEL, D_MODEL), lambda s, i: (l, 0, 0)),
            pl.BlockSpec((1, 1, D_MODEL), lambda s, i: (l, 0, 0)),
            pl.BlockSpec((N_EXPERTS, D_MODEL), lambda s, i: (0, 0)),
            pl.BlockSpec((N_EXPERTS, 1), lambda s, i: (0, 0)),
            pl.BlockSpec((1, N_EXPERTS, D_MODEL, D_EXPERT), lambda s, i: (l, 0, 0, 0), pipeline_mode=once),
            pl.BlockSpec((1, N_EXPERTS, D_MODEL, D_EXPERT), lambda s, i: (l, 0, 0, 0), pipeline_mode=once),
            pl.BlockSpec((1, hidden, D_MODEL), lambda s, i: (l, 0, 0), pipeline_mode=once),
        ],
        out_specs=(_ctx_rows_spec(tm, D_MODEL), _lat_rows_spec(tm, D_MODEL)),
        compiler_params=_params("arbitrary", "arbitrary"),
        name="output_experts",
    )(x_ctx, x_lat, mods, o_ctx, o_da, o_na, gates, wa_b, wb_b, wo_b, g_norm2, w_router_t, b_router_c,
      wg_all, wu_all, wd_all)


def _rope_tables():
    t = np.arange(DEC_SEQ)
    pos = np.stack([t // GRID_W, t % GRID_W], axis=-1).astype(np.float32)
    inv_freq = (ROPE_BASE ** (-np.arange(ROPE_PAIRS, dtype=np.float32) / ROPE_PAIRS)).astype(np.float32)
    ang = pos[:, :, None] * inv_freq
    lane = np.arange(LANES)
    axis = (lane % HEAD_DIM) // (2 * ROPE_PAIRS)
    pair = lane % ROPE_PAIRS
    second = ((lane // ROPE_PAIRS) % 2).astype(bool)
    a = ang[:, axis, pair]
    cos = np.cos(a)
    sin = np.sin(a)
    lat = np.concatenate([cos, np.where(second, 0.0, -sin), np.where(second, sin, 0.0)], axis=1)
    ident = np.concatenate([np.ones_like(cos), np.zeros_like(cos), np.zeros_like(cos)], axis=1)
    return jnp.asarray(np.stack([ident, lat]).astype(np.float32))


def _head_block_diag():
    r = np.arange(CHUNK // 2) // HEAD_DIM
    return jnp.asarray((r[:, None] == r[None, :]).astype(np.float32), dtype=BF16)


def kernel(x_prompt, x_sample, cache_da_k, cache_da_v, cache_na_k, cache_na_v, c, c_ctx, w_mod, b_mod, g_norm1,
           g_norm2, w_in, g_q_da, g_k_da, g_q_na, g_k_na, lam_q1, lam_k1, lam_q2, lam_k2, g_subln, rpb, w_br_a,
           w_br_b, w_out, w_router, b_router, w_gate, w_up, w_down):
    L = DEPTH
    x_ctx = x_prompt.reshape(1, SEG_ROWS, D_MODEL)
    x_lat = x_sample
    cond8 = jnp.concatenate([c_ctx[None], c, jnp.zeros((8 - N_SEG, D_MODEL), F32)], axis=0)
    w_in_b = w_in.astype(BF16)
    wa_b = w_br_a.astype(BF16)
    wb_b = w_br_b.astype(BF16)
    wo_b = w_out.astype(BF16)
    hidden = N_EXPERTS * D_EXPERT
    wg_all = w_gate.astype(BF16)
    wu_all = w_up.astype(BF16)
    wd_all = w_down.astype(BF16).reshape(L, hidden, D_MODEL)
    w_router_t = w_router.T
    b_router_c = b_router.reshape(N_EXPERTS, 1)
    head_gains = jnp.stack([jnp.tile(g, (1, CHUNK // HEAD_DIM)) for g in (g_q_da, g_k_da, g_q_na, g_k_na)],
                           axis=1).reshape(L, 4, CHUNK)
    g_norm1 = g_norm1.reshape(L, 1, D_MODEL)
    g_norm2 = g_norm2.reshape(L, 1, D_MODEL)
    g_subln = g_subln.reshape(L, 1, LANES)
    pad64 = lambda t: jnp.pad(t, ((0, 0), (0, LANES - HEAD_DIM)))
    lam_inits = jnp.asarray([0.8 - 0.6 * math.exp(-0.3 * l) for l in range(L)], F32)
    lam_tab = jnp.stack([pad64(lam_q1), pad64(lam_k1), pad64(lam_q2), pad64(lam_k2),
                         jnp.broadcast_to(lam_inits[:, None], (L, LANES)),
                         jnp.zeros((L, LANES), F32), jnp.zeros((L, LANES), F32), jnp.zeros((L, LANES), F32)],
                        axis=1)
    cache_da_k_t = jnp.swapaxes(cache_da_k, -1, -2)
    cache_na_k_t = jnp.swapaxes(cache_na_k, -1, -2)
    cache_na_v_t = jnp.swapaxes(cache_na_v, -1, -2)
    na_bias = _na_bias_tables(rpb)
    rope_tab = _rope_tables()
    bd = _head_block_diag()

    mods_all = _modulation(cond8, w_mod, b_mod)[:, :N_SEG].reshape(L, N_SEG, N_MOD, D_MODEL)

    caches = None
    for l in range(L):
        mods = mods_all[l]
        qkv, gates, *caches = _in_projection(l, x_ctx, x_lat, mods, g_norm1, w_in_b, head_gains, bd, rope_tab,
                                             caches)
        o_ctx = _attention_ctx(l, qkv, lam_tab, g_subln).reshape(1, SEG_ROWS, D_MODEL)
        o_da = _attention_da(l, qkv, cache_da_k_t, cache_da_v, lam_tab, g_subln)
        o_na = _attention_na(l, qkv, cache_na_k_t, cache_na_v_t, na_bias)
        x_ctx, x_lat = _output_stage(l, x_ctx, x_lat, mods, o_ctx, o_da, o_na, gates, wa_b, wb_b, wo_b, g_norm2,
                                     w_router_t, b_router_c, wg_all, wu_all, wd_all)

    y_prompt = x_ctx.reshape(BATCH, SEQ, D_MODEL)
    y_sample = x_lat
    da_k_t, new_da_v, na_k_t, na_v_t = caches
    return (y_prompt, y_sample, jnp.swapaxes(da_k_t, -1, -2), new_da_v, jnp.swapaxes(na_k_t, -1, -2),
            jnp.swapaxes(na_v_t, -1, -2))
```

```python
import functools
import math

import numpy as np
import jax
import jax.numpy as jnp
from jax import lax
from jax.experimental import pallas as pl
from jax.experimental.pallas import tpu as pltpu

D_MODEL = 1024
BATCH = 16
SEQ = 256
DEPTH = 4
DEC_BATCH = 4
DEC_SEQ = 4096
PAST_LEN = 512
GRID_W = 64
GRID_H = DEC_SEQ // GRID_W
HEAD_DIM = 64
DA_HEADS = 4
DA_WIDTH = 512
NA_HEADS = 8
NA_WIDTH = 512
NA_ROWS = 8
NA_COLS = 16
IN_COLS = 3 * DA_WIDTH + 3 * NA_WIDTH + 2 * D_MODEL
ROPE_BASE = 10000.0
ROPE_PAIRS = HEAD_DIM // 4
N_EXPERTS = 16
N_GROUPS = 4
EXPERTS_PER_GROUP = 4
D_EXPERT = 256
N_MOD = 6
EPS = 1e-6
ATTN_SCALE = HEAD_DIM ** -0.5
Q_PRESCALE = ATTN_SCALE * math.log2(math.e)
LOG2E = math.log2(math.e)

N_SEG = 1 + DEC_BATCH
SEG_ROWS = DEC_SEQ
LANES = 128
CHUNK = 512

VMEM_LIMIT = 56 * 1024 * 1024

F32 = jnp.float32
BF16 = jnp.bfloat16


def _dot(a, b):
    return jnp.dot(a, b, preferred_element_type=F32)


def _dot_nt(a, b):
    return lax.dot_general(a, b, (((1,), (1,)), ((), ())), preferred_element_type=F32)


def _params(*sem):
    return pltpu.CompilerParams(dimension_semantics=sem, vmem_limit_bytes=VMEM_LIMIT)


def _ctx_rows_spec(tm, width):
    nt = SEG_ROWS // tm
    return pl.BlockSpec((1, tm, width), lambda s, i: (0, jnp.where(s == 0, i, nt - 1), 0))


def _lat_rows_spec(tm, width):
    return pl.BlockSpec((1, tm, width), lambda s, i: (jnp.maximum(s - 1, 0), jnp.where(s == 0, 0, i), 0))


def _mod_kernel(cond_ref, w_ref, b_ref, o_ref):
    c = cond_ref[...]
    sc = (c * jax.nn.sigmoid(c)).astype(BF16)
    o_ref[0] = _dot(sc, w_ref[0].astype(BF16)) + b_ref[0]


def _modulation(cond8, w_mod, b_mod):
    tn = 1536
    return pl.pallas_call(
        _mod_kernel,
        out_shape=jax.ShapeDtypeStruct((DEPTH, 8, N_MOD * D_MODEL), F32),
        grid=(DEPTH, N_MOD * D_MODEL // tn),
        in_specs=[
            pl.BlockSpec((8, D_MODEL), lambda l, j: (0, 0)),
            pl.BlockSpec((1, D_MODEL, tn), lambda l, j: (l, 0, j)),
            pl.BlockSpec((1, 1, tn), lambda l, j: (l, 0, j)),
        ],
        out_specs=pl.BlockSpec((1, 8, tn), lambda l, j: (l, 0, j)),
        compiler_params=_params("arbitrary", "arbitrary"),
        name="modulation",
    )(cond8, w_mod, b_mod.reshape(DEPTH, 1, N_MOD * D_MODEL))


def _store_heads_t(dst_ref, b, t, rows):
    tt = t[rows].T
    lead = dst_ref.shape[2:-2]
    for h in range(CHUNK // HEAD_DIM):
        idx = np.unravel_index(h, lead)
        dst_ref[(b, 0) + tuple(int(i) for i in idx)] = tt[h * HEAD_DIM:(h + 1) * HEAD_DIM, :]


def _inproj_kernel(xc_ref, xl_ref, mod_ref, g1_ref, w_ref, hg_ref, bd_ref, rope_ref, *rest):
    qkv_ref, gate_ref, cdk_ref, cdv_ref, cnk_ref, cnv_ref = rest[-6:]
    s = pl.program_id(0)
    x = jnp.where(s == 0, xc_ref[0], xl_ref[0])
    ms = jnp.mean(x * x, axis=-1, keepdims=True)
    y = x * lax.rsqrt(ms + EPS) * g1_ref[0]
    mod = mod_ref[0]
    h = (y * (1.0 + mod[1:2]) + mod[0:1]).astype(BF16)

    rope = rope_ref[0]
    cos4 = jnp.concatenate([rope[:, 0:LANES]] * 4, axis=1)
    sin_up4 = jnp.concatenate([rope[:, LANES:2 * LANES]] * 4, axis=1)
    sin_dn4 = jnp.concatenate([rope[:, 2 * LANES:3 * LANES]] * 4, axis=1)

    def head_norm(acc, row):
        sq = (acc * acc).astype(BF16)
        half = CHUNK // 2
        ssum = jnp.concatenate([_dot(sq[:, :half], bd_ref[...]), _dot(sq[:, half:], bd_ref[...])], axis=1)
        return acc * lax.rsqrt(ssum * (1.0 / HEAD_DIM) + EPS) * hg_ref[0, row:row + 1, :]

    def rope_rot(t):
        return (t * cos4 + pltpu.roll(t, CHUNK - ROPE_PAIRS, 1) * sin_up4
                + pltpu.roll(t, ROPE_PAIRS, 1) * sin_dn4)

    def proj(c):
        return _dot(h, w_ref[0, :, c * CHUNK:(c + 1) * CHUNK])

    is_ctx = s == 0
    batches = [(b, slice(b * SEQ, (b + 1) * SEQ)) for b in range(x.shape[0] // SEQ)]

    q = rope_rot(head_norm(proj(0), 0)) * Q_PRESCALE
    qkv_ref[0, :, 0:CHUNK] = q.astype(BF16)

    k = rope_rot(head_norm(proj(1), 1))
    qkv_ref[0, :, CHUNK:2 * CHUNK] = k.astype(BF16)

    @pl.when(is_ctx)
    def _():
        for b, rows in batches:
            _store_heads_t(cdk_ref, b, k, rows)

    v = proj(2)
    qkv_ref[0, :, 2 * CHUNK:3 * CHUNK] = v.astype(BF16)

    @pl.when(is_ctx)
    def _():
        for b, rows in batches:
            for h in range(DA_HEADS):
                cdv_ref[b, 0, h] = v[rows, h * LANES:(h + 1) * LANES]

    q = head_norm(proj(3), 2) * Q_PRESCALE
    qkv_ref[0, :, 3 * CHUNK:4 * CHUNK] = q.astype(BF16)

    k = head_norm(proj(4), 3)
    qkv_ref[0, :, 4 * CHUNK:5 * CHUNK] = k.astype(BF16)

    @pl.when(is_ctx)
    def _():
        for b, rows in batches:
            _store_heads_t(cnk_ref, b, k, rows)

    v = proj(5)
    qkv_ref[0, :, 5 * CHUNK:6 * CHUNK] = v.astype(BF16)

    @pl.when(is_ctx)
    def _():
        for b, rows in batches:
            _store_heads_t(cnv_ref, b, v, rows)

    for c in range(4):
        g = proj(6 + c)
        gate_ref[0, :, c * CHUNK:(c + 1) * CHUNK] = g.astype(BF16)


CACHE_SHAPES = (
    (BATCH, DEPTH, 2, DA_HEADS, HEAD_DIM, SEQ),
    (BATCH, DEPTH, DA_HEADS, SEQ, 2 * HEAD_DIM),
    (BATCH, DEPTH, NA_HEADS, HEAD_DIM, SEQ),
    (BATCH, DEPTH, NA_HEADS, HEAD_DIM, SEQ),
)


def _in_projection(l, x_ctx, x_lat, mods, g_norm1, w_in_b, head_gains, bd, rope_tab, caches):
    tm = 512
    nt = SEG_ROWS // tm
    nb = tm // SEQ

    def cache_spec(shape):
        block = (nb, 1) + shape[2:]
        zeros = (0,) * (len(shape) - 2)
        return pl.BlockSpec(block, lambda s, i: (jnp.where(s == 0, i, nt - 1), l) + zeros)

    chained = caches is not None
    cache_in = list(caches) if chained else []
    n_in = 8
    return pl.pallas_call(
        _inproj_kernel,
        out_shape=(
            jax.ShapeDtypeStruct((N_SEG, SEG_ROWS, 6 * CHUNK), BF16),
            jax.ShapeDtypeStruct((N_SEG, SEG_ROWS, 2 * D_MODEL), BF16),
        ) + tuple(jax.ShapeDtypeStruct(shape, F32) for shape in CACHE_SHAPES),
        grid=(N_SEG, nt),
        in_specs=[
            _ctx_rows_spec(tm, D_MODEL),
            _lat_rows_spec(tm, D_MODEL),
            pl.BlockSpec((1, N_MOD, D_MODEL), lambda s, i: (s, 0, 0)),
            pl.BlockSpec((1, 1, D_MODEL), lambda s, i: (l, 0, 0)),
            pl.BlockSpec((1, D_MODEL, IN_COLS), lambda s, i: (l, 0, 0), pipeline_mode=pl.Buffered(1)),
            pl.BlockSpec((1, 4, CHUNK), lambda s, i: (l, 0, 0)),
            pl.BlockSpec((CHUNK // 2, CHUNK // 2), lambda s, i: (0, 0)),
            pl.BlockSpec((1, tm, 3 * LANES), lambda s, i: (jnp.minimum(s, 1), i, 0)),
        ] + [pl.BlockSpec(memory_space=pl.ANY) for _ in cache_in],
        out_specs=(
            pl.BlockSpec((1, tm, 6 * CHUNK), lambda s, i: (s, i, 0)),
            pl.BlockSpec((1, tm, 2 * D_MODEL), lambda s, i: (s, i, 0)),
        ) + tuple(cache_spec(shape) for shape in CACHE_SHAPES),
        input_output_aliases={n_in + j: 2 + j for j in range(len(cache_in))},
        compiler_params=_params("arbitrary", "arbitrary"),
        name="in_projection",
    )(x_ctx, x_lat, mods, g_norm1, w_in_b, head_gains, bd, rope_tab, *cache_in)


def _lam_value(lam_ref):
    lp = lam_ref[0]
    s1 = jnp.sum(lp[0:1] * lp[1:2], axis=-1, keepdims=True)
    s2 = jnp.sum(lp[2:3] * lp[3:4], axis=-1, keepdims=True)
    lam_init = lp[4:5, 0:1]
    return jnp.exp(s1) - jnp.exp(s2) + lam_init, 1.0 - lam_init


def _half_masks():
    lane = lax.broadcasted_iota(jnp.int32, (1, LANES), 1)
    return (lane < HEAD_DIM, lane >= HEAD_DIM)


def _pick_head(mask, t):
    return jnp.where(mask, t, jnp.zeros_like(t))


def _sub_layer_norm(o, gsub, one_minus):
    ms = jnp.mean(o * o, axis=-1, keepdims=True)
    return o * lax.rsqrt(ms + EPS) * gsub * one_minus


def _attn_ctx_kernel(qkv_ref, lam_ref, gsub_ref, o_ref):
    lam, one_minus = _lam_value(lam_ref)
    masks = _half_masks()
    qda = qkv_ref[0, :, 0:CHUNK]
    qna = qkv_ref[0, :, 3 * CHUNK:4 * CHUNK]
    gsub = gsub_ref[0]

    def kv(c0):
        off = c0 + CHUNK if c0 < 2 * CHUNK else c0 + 2 * CHUNK
        return qkv_ref[0, :, off:off + LANES]

    def softmax_parts(sc):
        m = jnp.max(sc, axis=-1, keepdims=True)
        e = jnp.exp2(sc - m)
        return e, 1.0 / jnp.sum(e, axis=-1, keepdims=True)

    for h in range(DA_HEADS):
        p, sub = divmod(h, 2)
        q1 = _pick_head(masks[sub], qda[:, p * LANES:(p + 1) * LANES])
        q2 = _pick_head(masks[sub], qda[:, 2 * LANES + p * LANES:2 * LANES + (p + 1) * LANES])
        e1, r1 = softmax_parts(_dot_nt(q1, kv(p * LANES)))
        e2, r2 = softmax_parts(_dot_nt(q2, kv(2 * LANES + p * LANES)))
        a = e1 * r1 - e2 * (lam * r2)
        o = _dot(a.astype(BF16), kv(CHUNK + h * LANES))
        o_ref[:, h * LANES:(h + 1) * LANES] = _sub_layer_norm(o, gsub, one_minus).astype(BF16)

    for p in range(NA_HEADS // 2):
        kp = kv(2 * CHUNK + p * LANES)
        vp = kv(3 * CHUNK + p * LANES)
        both = None
        for sub in range(2):
            qm = _pick_head(masks[sub], qna[:, p * LANES:(p + 1) * LANES])
            e, r = softmax_parts(_dot_nt(qm, kp))
            o = _dot(e.astype(BF16), vp) * r
            both = o if sub == 0 else jnp.where(masks[1], o, both)
        o_ref[:, DA_WIDTH + p * LANES:DA_WIDTH + (p + 1) * LANES] = both.astype(BF16)


def _attention_ctx(l, qkv, lam_tab, g_subln):
    return pl.pallas_call(
        _attn_ctx_kernel,
        out_shape=jax.ShapeDtypeStruct((SEG_ROWS, D_MODEL), BF16),
        grid=(BATCH,),
        in_specs=[
            pl.BlockSpec((1, SEQ, 6 * CHUNK), lambda b: (0, b, 0)),
            pl.BlockSpec((1, 8, LANES), lambda b: (l, 0, 0)),
            pl.BlockSpec((1, 1, LANES), lambda b: (l, 0, 0)),
        ],
        out_specs=pl.BlockSpec((SEQ, D_MODEL), lambda b: (b, 0)),
        compiler_params=_params("arbitrary"),
        name="attention_ctx",
    )(qkv, lam_tab, g_subln)


DA_KEYS = DEC_SEQ + PAST_LEN
DA_KCHUNK = 1024
DA_TQ = 256


def _pair_heads(head_ref, lead, first):
    a = head_ref[lead + (first,)]
    b = head_ref[lead + (first + 1,)]
    return jnp.concatenate([a, b], axis=0).T.astype(BF16)


def _attn_da_kernel(q_ref, k_ref, v_ref, kc_ref, vc_ref, lam_ref, gsub_ref, o_ref, vt_ref, kcs_ref, e1_ref, e2_ref):
    chunks = [slice(r, r + DA_KCHUNK) for r in range(0, DEC_SEQ, DA_KCHUNK)] + [slice(DEC_SEQ, DA_KEYS)]
    n_chunks = len(chunks)

    @pl.when(pl.program_id(1) == 0)
    def _():
        for r in range(0, DEC_SEQ, PAST_LEN):
            rows = slice(r, r + PAST_LEN)
            vt_ref[:, rows] = v_ref[0, rows, :].astype(F32).T.astype(BF16)
        for h in range(DA_HEADS):
            vt_ref[h * LANES:(h + 1) * LANES, DEC_SEQ:DA_KEYS] = vc_ref[0, 0, h].T.astype(BF16)
        for m in range(2):
            for p in range(DA_HEADS // 2):
                c0 = m * 2 * LANES + p * LANES
                kcs_ref[:, c0:c0 + LANES] = _pair_heads(kc_ref, (0, 0, m), 2 * p)

    lam, one_minus = _lam_value(lam_ref)
    masks = _half_masks()
    q = q_ref[0]
    gsub = gsub_ref[0]

    def key_chunk(c, cols):
        if chunks[c].start < DEC_SEQ:
            return k_ref[0, chunks[c], cols]
        return kcs_ref[:, cols]

    def score_map(qm, cols, e_ref, weight):
        maxes, sums = [], []
        for c in range(n_chunks):
            s = _dot_nt(key_chunk(c, cols), qm)
            mc = jnp.max(s, axis=0, keepdims=True)
            e = jnp.exp2(s - mc)
            sums.append(jnp.sum(e, axis=0, keepdims=True))
            maxes.append(mc)
            e_ref[chunks[c], :] = e.astype(BF16)
        m = functools.reduce(jnp.maximum, maxes)
        facs = [jnp.exp2(mc - m) for mc in maxes]
        tot = functools.reduce(lambda a, b: a + b, [lc * fc for lc, fc in zip(sums, facs)])
        w = weight / tot
        return [(fc * w).astype(BF16) for fc in facs]

    for h in range(DA_HEADS):
        p, sub = divmod(h, 2)
        c1 = slice(p * LANES, (p + 1) * LANES)
        c2 = slice(2 * LANES + p * LANES, 2 * LANES + (p + 1) * LANES)
        f1 = score_map(_pick_head(masks[sub], q[:, c1]), c1, e1_ref, 1.0)
        f2 = score_map(_pick_head(masks[sub], q[:, c2]), c2, e2_ref, lam)
        o_t = jnp.zeros((LANES, DA_TQ), F32)
        for c, rows in enumerate(chunks):
            a = e1_ref[rows, :] * f1[c] - e2_ref[rows, :] * f2[c]
            o_t = o_t + _dot(vt_ref[h * LANES:(h + 1) * LANES, rows], a)
        o_ref[0, :, h * LANES:(h + 1) * LANES] = _sub_layer_norm(o_t.T, gsub, one_minus).astype(BF16)


def _attention_da(l, qkv, ctx_k, ctx_v, lam_tab, g_subln):
    tq = DA_TQ
    return pl.pallas_call(
        _attn_da_kernel,
        out_shape=jax.ShapeDtypeStruct((DEC_BATCH, DEC_SEQ, DA_WIDTH), BF16),
        grid=(DEC_BATCH, DEC_SEQ // tq),
        in_specs=[
            pl.BlockSpec((1, tq, CHUNK), lambda b, i: (b + 1, i, 0)),
            pl.BlockSpec((1, DEC_SEQ, CHUNK), lambda b, i: (b + 1, 0, 1), pipeline_mode=pl.Buffered(1)),
            pl.BlockSpec((1, DEC_SEQ, CHUNK), lambda b, i: (b + 1, 0, 2), pipeline_mode=pl.Buffered(1)),
            pl.BlockSpec((1, 1, 2, DA_HEADS, HEAD_DIM, PAST_LEN), lambda b, i: (b, l, 0, 0, 0, 0)),
            pl.BlockSpec((1, 1, DA_HEADS, PAST_LEN, 2 * HEAD_DIM), lambda b, i: (b, l, 0, 0, 0)),
            pl.BlockSpec((1, 8, LANES), lambda b, i: (l, 0, 0)),
            pl.BlockSpec((1, 1, LANES), lambda b, i: (l, 0, 0)),
        ],
        out_specs=pl.BlockSpec((1, tq, DA_WIDTH), lambda b, i: (b, i, 0)),
        scratch_shapes=[
            pltpu.VMEM((DA_WIDTH, DA_KEYS), BF16),
            pltpu.VMEM((PAST_LEN, DA_WIDTH), BF16),
            pltpu.VMEM((DA_KEYS, tq), BF16),
            pltpu.VMEM((DA_KEYS, tq), BF16),
        ],
        compiler_params=_params("arbitrary", "arbitrary"),
        name="attention_da",
    )(qkv, qkv, qkv, ctx_k, ctx_v, lam_tab, g_subln)


NA_QROWS = 8
NA_KROWS = 16


def _attn_na_kernel(q_ref, k_ref, v_ref, kc_ref, vc_ref, bias_ref, o_ref, kcs_ref, vcs_ref):
    i = pl.program_id(1)

    @pl.when(i == 0)
    def _():
        for p in range(NA_HEADS // 2):
            kcs_ref[:, p * LANES:(p + 1) * LANES] = _pair_heads(kc_ref, (0, 0), 2 * p)
            vcs_ref[:, p * LANES:(p + 1) * LANES] = _pair_heads(vc_ref, (0, 0), 2 * p)

    masks = _half_masks()
    q = q_ref[0]
    row0 = jnp.clip(i * NA_QROWS - NA_ROWS // 2, 0, GRID_H - NA_KROWS)
    start = pl.multiple_of(row0 * GRID_W, 256)
    nk = NA_KROWS * GRID_W

    for p in range(NA_HEADS // 2):
        c0 = p * LANES
        kw = k_ref[0, pl.ds(start, nk), c0:c0 + LANES]
        vw = v_ref[0, pl.ds(start, nk), c0:c0 + LANES]
        kc = kcs_ref[:, c0:c0 + LANES]
        vc = vcs_ref[:, c0:c0 + LANES]
        both = None
        for sub in range(2):
            qm = _pick_head(masks[sub], q[:, c0:c0 + LANES])
            s_loc = _dot_nt(qm, kw) + bias_ref[0, 0, 2 * p + sub].astype(F32)
            s_ctx = _dot_nt(qm, kc)
            m = jnp.maximum(jnp.max(s_loc, axis=-1, keepdims=True), jnp.max(s_ctx, axis=-1, keepdims=True))
            e_loc = jnp.exp2(s_loc - m)
            e_ctx = jnp.exp2(s_ctx - m)
            tot = jnp.sum(e_loc, axis=-1, keepdims=True) + jnp.sum(e_ctx, axis=-1, keepdims=True)
            o = (_dot(e_loc.astype(BF16), vw) + _dot(e_ctx.astype(BF16), vc)) * (1.0 / tot)
            both = o if sub == 0 else jnp.where(masks[1], o, both)
        o_ref[0, :, c0:c0 + LANES] = both.astype(BF16)


def _attention_na(l, qkv, ctx_k, ctx_v, bias):
    tq = NA_QROWS * GRID_W
    n_i = DEC_SEQ // tq

    def bias_idx(b, i):
        return (l, jnp.where(i == 0, 0, jnp.where(i == n_i - 1, 2, 1)), 0, 0, 0)

    return pl.pallas_call(
        _attn_na_kernel,
        out_shape=jax.ShapeDtypeStruct((DEC_BATCH, DEC_SEQ, NA_WIDTH), BF16),
        grid=(DEC_BATCH, n_i),
        in_specs=[
            pl.BlockSpec((1, tq, CHUNK), lambda b, i: (b + 1, i, 3)),
            pl.BlockSpec((1, DEC_SEQ, CHUNK), lambda b, i: (b + 1, 0, 4), pipeline_mode=pl.Buffered(1)),
            pl.BlockSpec((1, DEC_SEQ, CHUNK), lambda b, i: (b + 1, 0, 5), pipeline_mode=pl.Buffered(1)),
            pl.BlockSpec((1, 1, NA_HEADS, HEAD_DIM, PAST_LEN), lambda b, i: (b, l, 0, 0, 0)),
            pl.BlockSpec((1, 1, NA_HEADS, HEAD_DIM, PAST_LEN), lambda b, i: (b, l, 0, 0, 0)),
            pl.BlockSpec((1, 1, NA_HEADS, tq, NA_KROWS * GRID_W), bias_idx),
        ],
        out_specs=pl.BlockSpec((1, tq, NA_WIDTH), lambda b, i: (b, i, 0)),
        scratch_shapes=[
            pltpu.VMEM((PAST_LEN, NA_WIDTH), BF16),
            pltpu.VMEM((PAST_LEN, NA_WIDTH), BF16),
        ],
        compiler_params=_params("arbitrary", "arbitrary"),
        name="attention_na",
    )(qkv, qkv, qkv, ctx_k, ctx_v, bias)


def _na_bias_tables(rpb):
    qr = np.arange(NA_QROWS)
    qc = np.arange(GRID_W)
    kr = np.arange(NA_KROWS)
    kc = np.arange(GRID_W)
    c0 = np.clip(qc - NA_COLS // 2, 0, GRID_W - NA_COLS)
    col_ok = (kc[None, :] >= c0[:, None]) & (kc[None, :] < c0[:, None] + NA_COLS)
    cpad = GRID_W - NA_COLS
    rpb_c = jnp.pad(rpb, ((0, 0), (0, 0), (0, 0), (cpad, cpad)))
    by_col = jnp.stack([rpb_c[..., GRID_W - 1 - c:2 * GRID_W - 1 - c] for c in range(GRID_W)], axis=-2)
    by_col = jnp.where(col_ok, by_col * LOG2E, -jnp.inf)

    plans = []
    for i in (0, 1, GRID_H // NA_QROWS - 1):
        rows_q = NA_QROWS * i + qr
        row0_k = min(max(NA_QROWS * i - NA_ROWS // 2, 0), GRID_H - NA_KROWS)
        rows_k = row0_k + kr
        r0 = np.clip(rows_q - NA_ROWS // 2, 0, GRID_H - NA_ROWS)
        row_ok = (rows_k[None, :] >= r0[:, None]) & (rows_k[None, :] < r0[:, None] + NA_ROWS)
        slab = rows_k[None, :] - rows_q[:, None] + NA_ROWS - 1
        plans.append([[int(slab[r, c]) if row_ok[r, c] else None for c in range(NA_KROWS)]
                      for r in range(NA_QROWS)])

    heads_per_step = NA_HEADS // 2

    def tile_kernel(tab_ref, o_ref):
        kind = pl.program_id(1)
        for which, plan in enumerate(plans):
            @pl.when(kind == which)
            def _(plan=plan):
                for h in range(heads_per_step):
                    for r in range(NA_QROWS):
                        for c in range(NA_KROWS):
                            if plan[r][c] is None:
                                blk = jnp.full((GRID_W, GRID_W), -jnp.inf, BF16)
                            else:
                                blk = tab_ref[0, h, plan[r][c]].astype(BF16)
                            o_ref[0, 0, h, r * GRID_W:(r + 1) * GRID_W, c * GRID_W:(c + 1) * GRID_W] = blk

    n_slab = 2 * NA_ROWS - 1
    return pl.pallas_call(
        tile_kernel,
        out_shape=jax.ShapeDtypeStruct((DEPTH, len(plans), NA_HEADS, NA_QROWS * GRID_W, NA_KROWS * GRID_W), BF16),
        grid=(DEPTH, len(plans), NA_HEADS // heads_per_step),
        in_specs=[pl.BlockSpec((1, heads_per_step, n_slab, GRID_W, GRID_W), lambda l, t, h: (l, h, 0, 0, 0))],
        out_specs=pl.BlockSpec((1, 1, heads_per_step, NA_QROWS * GRID_W, NA_KROWS * GRID_W),
                               lambda l, t, h: (l, t, h, 0, 0)),
        compiler_params=_params("arbitrary", "arbitrary", "arbitrary"),
        name="na_bias_tiles",
    )(by_col)


def _first_max_of_four(vals):
    a, b, c, d = vals
    m = jnp.maximum(jnp.maximum(a, b), jnp.maximum(c, d))
    idx = jnp.where(a == m, 0.0, jnp.where(b == m, 1.0, jnp.where(c == m, 2.0, 3.0)))
    return m, idx


MOE_CHUNK = EXPERTS_PER_GROUP * D_EXPERT


def _out_kernel(xc_ref, xl_ref, mod_ref, oc_ref, oda_ref, ona_ref, gate_ref, wa_ref, wb_ref, wo_ref, g2_ref,
                wrt_ref, br_ref, wg_ref, wu_ref, wd_ref, yc_ref, yl_ref):
    is_ctx = pl.program_id(0) == 0
    oc = oc_ref[0]
    oa = jnp.where(is_ctx, oc[:, :DA_WIDTH], oda_ref[0])
    ob = jnp.where(is_ctx, oc[:, DA_WIDTH:], ona_ref[0])
    ya = _dot(oa, wa_ref[0])
    yb = _dot(ob, wb_ref[0])
    g = jax.nn.sigmoid(gate_ref[0].astype(F32))
    mixed = (g[:, :D_MODEL] * ya + g[:, D_MODEL:] * yb).astype(BF16)
    mod = mod_ref[0]
    x1 = jnp.where(is_ctx, xc_ref[0], xl_ref[0]) + mod[2:3] * _dot(mixed, wo_ref[0])

    ms = jnp.mean(x1 * x1, axis=-1, keepdims=True)
    h2 = x1 * lax.rsqrt(ms + EPS) * g2_ref[0] * (1.0 + mod[4:5]) + mod[3:4]
    h2_hi = h2.astype(BF16)

    h2_lo = (h2 - h2_hi.astype(F32)).astype(BF16)
    wr = wrt_ref[...]
    wr_hi = wr.astype(BF16)
    wr_lo = (wr - wr_hi.astype(F32)).astype(BF16)
    by_hi = _dot_nt(jnp.concatenate([wr_hi, wr_lo], axis=0), h2_hi)
    logits = by_hi[:N_EXPERTS] + (by_hi[N_EXPERTS:] + _dot_nt(wr_hi, h2_lo))
    scores = jax.nn.sigmoid(logits)
    sel = scores + br_ref[...]
    score_rows = [scores[e:e + 1] for e in range(N_EXPERTS)]
    sel_rows = [sel[e:e + 1] for e in range(N_EXPERTS)]

    neg = -jnp.inf
    best = None
    for grp in range(N_GROUPS):
        vals = sel_rows[grp * EXPERTS_PER_GROUP:(grp + 1) * EXPERTS_PER_GROUP]
        m1, i1 = _first_max_of_four(vals)
        rest = [jnp.where(i1 == float(j), neg, v) for j, v in enumerate(vals)]
        m2, i2 = _first_max_of_four(rest)
        cand = (m1 + m2, i1 + float(grp * EXPERTS_PER_GROUP), i2 + float(grp * EXPERTS_PER_GROUP))
        if best is None:
            best = cand
        else:
            upd = cand[0] > best[0]
            best = tuple(jnp.where(upd, n, o) for n, o in zip(cand, best))
    _, e1, e2 = best
    hits1 = [e1 == float(e) for e in range(N_EXPERTS)]
    hits2 = [e2 == float(e) for e in range(N_EXPERTS)]
    s1 = functools.reduce(lambda a, b: a + b, [jnp.where(h, r, 0.0) for h, r in zip(hits1, score_rows)])
    s2 = functools.reduce(lambda a, b: a + b, [jnp.where(h, r, 0.0) for h, r in zip(hits2, score_rows)])
    den = s1 + s2
    w1 = s1 / den
    w2 = s2 / den
    rows = [jnp.where(h1, w1, 0.0) + jnp.where(h2_, w2, 0.0) for h1, h2_ in zip(hits1, hits2)]
    rows.append(jnp.zeros((LANES - N_EXPERTS, rows[0].shape[1]), F32))
    comb = jnp.concatenate(rows, axis=0).T

    acc = jnp.zeros(x1.shape, F32)
    for c in range(N_GROUPS):
        parts = []
        for j in range(EXPERTS_PER_GROUP):
            e = c * EXPERTS_PER_GROUP + j
            g = _dot(h2_hi, wg_ref[0, e])
            u = _dot(h2_hi, wu_ref[0, e])
            parts.append(g * jax.nn.sigmoid(g) * u * comb[:, e:e + 1])
        hs = jnp.concatenate(parts, axis=1).astype(BF16)
        acc = acc + _dot(hs, wd_ref[0, c * MOE_CHUNK:(c + 1) * MOE_CHUNK, :])
    out = x1 + mod[5:6] * acc

    @pl.when(is_ctx)
    def _():
        yc_ref[0] = out

    @pl.when(jnp.logical_not(is_ctx))
    def _():
        yl_ref[0] = out


def _output_stage(l, x_ctx, x_lat, mods, o_ctx, o_da, o_na, gates, wa_b, wb_b, wo_b, g_norm2, w_router_t,
                  b_router_c, wg_all, wu_all, wd_all):
    tm = 256
    nt = SEG_ROWS // tm
    hidden = N_EXPERTS * D_EXPERT
    once = pl.Buffered(1)
    return pl.pallas_call(
        _out_kernel,
        out_shape=(
            jax.ShapeDtypeStruct((1, SEG_ROWS, D_MODEL), F32),
            jax.ShapeDtypeStruct((DEC_BATCH, DEC_SEQ, D_MODEL), F32),
        ),
        grid=(N_SEG, nt),
        in_specs=[
            _ctx_rows_spec(tm, D_MODEL),
            _lat_rows_spec(tm, D_MODEL),
            pl.BlockSpec((1, N_MOD, D_MODEL), lambda s, i: (s, 0, 0)),
            _ctx_rows_spec(tm, D_MODEL),
            _lat_rows_spec(tm, DA_WIDTH),
            _lat_rows_spec(tm, NA_WIDTH),
            pl.BlockSpec((1, tm, 2 * D_MODEL), lambda s, i: (s, i, 0)),
            pl.BlockSpec((1, DA_WIDTH, D_MODEL), lambda s, i: (l, 0, 0)),
            pl.BlockSpec((1, NA_WIDTH, D_MODEL), lambda s, i: (l, 0, 0)),
            pl.BlockSpec((1, D_MODEL, D_MODEL), lambda s, i: (l, 0, 0)),
            pl.BlockSpec((1, 1, D_MODEL), lambda s, i: (l, 0, 0)),
            pl.BlockSpec((N_EXPERTS, D_MODEL), lambda s, i: (0, 0)),
            pl.BlockSpec((N_EXPERTS, 1), lambda s, i: (0, 0)),
            pl.BlockSpec((1, N_EXPERTS, D_MODEL, D_EXPERT), lambda s, i: (l, 0, 0, 0), pipeline_mode=once),
            pl.BlockSpec((1, N_EXPERTS, D_MODEL, D_EXPERT), lambda s, i: (l, 0, 0, 0), pipeline_mode=once),
            pl.BlockSpec((1, hidden, D_MODEL), lambda s, i: (l, 0, 0), pipeline_mode=once),
        ],
        out_specs=(_ctx_rows_spec(tm, D_MODEL), _lat_rows_spec(tm, D_MODEL)),
        compiler_params=_params("arbitrary", "arbitrary"),
        name="output_experts",
    )(x_ctx, x_lat, mods, o_ctx, o_da, o_na, gates, wa_b, wb_b, wo_b, g_norm2, w_router_t, b_router_c,
      wg_all, wu_all, wd_all)


def _rope_tables():
    t = np.arange(DEC_SEQ)
    pos = np.stack([t // GRID_W, t % GRID_W], axis=-1).astype(np.float32)
    inv_freq = (ROPE_BASE ** (-np.arange(ROPE_PAIRS, dtype=np.float32) / ROPE_PAIRS)).astype(np.float32)
    ang = pos[:, :, None] * inv_freq
    lane = np.arange(LANES)
    axis = (lane % HEAD_DIM) // (2 * ROPE_PAIRS)
    pair = lane % ROPE_PAIRS
    second = ((lane // ROPE_PAIRS) % 2).astype(bool)
    a = ang[:, axis, pair]
    cos = np.cos(a)
    sin = np.sin(a)
    lat = np.concatenate([cos, np.where(second, 0.0, -sin), np.where(second, sin, 0.0)], axis=1)
    ident = np.concatenate([np.ones_like(cos), np.zeros_like(cos), np.zeros_like(cos)], axis=1)
    return jnp.asarray(np.stack([ident, lat]).astype(np.float32))


def _head_block_diag():
    r = np.arange(CHUNK // 2) // HEAD_DIM
    return jnp.asarray((r[:, None] == r[None, :]).astype(np.float32), dtype=BF16)


def kernel(x_prompt, x_sample, cache_da_k, cache_da_v, cache_na_k, cache_na_v, c, c_ctx, w_mod, b_mod, g_norm1,
           g_norm2, w_in, g_q_da, g_k_da, g_q_na, g_k_na, lam_q1, lam_k1, lam_q2, lam_k2, g_subln, rpb, w_br_a,
           w_br_b, w_out, w_router, b_router, w_gate, w_up, w_down):
    L = DEPTH
    x_ctx = x_prompt.reshape(1, SEG_ROWS, D_MODEL)
    x_lat = x_sample
    cond8 = jnp.concatenate([c_ctx[None], c, jnp.zeros((8 - N_SEG, D_MODEL), F32)], axis=0)
    w_in_b = w_in.astype(BF16)
    wa_b = w_br_a.astype(BF16)
    wb_b = w_br_b.astype(BF16)
    wo_b = w_out.astype(BF16)
    hidden = N_EXPERTS * D_EXPERT
    wg_all = w_gate.astype(BF16)
    wu_all = w_up.astype(BF16)
    wd_all = w_down.astype(BF16).reshape(L, hidden, D_MODEL)
    w_router_t = w_router.T
    b_router_c = b_router.reshape(N_EXPERTS, 1)
    head_gains = jnp.stack([jnp.tile(g, (1, CHUNK // HEAD_DIM)) for g in (g_q_da, g_k_da, g_q_na, g_k_na)],
                           axis=1).reshape(L, 4, CHUNK)
    g_norm1 = g_norm1.reshape(L, 1, D_MODEL)
    g_norm2 = g_norm2.reshape(L, 1, D_MODEL)
    g_subln = g_subln.reshape(L, 1, LANES)
    pad64 = lambda t: jnp.pad(t, ((0, 0), (0, LANES - HEAD_DIM)))
    lam_inits = jnp.asarray([0.8 - 0.6 * math.exp(-0.3 * l) for l in range(L)], F32)
    lam_tab = jnp.stack([pad64(lam_q1), pad64(lam_k1), pad64(lam_q2), pad64(lam_k2),
                         jnp.broadcast_to(lam_inits[:, None], (L, LANES)),
                         jnp.zeros((L, LANES), F32), jnp.zeros((L, LANES), F32), jnp.zeros((L, LANES), F32)],
                        axis=1)
    cache_da_k_t = jnp.swapaxes(cache_da_k, -1, -2)
    cache_na_k_t = jnp.swapaxes(cache_na_k, -1, -2)
    cache_na_v_t = jnp.swapaxes(cache_na_v, -1, -2)
    na_bias = _na_bias_tables(rpb)
    rope_tab = _rope_tables()
    bd = _head_block_diag()

    mods_all = _modulation(cond8, w_mod, b_mod)[:, :N_SEG].reshape(L, N_SEG, N_MOD, D_MODEL)

    caches = None
    for l in range(L):
        mods = mods_all[l]
        qkv, gates, *caches = _in_projection(l, x_ctx, x_lat, mods, g_norm1, w_in_b, head_gains, bd, rope_tab,
                                             caches)
        o_ctx = _attention_ctx(l, qkv, lam_tab, g_subln).reshape(1, SEG_ROWS, D_MODEL)
        o_da = _attention_da(l, qkv, cache_da_k_t, cache_da_v, lam_tab, g_subln)
        o_na = _attention_na(l, qkv, cache_na_k_t, cache_na_v_t, na_bias)
        x_ctx, x_lat = _output_stage(l, x_ctx, x_lat, mods, o_ctx, o_da, o_na, gates, wa_b, wb_b, wo_b, g_norm2,
                                     w_router_t, b_router_c, wg_all, wu_all, wd_all)

    y_prompt = x_ctx.reshape(BATCH, SEQ, D_MODEL)
    y_sample = x_lat
    da_k_t, new_da_v, na_k_t, na_v_t = caches
    return (y_prompt, y_sample, jnp.swapaxes(da_k_t, -1, -2), new_da_v, jnp.swapaxes(na_k_t, -1, -2),
            jnp.swapaxes(na_v_t, -1, -2))
```

```python
import functools
import math

import numpy as np
import jax
import jax.numpy as jnp
from jax import lax
from jax.experimental import pallas as pl
from jax.experimental.pallas import tpu as pltpu

D_MODEL = 1024
BATCH = 16
SEQ = 256
DEPTH = 4
DEC_BATCH = 4
DEC_SEQ = 4096
PAST_LEN = 512
GRID_W = 64
GRID_H = DEC_SEQ // GRID_W
HEAD_DIM = 64
DA_HEADS = 4
DA_WIDTH = 512
NA_HEADS = 8
NA_WIDTH = 512
NA_ROWS = 8
NA_COLS = 16
IN_COLS = 3 * DA_WIDTH + 3 * NA_WIDTH + 2 * D_MODEL
ROPE_BASE = 10000.0
ROPE_PAIRS = HEAD_DIM // 4
N_EXPERTS = 16
N_GROUPS = 4
EXPERTS_PER_GROUP = 4
D_EXPERT = 256
N_MOD = 6
EPS = 1e-6
ATTN_SCALE = HEAD_DIM ** -0.5
Q_PRESCALE = ATTN_SCALE * math.log2(math.e)
LOG2E = math.log2(math.e)

N_SEG = 1 + DEC_BATCH
SEG_ROWS = DEC_SEQ
LANES = 128
CHUNK = 512

VMEM_LIMIT = 56 * 1024 * 1024

F32 = jnp.float32
BF16 = jnp.bfloat16


def _dot(a, b):
    return jnp.dot(a, b, preferred_element_type=F32)


def _dot_nt(a, b):
    return lax.dot_general(a, b, (((1,), (1,)), ((), ())), preferred_element_type=F32)


def _params(*sem):
    return pltpu.CompilerParams(dimension_semantics=sem, vmem_limit_bytes=VMEM_LIMIT)


def _ctx_rows_spec(tm, width):
    nt = SEG_ROWS // tm
    return pl.BlockSpec((1, tm, width), lambda s, i: (0, jnp.where(s == 0, i, nt - 1), 0))


def _lat_rows_spec(tm, width):
    return pl.BlockSpec((1, tm, width), lambda s, i: (jnp.maximum(s - 1, 0), jnp.where(s == 0, 0, i), 0))


def _mod_kernel(cond_ref, w_ref, b_ref, o_ref):
    c = cond_ref[...]
    sc = (c * jax.nn.sigmoid(c)).astype(BF16)
    o_ref[0] = _dot(sc, w_ref[0].astype(BF16)) + b_ref[0]


def _modulation(cond8, w_mod, b_mod):
    tn = 1536
    return pl.pallas_call(
        _mod_kernel,
        out_shape=jax.ShapeDtypeStruct((DEPTH, 8, N_MOD * D_MODEL), F32),
        grid=(DEPTH, N_MOD * D_MODEL // tn),
        in_specs=[
            pl.BlockSpec((8, D_MODEL), lambda l, j: (0, 0)),
            pl.BlockSpec((1, D_MODEL, tn), lambda l, j: (l, 0, j)),
            pl.BlockSpec((1, 1, tn), lambda l, j: (l, 0, j)),
        ],
        out_specs=pl.BlockSpec((1, 8, tn), lambda l, j: (l, 0, j)),
        compiler_params=_params("arbitrary", "arbitrary"),
        name="modulation",
    )(cond8, w_mod, b_mod.reshape(DEPTH, 1, N_MOD * D_MODEL))


def _store_heads_t(dst_ref, b, t, rows):
    tt = t[rows].T
    lead = dst_ref.shape[2:-2]
    for h in range(CHUNK // HEAD_DIM):
        idx = np.unravel_index(h, lead)
        dst_ref[(b, 0) + tuple(int(i) for i in idx)] = tt[h * HEAD_DIM:(h + 1) * HEAD_DIM, :]


def _inproj_kernel(xc_ref, xl_ref, mod_ref, g1_ref, w_ref, hg_ref, bd_ref, rope_ref, *rest):
    qkv_ref, gate_ref, cdk_ref, cdv_ref, cnk_ref, cnv_ref = rest[-6:]
    s = pl.program_id(0)
    x = jnp.where(s == 0, xc_ref[0], xl_ref[0])
    ms = jnp.mean(x * x, axis=-1, keepdims=True)
    y = x * lax.rsqrt(ms + EPS) * g1_ref[0]
    mod = mod_ref[0]
    h = (y * (1.0 + mod[1:2]) + mod[0:1]).astype(BF16)

    rope = rope_ref[0]
    cos4 = jnp.concatenate([rope[:, 0:LANES]] * 4, axis=1)
    sin_up4 = jnp.concatenate([rope[:, LANES:2 * LANES]] * 4, axis=1)
    sin_dn4 = jnp.concatenate([rope[:, 2 * LANES:3 * LANES]] * 4, axis=1)

    def head_norm(acc, row):
        sq = (acc * acc).astype(BF16)
        half = CHUNK // 2
        ssum = jnp.concatenate([_dot(sq[:, :half], bd_ref[...]), _dot(sq[:, half:], bd_ref[...])], axis=1)
        return acc * lax.rsqrt(ssum * (1.0 / HEAD_DIM) + EPS) * hg_ref[0, row:row + 1, :]

    def rope_rot(t):
        return (t * cos4 + pltpu.roll(t, CHUNK - ROPE_PAIRS, 1) * sin_up4
                + pltpu.roll(t, ROPE_PAIRS, 1) * sin_dn4)

    def proj(c):
        return _dot(h, w_ref[0, :, c * CHUNK:(c + 1) * CHUNK])

    is_ctx = s == 0
    batches = [(b, slice(b * SEQ, (b + 1) * SEQ)) for b in range(x.shape[0] // SEQ)]

    q = rope_rot(head_norm(proj(0), 0)) * Q_PRESCALE
    qkv_ref[0, :, 0:CHUNK] = q.astype(BF16)

    k = rope_rot(head_norm(proj(1), 1))
    qkv_ref[0, :, CHUNK:2 * CHUNK] = k.astype(BF16)

    @pl.when(is_ctx)
    def _():
        for b, rows in batches:
            _store_heads_t(cdk_ref, b, k, rows)

    v = proj(2)
    qkv_ref[0, :, 2 * CHUNK:3 * CHUNK] = v.astype(BF16)

    @pl.when(is_ctx)
    def _():
        for b, rows in batches:
            for h in range(DA_HEADS):
                cdv_ref[b, 0, h] = v[rows, h * LANES:(h + 1) * LANES]

    q = head_norm(proj(3), 2) * Q_PRESCALE
    qkv_ref[0, :, 3 * CHUNK:4 * CHUNK] = q.astype(BF16)

    k = head_norm(proj(4), 3)
    qkv_ref[0, :, 4 * CHUNK:5 * CHUNK] = k.astype(BF16)

    @pl.when(is_ctx)
    def _():
        for b, rows in batches:
            _store_heads_t(cnk_ref, b, k, rows)

    v = proj(5)
    qkv_ref[0, :, 5 * CHUNK:6 * CHUNK] = v.astype(BF16)

    @pl.when(is_ctx)
    def _():
        for b, rows in batches:
            _store_heads_t(cnv_ref, b, v, rows)

    for c in range(4):
        g = proj(6 + c)
        gate_ref[0, :, c * CHUNK:(c + 1) * CHUNK] = g.astype(BF16)


CACHE_SHAPES = (
    (BATCH, DEPTH, 2, DA_HEADS, HEAD_DIM, SEQ),
    (BATCH, DEPTH, DA_HEADS, SEQ, 2 * HEAD_DIM),
    (BATCH, DEPTH, NA_HEADS, HEAD_DIM, SEQ),
    (BATCH, DEPTH, NA_HEADS, HEAD_DIM, SEQ),
)


def _in_projection(l, x_ctx, x_lat, mods, g_norm1, w_in_b, head_gains, bd, rope_tab, caches):
    tm = 512
    nt = SEG_ROWS // tm
    nb = tm // SEQ

    def cache_spec(shape):
        block = (nb, 1) + shape[2:]
        zeros = (0,) * (len(shape) - 2)
        return pl.BlockSpec(block, lambda s, i: (jnp.where(s == 0, i, nt - 1), l) + zeros)

    chained = caches is not None
    cache_in = list(caches) if chained else []
    n_in = 8
    return pl.pallas_call(
        _inproj_kernel,
        out_shape=(
            jax.ShapeDtypeStruct((N_SEG, SEG_ROWS, 6 * CHUNK), BF16),
            jax.ShapeDtypeStruct((N_SEG, SEG_ROWS, 2 * D_MODEL), BF16),
        ) + tuple(jax.ShapeDtypeStruct(shape, F32) for shape in CACHE_SHAPES),
        grid=(N_SEG, nt),
        in_specs=[
            _ctx_rows_spec(tm, D_MODEL),
            _lat_rows_spec(tm, D_MODEL),
            pl.BlockSpec((1, N_MOD, D_MODEL), lambda s, i: (s, 0, 0)),
            pl.BlockSpec((1, 1, D_MODEL), lambda s, i: (l, 0, 0)),
            pl.BlockSpec((1, D_MODEL, IN_COLS), lambda s, i: (l, 0, 0), pipeline_mode=pl.Buffered(1)),
            pl.BlockSpec((1, 4, CHUNK), lambda s, i: (l, 0, 0)),
            pl.BlockSpec((CHUNK // 2, CHUNK // 2), lambda s, i: (0, 0)),
            pl.BlockSpec((1, tm, 3 * LANES), lambda s, i: (jnp.minimum(s, 1), i, 0)),
        ] + [pl.BlockSpec(memory_space=pl.ANY) for _ in cache_in],
        out_specs=(
            pl.BlockSpec((1, tm, 6 * CHUNK), lambda s, i: (s, i, 0)),
            pl.BlockSpec((1, tm, 2 * D_MODEL), lambda s, i: (s, i, 0)),
        ) + tuple(cache_spec(shape) for shape in CACHE_SHAPES),
        input_output_aliases={n_in + j: 2 + j for j in range(len(cache_in))},
        compiler_params=_params("arbitrary", "arbitrary"),
        name="in_projection",
    )(x_ctx, x_lat, mods, g_norm1, w_in_b, head_gains, bd, rope_tab, *cache_in)


def _lam_value(lam_ref):
    lp = lam_ref[0]
    s1 = jnp.sum(lp[0:1] * lp[1:2], axis=-1, keepdims=True)
    s2 = jnp.sum(lp[2:3] * lp[3:4], axis=-1, keepdims=True)
    lam_init = lp[4:5, 0:1]
    return jnp.exp(s1) - jnp.exp(s2) + lam_init, 1.0 - lam_init


def _half_masks():
    lane = lax.broadcasted_iota(jnp.int32, (1, LANES), 1)
    return (lane < HEAD_DIM, lane >= HEAD_DIM)


def _pick_head(mask, t):
    return jnp.where(mask, t, jnp.zeros_like(t))


def _sub_layer_norm(o, gsub, one_minus):
    ms = jnp.mean(o * o, axis=-1, keepdims=True)
    return o * lax.rsqrt(ms + EPS) * gsub * one_minus


def _attn_ctx_kernel(qkv_ref, lam_ref, gsub_ref, o_ref):
    lam, one_minus = _lam_value(lam_ref)
    masks = _half_masks()
    qda = qkv_ref[0, :, 0:CHUNK]
    qna = qkv_ref[0, :, 3 * CHUNK:4 * CHUNK]
    gsub = gsub_ref[0]

    def kv(c0):
        off = c0 + CHUNK if c0 < 2 * CHUNK else c0 + 2 * CHUNK
        return qkv_ref[0, :, off:off + LANES]

    def softmax_parts(sc):
        m = jnp.max(sc, axis=-1, keepdims=True)
        e = jnp.exp2(sc - m)
        return e, 1.0 / jnp.sum(e, axis=-1, keepdims=True)

    for h in range(DA_HEADS):
        p, sub = divmod(h, 2)
        q1 = _pick_head(masks[sub], qda[:, p * LANES:(p + 1) * LANES])
        q2 = _pick_head(masks[sub], qda[:, 2 * LANES + p * LANES:2 * LANES + (p + 1) * LANES])
        e1, r1 = softmax_parts(_dot_nt(q1, kv(p * LANES)))
        e2, r2 = softmax_parts(_dot_nt(q2, kv(2 * LANES + p * LANES)))
        a = e1 * r1 - e2 * (lam * r2)
        o = _dot(a.astype(BF16), kv(CHUNK + h * LANES))
        o_ref[:, h * LANES:(h + 1) * LANES] = _sub_layer_norm(o, gsub, one_minus).astype(BF16)

    for p in range(NA_HEADS // 2):
        kp = kv(2 * CHUNK + p * LANES)
        vp = kv(3 * CHUNK + p * LANES)
        both = None
        for sub in range(2):
            qm = _pick_head(masks[sub], qna[:, p * LANES:(p + 1) * LANES])
            e, r = softmax_parts(_dot_nt(qm, kp))
            o = _dot(e.astype(BF16), vp) * r
            both = o if sub == 0 else jnp.where(masks[1], o, both)
        o_ref[:, DA_WIDTH + p * LANES:DA_WIDTH + (p + 1) * LANES] = both.astype(BF16)


def _attention_ctx(l, qkv, lam_tab, g_subln):
    return pl.pallas_call(
        _attn_ctx_kernel,
        out_shape=jax.ShapeDtypeStruct((SEG_ROWS, D_MODEL), BF16),
        grid=(BATCH,),
        in_specs=[
            pl.BlockSpec((1, SEQ, 6 * CHUNK), lambda b: (0, b, 0)),
            pl.BlockSpec((1, 8, LANES), lambda b: (l, 0, 0)),
            pl.BlockSpec((1, 1, LANES), lambda b: (l, 0, 0)),
        ],
        out_specs=pl.BlockSpec((SEQ, D_MODEL), lambda b: (b, 0)),
        compiler_params=_params("arbitrary"),
        name="attention_ctx",
    )(qkv, lam_tab, g_subln)


DA_KEYS = DEC_SEQ + PAST_LEN
DA_KCHUNK = 1024
DA_TQ = 256


def _pair_heads(head_ref, lead, first):
    a = head_ref[lead + (first,)]
    b = head_ref[lead + (first + 1,)]
    return jnp.concatenate([a, b], axis=0).T.astype(BF16)


def _attn_da_kernel(q_ref, k_ref, v_ref, kc_ref, vc_ref, lam_ref, gsub_ref, o_ref, vt_ref, kcs_ref, e1_ref, e2_ref):
    chunks = [slice(r, r + DA_KCHUNK) for r in range(0, DEC_SEQ, DA_KCHUNK)] + [slice(DEC_SEQ, DA_KEYS)]
    n_chunks = len(chunks)

    @pl.when(pl.program_id(1) == 0)
    def _():
        for r in range(0, DEC_SEQ, PAST_LEN):
            rows = slice(r, r + PAST_LEN)
            vt_ref[:, rows] = v_ref[0, rows, :].astype(F32).T.astype(BF16)
        for h in range(DA_HEADS):
            vt_ref[h * LANES:(h + 1) * LANES, DEC_SEQ:DA_KEYS] = vc_ref[0, 0, h].T.astype(BF16)
        for m in range(2):
            for p in range(DA_HEADS // 2):
                c0 = m * 2 * LANES + p * LANES
                kcs_ref[:, c0:c0 + LANES] = _pair_heads(kc_ref, (0, 0, m), 2 * p)

    lam, one_minus = _lam_value(lam_ref)
    masks = _half_masks()
    q = q_ref[0]
    gsub = gsub_ref[0]

    def key_chunk(c, cols):
        if chunks[c].start < DEC_SEQ:
            return k_ref[0, chunks[c], cols]
        return kcs_ref[:, cols]

    def score_map(qm, cols, e_ref, weight):
        maxes, sums = [], []
        for c in range(n_chunks):
            s = _dot_nt(key_chunk(c, cols), qm)
            mc = jnp.max(s, axis=0, keepdims=True)
            e = jnp.exp2(s - mc)
            sums.append(jnp.sum(e, axis=0, keepdims=True))
            maxes.append(mc)
            e_ref[chunks[c], :] = e.astype(BF16)
        m = functools.reduce(jnp.maximum, maxes)
        facs = [jnp.exp2(mc - m) for mc in maxes]
        tot = functools.reduce(lambda a, b: a + b, [lc * fc for lc, fc in zip(sums, facs)])
        w = weight / tot
        return [fc * w for fc in facs]

    for h in range(DA_HEADS):
        p, sub = divmod(h, 2)
        c1 = slice(p * LANES, (p + 1) * LANES)
        c2 = slice(2 * LANES + p * LANES, 2 * LANES + (p + 1) * LANES)
        f1 = score_map(_pick_head(masks[sub], q[:, c1]), c1, e1_ref, 1.0)
        f2 = score_map(_pick_head(masks[sub], q[:, c2]), c2, e2_ref, lam)
        o_t = jnp.zeros((LANES, DA_TQ), F32)
        for c, rows in enumerate(chunks):
            vt = vt_ref[h * LANES:(h + 1) * LANES, rows]
            o_t = o_t + _dot(vt, e1_ref[rows, :]) * f1[c] - _dot(vt, e2_ref[rows, :]) * f2[c]
        o_ref[0, :, h * LANES:(h + 1) * LANES] = _sub_layer_norm(o_t.T, gsub, one_minus).astype(BF16)


def _attention_da(l, qkv, ctx_k, ctx_v, lam_tab, g_subln):
    tq = DA_TQ
    return pl.pallas_call(
        _attn_da_kernel,
        out_shape=jax.ShapeDtypeStruct((DEC_BATCH, DEC_SEQ, DA_WIDTH), BF16),
        grid=(DEC_BATCH, DEC_SEQ // tq),
        in_specs=[
            pl.BlockSpec((1, tq, CHUNK), lambda b, i: (b + 1, i, 0)),
            pl.BlockSpec((1, DEC_SEQ, CHUNK), lambda b, i: (b + 1, 0, 1), pipeline_mode=pl.Buffered(1)),
            pl.BlockSpec((1, DEC_SEQ, CHUNK), lambda b, i: (b + 1, 0, 2), pipeline_mode=pl.Buffered(1)),
            pl.BlockSpec((1, 1, 2, DA_HEADS, HEAD_DIM, PAST_LEN), lambda b, i: (b, l, 0, 0, 0, 0)),
            pl.BlockSpec((1, 1, DA_HEADS, PAST_LEN, 2 * HEAD_DIM), lambda b, i: (b, l, 0, 0, 0)),
            pl.BlockSpec((1, 8, LANES), lambda b, i: (l, 0, 0)),
            pl.BlockSpec((1, 1, LANES), lambda b, i: (l, 0, 0)),
        ],
        out_specs=pl.BlockSpec((1, tq, DA_WIDTH), lambda b, i: (b, i, 0)),
        scratch_shapes=[
            pltpu.VMEM((DA_WIDTH, DA_KEYS), BF16),
            pltpu.VMEM((PAST_LEN, DA_WIDTH), BF16),
            pltpu.VMEM((DA_KEYS, tq), BF16),
            pltpu.VMEM((DA_KEYS, tq), BF16),
        ],
        compiler_params=_params("arbitrary", "arbitrary"),
        name="attention_da",
    )(qkv, qkv, qkv, ctx_k, ctx_v, lam_tab, g_subln)


NA_QROWS = 8
NA_KROWS = 16


def _attn_na_kernel(q_ref, k_ref, v_ref, kc_ref, vc_ref, bias_ref, o_ref, kcs_ref, vcs_ref):
    i = pl.program_id(1)

    @pl.when(i == 0)
    def _():
        for p in range(NA_HEADS // 2):
            kcs_ref[:, p * LANES:(p + 1) * LANES] = _pair_heads(kc_ref, (0, 0), 2 * p)
            vcs_ref[:, p * LANES:(p + 1) * LANES] = _pair_heads(vc_ref, (0, 0), 2 * p)

    masks = _half_masks()
    q = q_ref[0]
    row0 = jnp.clip(i * NA_QROWS - NA_ROWS // 2, 0, GRID_H - NA_KROWS)
    start = pl.multiple_of(row0 * GRID_W, 256)
    nk = NA_KROWS * GRID_W

    for p in range(NA_HEADS // 2):
        c0 = p * LANES
        kw = k_ref[0, pl.ds(start, nk), c0:c0 + LANES]
        vw = v_ref[0, pl.ds(start, nk), c0:c0 + LANES]
        kc = kcs_ref[:, c0:c0 + LANES]
        vc = vcs_ref[:, c0:c0 + LANES]
        both = None
        for sub in range(2):
            qm = _pick_head(masks[sub], q[:, c0:c0 + LANES])
            s_loc = _dot_nt(qm, kw) + bias_ref[0, 0, 2 * p + sub].astype(F32)
            s_ctx = _dot_nt(qm, kc)
            m = jnp.maximum(jnp.max(s_loc, axis=-1, keepdims=True), jnp.max(s_ctx, axis=-1, keepdims=True))
            e_loc = jnp.exp2(s_loc - m)
            e_ctx = jnp.exp2(s_ctx - m)
            tot = jnp.sum(e_loc, axis=-1, keepdims=True) + jnp.sum(e_ctx, axis=-1, keepdims=True)
            o = (_dot(e_loc.astype(BF16), vw) + _dot(e_ctx.astype(BF16), vc)) * (1.0 / tot)
            both = o if sub == 0 else jnp.where(masks[1], o, both)
        o_ref[0, :, c0:c0 + LANES] = both.astype(BF16)


def _attention_na(l, qkv, ctx_k, ctx_v, bias):
    tq = NA_QROWS * GRID_W
    n_i = DEC_SEQ // tq

    def bias_idx(b, i):
        return (l, jnp.where(i == 0, 0, jnp.where(i == n_i - 1, 2, 1)), 0, 0, 0)

    return pl.pallas_call(
        _attn_na_kernel,
        out_shape=jax.ShapeDtypeStruct((DEC_BATCH, DEC_SEQ, NA_WIDTH), BF16),
        grid=(DEC_BATCH, n_i),
        in_specs=[
            pl.BlockSpec((1, tq, CHUNK), lambda b, i: (b + 1, i, 3)),
            pl.BlockSpec((1, DEC_SEQ, CHUNK), lambda b, i: (b + 1, 0, 4), pipeline_mode=pl.Buffered(1)),
            pl.BlockSpec((1, DEC_SEQ, CHUNK), lambda b, i: (b + 1, 0, 5), pipeline_mode=pl.Buffered(1)),
            pl.BlockSpec((1, 1, NA_HEADS, HEAD_DIM, PAST_LEN), lambda b, i: (b, l, 0, 0, 0)),
            pl.BlockSpec((1, 1, NA_HEADS, HEAD_DIM, PAST_LEN), lambda b, i: (b, l, 0, 0, 0)),
            pl.BlockSpec((1, 1, NA_HEADS, tq, NA_KROWS * GRID_W), bias_idx),
        ],
        out_specs=pl.BlockSpec((1, tq, NA_WIDTH), lambda b, i: (b, i, 0)),
        scratch_shapes=[
            pltpu.VMEM((PAST_LEN, NA_WIDTH), BF16),
            pltpu.VMEM((PAST_LEN, NA_WIDTH), BF16),
        ],
        compiler_params=_params("arbitrary", "arbitrary"),
        name="attention_na",
    )(qkv, qkv, qkv, ctx_k, ctx_v, bias)


def _na_bias_tables(rpb):
    qr = np.arange(NA_QROWS)
    qc = np.arange(GRID_W)
    kr = np.arange(NA_KROWS)
    kc = np.arange(GRID_W)
    c0 = np.clip(qc - NA_COLS // 2, 0, GRID_W - NA_COLS)
    col_ok = (kc[None, :] >= c0[:, None]) & (kc[None, :] < c0[:, None] + NA_COLS)
    cpad = GRID_W - NA_COLS
    rpb_c = jnp.pad(rpb, ((0, 0), (0, 0), (0, 0), (cpad, cpad)))
    by_col = jnp.stack([rpb_c[..., GRID_W - 1 - c:2 * GRID_W - 1 - c] for c in range(GRID_W)], axis=-2)
    by_col = jnp.where(col_ok, by_col * LOG2E, -jnp.inf)

    plans = []
    for i in (0, 1, GRID_H // NA_QROWS - 1):
        rows_q = NA_QROWS * i + qr
        row0_k = min(max(NA_QROWS * i - NA_ROWS // 2, 0), GRID_H - NA_KROWS)
        rows_k = row0_k + kr
        r0 = np.clip(rows_q - NA_ROWS // 2, 0, GRID_H - NA_ROWS)
        row_ok = (rows_k[None, :] >= r0[:, None]) & (rows_k[None, :] < r0[:, None] + NA_ROWS)
        slab = rows_k[None, :] - rows_q[:, None] + NA_ROWS - 1
        plans.append([[int(slab[r, c]) if row_ok[r, c] else None for c in range(NA_KROWS)]
                      for r in range(NA_QROWS)])

    heads_per_step = NA_HEADS // 2

    def tile_kernel(tab_ref, o_ref):
        kind = pl.program_id(1)
        for which, plan in enumerate(plans):
            @pl.when(kind == which)
            def _(plan=plan):
                for h in range(heads_per_step):
                    for r in range(NA_QROWS):
                        for c in range(NA_KROWS):
                            if plan[r][c] is None:
                                blk = jnp.full((GRID_W, GRID_W), -jnp.inf, BF16)
                            else:
                                blk = tab_ref[0, h, plan[r][c]].astype(BF16)
                            o_ref[0, 0, h, r * GRID_W:(r + 1) * GRID_W, c * GRID_W:(c + 1) * GRID_W] = blk

    n_slab = 2 * NA_ROWS - 1
    return pl.pallas_call(
        tile_kernel,
        out_shape=jax.ShapeDtypeStruct((DEPTH, len(plans), NA_HEADS, NA_QROWS * GRID_W, NA_KROWS * GRID_W), BF16),
        grid=(DEPTH, len(plans), NA_HEADS // heads_per_step),
        in_specs=[pl.BlockSpec((1, heads_per_step, n_slab, GRID_W, GRID_W), lambda l, t, h: (l, h, 0, 0, 0))],
        out_specs=pl.BlockSpec((1, 1, heads_per_step, NA_QROWS * GRID_W, NA_KROWS * GRID_W),
                               lambda l, t, h: (l, t, h, 0, 0)),
        compiler_params=_params("arbitrary", "arbitrary", "arbitrary"),
        name="na_bias_tiles",
    )(by_col)


def _first_max_of_four(vals):
    a, b, c, d = vals
    m = jnp.maximum(jnp.maximum(a, b), jnp.maximum(c, d))
    idx = jnp.where(a == m, 0.0, jnp.where(b == m, 1.0, jnp.where(c == m, 2.0, 3.0)))
    return m, idx


MOE_CHUNK = EXPERTS_PER_GROUP * D_EXPERT


def _out_kernel(xc_ref, xl_ref, mod_ref, oc_ref, oda_ref, ona_ref, gate_ref, wa_ref, wb_ref, wo_ref, g2_ref,
                wrt_ref, br_ref, wg_ref, wu_ref, wd_ref, yc_ref, yl_ref):
    is_ctx = pl.program_id(0) == 0
    oc = oc_ref[0]
    oa = jnp.where(is_ctx, oc[:, :DA_WIDTH], oda_ref[0])
    ob = jnp.where(is_ctx, oc[:, DA_WIDTH:], ona_ref[0])
    ya = _dot(oa, wa_ref[0])
    yb = _dot(ob, wb_ref[0])
    g = jax.nn.sigmoid(gate_ref[0].astype(F32))
    mixed = (g[:, :D_MODEL] * ya + g[:, D_MODEL:] * yb).astype(BF16)
    mod = mod_ref[0]
    x1 = jnp.where(is_ctx, xc_ref[0], xl_ref[0]) + mod[2:3] * _dot(mixed, wo_ref[0])

    ms = jnp.mean(x1 * x1, axis=-1, keepdims=True)
    h2 = x1 * lax.rsqrt(ms + EPS) * g2_ref[0] * (1.0 + mod[4:5]) + mod[3:4]
    h2_hi = h2.astype(BF16)

    h2_lo = (h2 - h2_hi.astype(F32)).astype(BF16)
    wr = wrt_ref[...]
    wr_hi = wr.astype(BF16)
    wr_lo = (wr - wr_hi.astype(F32)).astype(BF16)
    by_hi = _dot_nt(jnp.concatenate([wr_hi, wr_lo], axis=0), h2_hi)
    logits = by_hi[:N_EXPERTS] + (by_hi[N_EXPERTS:] + _dot_nt(wr_hi, h2_lo))
    scores = jax.nn.sigmoid(logits)
    sel = scores + br_ref[...]
    score_rows = [scores[e:e + 1] for e in range(N_EXPERTS)]
    sel_rows = [sel[e:e + 1] for e in range(N_EXPERTS)]

    neg = -jnp.inf
    best = None
    for grp in range(N_GROUPS):
        vals = sel_rows[grp * EXPERTS_PER_GROUP:(grp + 1) * EXPERTS_PER_GROUP]
        m1, i1 = _first_max_of_four(vals)
        rest = [jnp.where(i1 == float(j), neg, v) for j, v in enumerate(vals)]
        m2, i2 = _first_max_of_four(rest)
        cand = (m1 + m2, i1 + float(grp * EXPERTS_PER_GROUP), i2 + float(grp * EXPERTS_PER_GROUP))
        if best is None:
            best = cand
        else:
            upd = cand[0] > best[0]
            best = tuple(jnp.where(upd, n, o) for n, o in zip(cand, best))
    _, e1, e2 = best
    hits1 = [e1 == float(e) for e in range(N_EXPERTS)]
    hits2 = [e2 == float(e) for e in range(N_EXPERTS)]
    s1 = functools.reduce(lambda a, b: a + b, [jnp.where(h, r, 0.0) for h, r in zip(hits1, score_rows)])
    s2 = functools.reduce(lambda a, b: a + b, [jnp.where(h, r, 0.0) for h, r in zip(hits2, score_rows)])
    den = s1 + s2
    w1 = s1 / den
    w2 = s2 / den
    rows = [jnp.where(h1, w1, 0.0) + jnp.where(h2_, w2, 0.0) for h1, h2_ in zip(hits1, hits2)]
    rows.append(jnp.zeros((LANES - N_EXPERTS, rows[0].shape[1]), F32))
    comb = jnp.concatenate(rows, axis=0).T

    acc = jnp.zeros(x1.shape, F32)
    for c in range(N_GROUPS):
        parts = []
        for j in range(EXPERTS_PER_GROUP):
            e = c * EXPERTS_PER_GROUP + j
            g = _dot(h2_hi, wg_ref[0, e])
            u = _dot(h2_hi, wu_ref[0, e])
            parts.append(g * jax.nn.sigmoid(g) * u * comb[:, e:e + 1])
        hs = jnp.concatenate(parts, axis=1).astype(BF16)
        acc = acc + _dot(hs, wd_ref[0, c * MOE_CHUNK:(c + 1) * MOE_CHUNK, :])
    out = x1 + mod[5:6] * acc

    @pl.when(is_ctx)
    def _():
        yc_ref[0] = out

    @pl.when(jnp.logical_not(is_ctx))
    def _():
        yl_ref[0] = out


def _output_stage(l, x_ctx, x_lat, mods, o_ctx, o_da, o_na, gates, wa_b, wb_b, wo_b, g_norm2, w_router_t,
                  b_router_c, wg_all, wu_all, wd_all):
    tm = 256
    nt = SEG_ROWS // tm
    hidden = N_EXPERTS * D_EXPERT
    once = pl.Buffered(1)
    return pl.pallas_call(
        _out_kernel,
        out_shape=(
            jax.ShapeDtypeStruct((1, SEG_ROWS, D_MODEL), F32),
            jax.ShapeDtypeStruct((DEC_BATCH, DEC_SEQ, D_MODEL), F32),
        ),
        grid=(N_SEG, nt),
        in_specs=[
            _ctx_rows_spec(tm, D_MODEL),
            _lat_rows_spec(tm, D_MODEL),
            pl.BlockSpec((1, N_MOD, D_MODEL), lambda s, i: (s, 0, 0)),
            _ctx_rows_spec(tm, D_MODEL),
            _lat_rows_spec(tm, DA_WIDTH),
            _lat_rows_spec(tm, NA_WIDTH),
            pl.BlockSpec((1, tm, 2 * D_MODEL), lambda s, i: (s, i, 0)),
            pl.BlockSpec((1, DA_WIDTH, D_MODEL), lambda s, i: (l, 0, 0)),
            pl.BlockSpec((1, NA_WIDTH, D_MODEL), lambda s, i: (l, 0, 0)),
            pl.BlockSpec((1, D_MODEL, D_MODEL), lambda s, i: (l, 0, 0)),
            pl.BlockSpec((1, 1, D_MODEL), lambda s, i: (l, 0, 0)),
            pl.BlockSpec((N_EXPERTS, D_MODEL), lambda s, i: (0, 0)),
            pl.BlockSpec((N_EXPERTS, 1), lambda s, i: (0, 0)),
            pl.BlockSpec((1, N_EXPERTS, D_MODEL, D_EXPERT), lambda s, i: (l, 0, 0, 0), pipeline_mode=once),
            pl.BlockSpec((1, N_EXPERTS, D_MODEL, D_EXPERT), lambda s, i: (l, 0, 0, 0), pipeline_mode=once),
            pl.BlockSpec((1, hidden, D_MODEL), lambda s, i: (l, 0, 0), pipeline_mode=once),
        ],
        out_specs=(_ctx_rows_spec(tm, D_MODEL), _lat_rows_spec(tm, D_MODEL)),
        compiler_params=_params("arbitrary", "arbitrary"),
        name="output_experts",
    )(x_ctx, x_lat, mods, o_ctx, o_da, o_na, gates, wa_b, wb_b, wo_b, g_norm2, w_router_t, b_router_c,
      wg_all, wu_all, wd_all)


def _rope_tables():
    t = np.arange(DEC_SEQ)
    pos = np.stack([t // GRID_W, t % GRID_W], axis=-1).astype(np.float32)
    inv_freq = (ROPE_BASE ** (-np.arange(ROPE_PAIRS, dtype=np.float32) / ROPE_PAIRS)).astype(np.float32)
    ang = pos[:, :, None] * inv_freq
    lane = np.arange(LANES)
    axis = (lane % HEAD_DIM) // (2 * ROPE_PAIRS)
    pair = lane % ROPE_PAIRS
    second = ((lane // ROPE_PAIRS) % 2).astype(bool)
    a = ang[:, axis, pair]
    cos = np.cos(a)
    sin = np.sin(a)
    lat = np.concatenate([cos, np.where(second, 0.0, -sin), np.where(second, sin, 0.0)], axis=1)
    ident = np.concatenate([np.ones_like(cos), np.zeros_like(cos), np.zeros_like(cos)], axis=1)
    return jnp.asarray(np.stack([ident, lat]).astype(np.float32))


def _head_block_diag():
    r = np.arange(CHUNK // 2) // HEAD_DIM
    return jnp.asarray((r[:, None] == r[None, :]).astype(np.float32), dtype=BF16)


def kernel(x_prompt, x_sample, cache_da_k, cache_da_v, cache_na_k, cache_na_v, c, c_ctx, w_mod, b_mod, g_norm1,
           g_norm2, w_in, g_q_da, g_k_da, g_q_na, g_k_na, lam_q1, lam_k1, lam_q2, lam_k2, g_subln, rpb, w_br_a,
           w_br_b, w_out, w_router, b_router, w_gate, w_up, w_down):
    L = DEPTH
    x_ctx = x_prompt.reshape(1, SEG_ROWS, D_MODEL)
    x_lat = x_sample
    cond8 = jnp.concatenate([c_ctx[None], c, jnp.zeros((8 - N_SEG, D_MODEL), F32)], axis=0)
    w_in_b = w_in.astype(BF16)
    wa_b = w_br_a.astype(BF16)
    wb_b = w_br_b.astype(BF16)
    wo_b = w_out.astype(BF16)
    hidden = N_EXPERTS * D_EXPERT
    wg_all = w_gate.astype(BF16)
    wu_all = w_up.astype(BF16)
    wd_all = w_down.astype(BF16).reshape(L, hidden, D_MODEL)
    w_router_t = w_router.T
    b_router_c = b_router.reshape(N_EXPERTS, 1)
    head_gains = jnp.stack([jnp.tile(g, (1, CHUNK // HEAD_DIM)) for g in (g_q_da, g_k_da, g_q_na, g_k_na)],
                           axis=1).reshape(L, 4, CHUNK)
    g_norm1 = g_norm1.reshape(L, 1, D_MODEL)
    g_norm2 = g_norm2.reshape(L, 1, D_MODEL)
    g_subln = g_subln.reshape(L, 1, LANES)
    pad64 = lambda t: jnp.pad(t, ((0, 0), (0, LANES - HEAD_DIM)))
    lam_inits = jnp.asarray([0.8 - 0.6 * math.exp(-0.3 * l) for l in range(L)], F32)
    lam_tab = jnp.stack([pad64(lam_q1), pad64(lam_k1), pad64(lam_q2), pad64(lam_k2),
                         jnp.broadcast_to(lam_inits[:, None], (L, LANES)),
                         jnp.zeros((L, LANES), F32), jnp.zeros((L, LANES), F32), jnp.zeros((L, LANES), F32)],
                        axis=1)
    cache_da_k_t = jnp.swapaxes(cache_da_k, -1, -2)
    cache_na_k_t = jnp.swapaxes(cache_na_k, -1, -2)
    cache_na_v_t = jnp.swapaxes(cache_na_v, -1, -2)
    na_bias = _na_bias_tables(rpb)
    rope_tab = _rope_tables()
    bd = _head_block_diag()

    mods_all = _modulation(cond8, w_mod, b_mod)[:, :N_SEG].reshape(L, N_SEG, N_MOD, D_MODEL)

    caches = None
    for l in range(L):
        mods = mods_all[l]
        qkv, gates, *caches = _in_projection(l, x_ctx, x_lat, mods, g_norm1, w_in_b, head_gains, bd, rope_tab,
                                             caches)
        o_ctx = _attention_ctx(l, qkv, lam_tab, g_subln).reshape(1, SEG_ROWS, D_MODEL)
        o_da = _attention_da(l, qkv, cache_da_k_t, cache_da_v, lam_tab, g_subln)
        o_na = _attention_na(l, qkv, cache_na_k_t, cache_na_v_t, na_bias)
        x_ctx, x_lat = _output_stage(l, x_ctx, x_lat, mods, o_ctx, o_da, o_na, gates, wa_b, wb_b, wo_b, g_norm2,
                                     w_router_t, b_router_c, wg_all, wu_all, wd_all)

    y_prompt = x_ctx.reshape(BATCH, SEQ, D_MODEL)
    y_sample = x_lat
    da_k_t, new_da_v, na_k_t, na_v_t = caches
    return (y_prompt, y_sample, jnp.swapaxes(da_k_t, -1, -2), new_da_v, jnp.swapaxes(na_k_t, -1, -2),
            jnp.swapaxes(na_v_t, -1, -2))
```
